```python
import jax
import jax.numpy as jnp
from jax import lax
import numpy as np

D_MODEL = 2048
BATCH = 2
SEQ = 16384
DEPTH = 2

CHUNK = 64
EPS = 1e-6
A_HEADS = 6
A_EXPAND = 128
A_HEAD_V = 128
A_DK = A_HEADS * A_EXPAND
A_DV = A_HEADS * A_HEAD_V
B_WIDTH = 512
CONV_W = 3
C_HEADS = 6
C_HEAD_DIM = 128
C_DIM = C_HEADS * C_HEAD_DIM
LEFT_CHUNKS = 8
BAND = (LEFT_CHUNKS + 1) * CHUNK
REL_CLIP = 256
N_REL = 2 * REL_CLIP + 1
N_BRANCH = 3
IN_SIZES = (A_DK, A_DK, A_DV, A_DV, B_WIDTH, B_WIDTH, B_WIDTH, C_DIM, C_DIM, C_DIM, N_BRANCH * D_MODEL)
IN_COLS = sum(IN_SIZES)
N_EXPERTS = 64
TOP_K = 8
N_GROUPS = 8
TOPK_GROUPS = 4
D_EXPERT = 512
D_SHARED = 512
ROUTED_SCALE = 2.5
MOE_BLOCK = 128
N_MOD = 6

kernel_name = 'hybrid_stream_hgrn2_conv_chunkattn_moe'


def _split(t, sizes):
    offs = [int(o) for o in np.cumsum(sizes)[:-1]]
    return jnp.split(t, offs, axis=-1)


def _rms(x, g):
    xf = x.astype(jnp.float32)
    y = xf * lax.rsqrt(jnp.mean(xf * xf, axis=-1, keepdims=True) + EPS)
    return y.astype(x.dtype) * g


def _hgrn2_chunk_scan(q, k, v, log_f):
    bsz, seq, nh, dk = q.shape
    dv = v.shape[-1]
    nc = seq // CHUNK

    def to_chunks(t):
        return t.reshape(bsz, nc, CHUNK, nh, t.shape[-1]).transpose(1, 0, 3, 2, 4).astype(jnp.float32)

    tri = jnp.tril(jnp.ones((CHUNK, CHUNK), dtype=bool))[:, :, None]

    def step(state, inp):
        qc, kc, vc, gc = inp
        b = jnp.cumsum(gc, axis=2)
        b_last = b[:, :, -1:, :]
        inter = jnp.einsum('bhtk,bhkv->bhtv', qc * jnp.exp(b), state)
        decay = jnp.exp(jnp.where(tri, b[:, :, :, None, :] - b[:, :, None, :, :], -jnp.inf))
        scores = jnp.sum(qc[:, :, :, None, :] * kc[:, :, None, :, :] * decay, axis=-1)
        intra = jnp.einsum('bhts,bhsv->bhtv', scores, vc)
        new_state = jnp.exp(b_last[:, :, 0, :])[..., None] * state + jnp.einsum(
            'bhsk,bhsv->bhkv', kc * jnp.exp(b_last - b), vc)
        return new_state, inter + intra

    s0 = jnp.zeros((bsz, nh, dk, dv), jnp.float32)
    _, o = lax.scan(step, s0, (to_chunks(q), to_chunks(k), to_chunks(v), to_chunks(log_f)))
    return o.transpose(1, 0, 3, 2, 4).reshape(bsz, seq, nh, dv)


def _chunk_attention(q, k, v, rel_bias):
    bsz, seq, _ = q.shape
    nc = seq // CHUNK
    pad = LEFT_CHUNKS * CHUNK
    qc = q.reshape(bsz, nc, CHUNK, C_HEADS, C_HEAD_DIM).transpose(1, 0, 2, 3, 4)
    kp = jnp.pad(k.reshape(bsz, seq, C_HEADS, C_HEAD_DIM), ((0, 0), (pad, 0), (0, 0), (0, 0)))
    vp = jnp.pad(v.reshape(bsz, seq, C_HEADS, C_HEAD_DIM), ((0, 0), (pad, 0), (0, 0), (0, 0)))
    q_pos = jnp.arange(CHUNK) + pad
    k_pos = jnp.arange(BAND)
    rel = jnp.clip(q_pos[:, None] - k_pos[None, :], -REL_CLIP, REL_CLIP) + REL_CLIP
    bias = rel_bias.astype(jnp.float32)[:, rel]
    scale = C_HEAD_DIM ** -0.5

    def one_chunk(args):
        j, qj = args
        kb = lax.dynamic_slice_in_dim(kp, j * CHUNK, BAND, axis=1)
        vb = lax.dynamic_slice_in_dim(vp, j * CHUNK, BAND, axis=1)
        s = jnp.einsum('bthd,bshd->bhts', qj, kb).astype(jnp.float32) * scale + bias
        valid = (j * CHUNK + k_pos) >= pad
        s = jnp.where(valid[None, None, None, :], s, -jnp.inf)
        p = jax.nn.softmax(s, axis=-1).astype(vb.dtype)
        return jnp.einsum('bhts,bshd->bthd', p, vb)

    o = lax.map(one_chunk, (jnp.arange(nc), qc))
    return o.transpose(1, 0, 2, 3, 4).reshape(bsz, seq, C_DIM)


def _mixer(h, lb, w_in, onorm_g, conv_w, rel_bias, w_a, w_b, w_c, w_out):
    bsz, seq, _ = h.shape
    proj = h @ w_in
    aq, af, ai, ag, bb, bc, bu, cq, ck, cv, gates = _split(proj, IN_SIZES)
    log_f = jnp.logaddexp(jnp.log(lb), jnp.log1p(-lb) + jax.nn.log_sigmoid(af.astype(jnp.float32)))
    key_a = -jnp.expm1(log_f)
    o = _hgrn2_chunk_scan(
        aq.reshape(bsz, seq, A_HEADS, A_EXPAND) * A_EXPAND ** -0.5,
        key_a.reshape(bsz, seq, A_HEADS, A_EXPAND),
        ai.reshape(bsz, seq, A_HEADS, A_HEAD_V),
        log_f.reshape(bsz, seq, A_HEADS, A_EXPAND))
    o = o * lax.rsqrt(jnp.mean(o * o, axis=-1, keepdims=True) + EPS)
    y_a = (o.reshape(bsz, seq, A_DV).astype(h.dtype) * onorm_g * jax.nn.silu(ag)) @ w_a
    u = bc * bu
    up = jnp.pad(u, ((0, 0), (CONV_W - 1, 0), (0, 0)))
    conv = up[:, 0:seq] * conv_w[0]
    for j in range(1, CONV_W):
        conv = conv + up[:, j:j + seq] * conv_w[j]
    y_b = (bb * conv) @ w_b
    y_c = _chunk_attention(cq, ck, cv, rel_bias) @ w_c
    g = jax.nn.sigmoid(gates.astype(jnp.float32)).astype(h.dtype).reshape(bsz, seq, N_BRANCH, D_MODEL)
    merged = g[:, :, 0] * y_a + g[:, :, 1] * y_b + g[:, :, 2] * y_c
    return merged @ w_out


def _swiglu(t, wg, wu, wd):
    return (jax.nn.silu(t @ wg) * (t @ wu)) @ wd


def _moe(h, router_w, router_bias, w_gate, w_up, w_down, sw_gate, sw_up, sw_down):
    bsz, seq, d = h.shape
    t = h.reshape(bsz * seq, d)
    n_tok = t.shape[0]
    scores = jax.nn.sigmoid((t @ router_w).astype(jnp.float32))
    sel = scores + router_bias.astype(jnp.float32)
    grp = sel.reshape(n_tok, N_GROUPS, N_EXPERTS // N_GROUPS)
    grp_score = jnp.sum(lax.top_k(grp, 2)[0], axis=-1)
    _, gidx = lax.top_k(grp_score, TOPK_GROUPS)
    gmask = jnp.any(gidx[:, :, None] == jnp.arange(N_GROUPS)[None, None, :], axis=1)
    emask = jnp.repeat(gmask, N_EXPERTS // N_GROUPS, axis=1)
    _, eidx = lax.top_k(jnp.where(emask, sel, -jnp.inf), TOP_K)
    wts = jnp.take_along_axis(scores, eidx, axis=1)
    wts = wts / jnp.sum(wts, axis=-1, keepdims=True) * ROUTED_SCALE
    n_asg = n_tok * TOP_K
    flat_e = eidx.reshape(-1)
    flat_tok = jnp.arange(n_asg, dtype=jnp.int32) // TOP_K
    order = jnp.argsort(flat_e, stable=True)
    se = flat_e[order]
    stok = flat_tok[order]
    sw = wts.reshape(-1)[order]
    counts = jnp.bincount(flat_e, length=N_EXPERTS)
    starts = jnp.cumsum(counts) - counts
    padded = (counts + MOE_BLOCK - 1) // MOE_BLOCK * MOE_BLOCK
    pends = jnp.cumsum(padded)
    pstarts = pends - padded
    dest = pstarts[se] + jnp.arange(n_asg) - starts[se]
    n_blocks = -(-n_asg // MOE_BLOCK) + N_EXPERTS
    buf_tok = jnp.zeros((n_blocks * MOE_BLOCK,), jnp.int32).at[dest].set(stok)
    buf_w = jnp.zeros((n_blocks * MOE_BLOCK,), jnp.float32).at[dest].set(sw)
    blk_exp = jnp.minimum(
        jnp.searchsorted(pends, jnp.arange(n_blocks) * MOE_BLOCK, side='right'), N_EXPERTS - 1)

    def block_step(acc, blk):
        toks, bw, e = blk
        y = _swiglu(t[toks], w_gate[e], w_up[e], w_down[e])
        return acc.at[toks].add(y * bw[:, None].astype(y.dtype)), None

    routed, _ = lax.scan(block_step, jnp.zeros_like(t),
                         (buf_tok.reshape(n_blocks, MOE_BLOCK), buf_w.reshape(n_blocks, MOE_BLOCK), blk_exp))
    out = routed + _swiglu(t, sw_gate, sw_up, sw_down)
    return out.reshape(bsz, seq, d)


def setup_inputs(seed: int = 0) -> dict:
    key = jax.random.key(seed)
    ks = jax.random.split(key, 24)
    f32 = jnp.float32
    D, L, E = D_MODEL, DEPTH, N_EXPERTS

    def nrm(k, shape, scale):
        return jax.random.normal(k, shape, f32) * scale

    return {
        'x': nrm(ks[0], (BATCH, SEQ, D), 1.0),
        'c': nrm(ks[1], (BATCH, D), 1.0),
        'ada_w': nrm(ks[2], (L, D, N_MOD * D), 0.5 * D ** -0.5),
        'ada_b': nrm(ks[3], (L, N_MOD * D), 0.02),
        'norm1_g': 1.0 + nrm(ks[4], (L, D), 0.05),
        'w_in': nrm(ks[5], (L, D, IN_COLS), D ** -0.5),
        'hgrn_lb_logits': nrm(ks[6], (L, A_DK), 0.5),
        'hgrn_onorm_g': 1.0 + nrm(ks[7], (L, A_DV), 0.05),
        'conv_w': nrm(ks[8], (L, CONV_W, B_WIDTH), CONV_W ** -0.5),
        'rel_bias': nrm(ks[9], (L, C_HEADS, N_REL), 0.5),
        'w_branch_a': nrm(ks[10], (L, A_DV, D), A_DV ** -0.5),
        'w_branch_b': nrm(ks[11], (L, B_WIDTH, D), B_WIDTH ** -0.5),
        'w_branch_c': nrm(ks[12], (L, C_DIM, D), C_DIM ** -0.5),
        'w_out': nrm(ks[13], (L, D, D), D ** -0.5),
        'norm2_g': 1.0 + nrm(ks[14], (L, D), 0.05),
        'router_w': nrm(ks[15], (L, D, E), D ** -0.5),
        'router_bias': nrm(ks[16], (L, E), 0.01),
        'moe_w_gate': nrm(ks[17], (L, E, D, D_EXPERT), D ** -0.5),
        'moe_w_up': nrm(ks[18], (L, E, D, D_EXPERT), D ** -0.5),
        'moe_w_down': nrm(ks[19], (L, E, D_EXPERT, D), D_EXPERT ** -0.5),
        'shared_w_gate': nrm(ks[20], (L, D, D_SHARED), D ** -0.5),
        'shared_w_up': nrm(ks[21], (L, D, D_SHARED), D ** -0.5),
        'shared_w_down': nrm(ks[22], (L, D_SHARED, D), D_SHARED ** -0.5),
        'final_g': 1.0 + nrm(ks[23], (D,), 0.05),
    }


def reference(x, c, ada_w, ada_b, norm1_g, w_in, hgrn_lb_logits, hgrn_onorm_g, conv_w, rel_bias,
              w_branch_a, w_branch_b, w_branch_c, w_out, norm2_g, router_w, router_bias,
              moe_w_gate, moe_w_up, moe_w_down, shared_w_gate, shared_w_up, shared_w_down, final_g):
    c_act = jax.nn.silu(c)
    lbs = jnp.cumsum(jax.nn.softmax(hgrn_lb_logits.astype(jnp.float32), axis=0), axis=0)
    lbs = lbs - lbs[0]
    for l in range(DEPTH):
        mod = (c_act @ ada_w[l] + ada_b[l])[:, None, :]
        sh1, sc1, g1, sh2, sc2, g2 = jnp.split(mod, N_MOD, axis=-1)
        h = _rms(x, norm1_g[l]) * (1.0 + sc1) + sh1
        x = x + g1 * _mixer(h, lbs[l], w_in[l], hgrn_onorm_g[l], conv_w[l], rel_bias[l],
                            w_branch_a[l], w_branch_b[l], w_branch_c[l], w_out[l])
        h = _rms(x, norm2_g[l]) * (1.0 + sc2) + sh2
        x = x + g2 * _moe(h, router_w[l], router_bias[l], moe_w_gate[l], moe_w_up[l], moe_w_down[l],
                          shared_w_gate[l], shared_w_up[l], shared_w_down[l])
    return _rms(x, final_g)
```

```python
import functools

import numpy as np
import jax
import jax.numpy as jnp
from jax import lax
from jax.experimental import pallas as pl
from jax.experimental.pallas import tpu as pltpu

F32 = jnp.float32
BF16 = jnp.bfloat16
I32 = jnp.int32
U32 = jnp.uint32

CHUNK = 64
EPS = 1e-6
A_HEADS = 6
A_HEAD_DIM = 128
A_DIM = A_HEADS * A_HEAD_DIM
B_WIDTH = 512
CONV_W = 3
C_HEADS = 6
C_HEAD_DIM = 128
C_DIM = C_HEADS * C_HEAD_DIM
LEFT_CHUNKS = 8
REL_CLIP = 256
N_EXPERTS = 64
TOP_K = 8
N_GROUPS = 8
GROUP_SIZE = N_EXPERTS // N_GROUPS
TOPK_GROUPS = 4
ROUTED_SCALE = 2.5
N_MOD = 6

V7X_VMEM_BYTES = 64 * 1024 * 1024
LANES = 128
NEG_BIG = -1e30

SUB = 16
MOE_BLK = 512
ATT_TQ = 256


def _vmem_limit(nbytes):
    return int(min(max(nbytes * 3 // 2, 16 * 1024 * 1024), V7X_VMEM_BYTES - 8 * 1024 * 1024))


def _cparams(sem, nbytes):
    return pltpu.CompilerParams(dimension_semantics=sem, vmem_limit_bytes=_vmem_limit(nbytes))


def _sigmoid(x):
    return 1.0 / (1.0 + jnp.exp(-x))


def _silu(x):
    return x * _sigmoid(x)


def _split3(x):
    hi = x.astype(BF16)
    r1 = x - hi.astype(F32)
    mid = r1.astype(BF16)
    lo = (r1 - mid.astype(F32)).astype(BF16)
    return hi, mid, lo


def _pack_bf16_pairs(x):
    n = x.shape[1] // 2
    xb = x.astype(BF16).astype(F32)
    lo = lax.bitcast_convert_type(xb[:, :n], U32) >> 16
    hi = lax.bitcast_convert_type(xb[:, n:], U32) & jnp.uint32(0xFFFF0000)
    return lo | hi


def _unpack_bf16_pairs(p):
    lo = lax.bitcast_convert_type(p << 16, F32).astype(BF16)
    hi = lax.bitcast_convert_type(p & jnp.uint32(0xFFFF0000), F32).astype(BF16)
    return lo, hi


def _mod_kernel(c_ref, w_ref, b_ref, o_ref):
    c = c_ref[...]
    o_ref[0] = jnp.dot(_silu(c), w_ref[0], preferred_element_type=F32) + b_ref[0]


def _modulation(c, ada_w, ada_b):
    depth, d, n = ada_w.shape
    bsz = c.shape[0]
    rows = 8
    c_pad = jnp.zeros((rows, d), F32).at[:bsz].set(c)
    tn = 1536 if n % 1536 == 0 else n
    out = pl.pallas_call(
        _mod_kernel,
        grid=(depth, n // tn),
        in_specs=[
            pl.BlockSpec((rows, d), lambda l, j: (0, 0)),
            pl.BlockSpec((1, d, tn), lambda l, j: (l, 0, j)),
            pl.BlockSpec((1, 1, tn), lambda l, j: (l, 0, j)),
        ],
        out_specs=pl.BlockSpec((1, rows, tn), lambda l, j: (l, 0, j)),
        out_shape=jax.ShapeDtypeStruct((depth, rows, n), F32),
        compiler_params=_cparams(("arbitrary", "arbitrary"), 2 * d * tn * 4 + 4 * rows * tn * 4),
        name="adaln_mod",
    )(c_pad, ada_w, ada_b.reshape(depth, 1, n))
    return out[:, :bsz]


def _inproj_kernel(x_ref, g_ref, sc_ref, sh_ref, w_ref, p_ref, af_ref, h_scr, *, n_gate_tiles, af_tile):
    j = pl.program_id(1)

    @pl.when(j == 0)
    def _():
        x = x_ref[...]
        y = x * lax.rsqrt(jnp.mean(x * x, axis=-1, keepdims=True) + EPS)
        h = y * g_ref[...] * (1.0 + sc_ref[0]) + sh_ref[0]
        h_scr[...] = h.astype(BF16)

    acc = jnp.dot(h_scr[...], w_ref[...], preferred_element_type=F32)

    @pl.when(j < n_gate_tiles)
    def _():
        p_ref[...] = _sigmoid(acc).astype(BF16)

    @pl.when(j >= n_gate_tiles)
    def _():
        p_ref[...] = acc.astype(BF16)

    @pl.when(j == af_tile)
    def _():
        af_ref[...] = acc


def _in_projection(x2, seq, g, sc, sh, w_bf16, d_model):
    t, d = x2.shape
    n = w_bf16.shape[1]
    tn = A_DIM
    tm = min(1024, seq)
    tiles_per_batch = seq // tm
    n_gate_tiles = (3 * d_model) // tn
    af_tile = n_gate_tiles + 1
    kern = functools.partial(_inproj_kernel, n_gate_tiles=n_gate_tiles, af_tile=af_tile)
    est = 2 * tm * d * 4 + 2 * d * tn * 2 + 2 * tm * tn * 2 + 2 * tm * tn * 4 + tm * d * 2 + 2 * tm * tn * 4
    p, af = pl.pallas_call(
        kern,
        grid=(t // tm, n // tn),
        in_specs=[
            pl.BlockSpec((tm, d), lambda i, j: (i, 0)),
            pl.BlockSpec((1, d), lambda i, j: (0, 0)),
            pl.BlockSpec((1, 1, d), lambda i, j: (i // tiles_per_batch, 0, 0)),
            pl.BlockSpec((1, 1, d), lambda i, j: (i // tiles_per_batch, 0, 0)),
            pl.BlockSpec((d, tn), lambda i, j: (0, j)),
        ],
        out_specs=[
            pl.BlockSpec((tm, tn), lambda i, j: (i, j)),
            pl.BlockSpec((tm, tn), lambda i, j: (i, 0)),
        ],
        out_shape=[jax.ShapeDtypeStruct((t, n), BF16), jax.ShapeDtypeStruct((t, tn), F32)],
        scratch_shapes=[pltpu.VMEM((tm, d), BF16)],
        compiler_params=_cparams(("arbitrary", "arbitrary"), est),
        name="norm_inproj",
    )(x2, g.reshape(1, d), sc, sh, w_bf16)
    return p, af


def _hgrn_kernel(aq_ref, af_ref, ai_ref, ag_ref, loglb_ref, log1mlb_ref, og_ref, o_ref,
                 st_scr, q_scr, k_scr, v_scr, b_scr, p_scr, r_scr, *, n_chunks):
    @pl.when(pl.program_id(1) == 0)
    def _():
        st_scr[...] = jnp.zeros_like(st_scr)

    hd = A_HEAD_DIM
    n_sub = CHUNK // SUB
    row_c = lax.broadcasted_iota(I32, (CHUNK, CHUNK), 0)
    col_c = lax.broadcasted_iota(I32, (CHUNK, CHUNK), 1)
    tri = (col_c <= row_c).astype(BF16)
    ones_blk = jnp.ones((hd, hd), BF16)
    sub_row = lax.broadcasted_iota(I32, (SUB, 1), 0)
    log_lb = loglb_ref[...]
    log1m_lb = log1mlb_ref[...]
    og = og_ref[...]

    def chunk_body(ci, carry):
        r0 = pl.multiple_of(ci * CHUNK, CHUNK)
        rows = pl.ds(r0, CHUNK)
        af = af_ref[rows, :]
        ls = jnp.minimum(af, 0.0) - jnp.log(1.0 + jnp.exp(-jnp.abs(af)))
        t2 = log1m_lb + ls
        mx = jnp.maximum(log_lb, t2)
        lf = mx + jnp.log(1.0 + jnp.exp(-jnp.abs(log_lb - t2)))
        k_scr[...] = 1.0 - jnp.exp(lf)
        hi, mid, lo = _split3(lf)
        b = (jnp.dot(tri, hi, preferred_element_type=F32) + jnp.dot(tri, mid, preferred_element_type=F32)
             + jnp.dot(tri, lo, preferred_element_type=F32))
        b_scr[...] = b
        q_scr[...] = aq_ref[rows, :].astype(F32) * (A_HEAD_DIM ** -0.5)
        v_scr[...] = ai_ref[rows, :].astype(F32)

        for i in range(n_sub):
            s0 = i * SUB
            b_blk = b_scr[pl.ds(s0, SUB), :]
            q_blk = q_scr[pl.ds(s0, SUB), :]
            for s in range(SUB):
                b_s = b_scr[pl.ds(s0 + s, 1), :]
                k_s = k_scr[pl.ds(s0 + s, 1), :]
                e = jnp.exp(jnp.where(sub_row >= s, b_blk - b_s, NEG_BIG))
                p_scr[pl.ds((s0 + s) * SUB, SUB), :] = (q_blk * k_s * e).astype(BF16)
        for h in range(A_HEADS):
            cols = slice(h * hd, (h + 1) * hd)
            r_scr[:, cols] = jnp.dot(p_scr[:, cols], ones_blk, preferred_element_type=F32)

        b_last = b_scr[pl.ds(CHUNK - 1, 1), :]
        outs = []
        for h in range(A_HEADS):
            cols = slice(h * hd, (h + 1) * hd)
            qh = q_scr[:, cols]
            kh = k_scr[:, cols]
            vh = v_scr[:, cols]
            bh = b_scr[:, cols]
            st = st_scr[h]
            qt = (qh * jnp.exp(bh)).astype(BF16)
            inter = lax.dot_general(qt, st.astype(BF16), (((1,), (1,)), ((), ())), preferred_element_type=F32)
            blocks = []
            for i in range(n_sub):
                s0 = i * SUB
                acc = inter[s0:s0 + SUB]
                for s in range(SUB):
                    acc = acc + r_scr[pl.ds((s0 + s) * SUB, SUB), cols] * v_scr[pl.ds(s0 + s, 1), cols]
                if i > 0:
                    beta = b_scr[pl.ds(s0 - 1, 1), cols]
                    qhat = (qh[s0:s0 + SUB] * jnp.exp(bh[s0:s0 + SUB] - beta)).astype(BF16)
                    khat = (kh[:s0] * jnp.exp(beta - bh[:s0])).astype(BF16)
                    a = lax.dot_general(qhat, khat, (((1,), (1,)), ((), ())), preferred_element_type=F32)
                    acc = acc + jnp.dot(a.astype(BF16), vh[:s0].astype(BF16), preferred_element_type=F32)
                blocks.append(acc)
            o = jnp.concatenate(blocks, axis=0)
            bl = b_last[:, cols]
            ktil = (kh * jnp.exp(bl - bh)).astype(BF16)
            upd = lax.dot_general(vh.astype(BF16), ktil, (((0,), (0,)), ((), ())), preferred_element_type=F32)
            st_scr[h] = st * jnp.exp(bl) + upd
            o = o * lax.rsqrt(jnp.mean(o * o, axis=-1, keepdims=True) + EPS)
            outs.append(o)
        o_all = jnp.concatenate(outs, axis=1)
        gate = ag_ref[rows, :].astype(F32)
        o_ref[rows, :] = (o_all * og * _silu(gate)).astype(BF16)
        return carry

    lax.fori_loop(0, n_chunks, chunk_body, 0)


def _hgrn(p, af, log_lb, log1m_lb, onorm_g, bsz, seq, aq_blk):
    t = p.shape[0]
    ts = min(512, seq)
    n_seq_tiles = seq // ts
    n_chunks = ts // CHUNK
    kern = functools.partial(_hgrn_kernel, n_chunks=n_chunks)
    row = lambda b, i: b * n_seq_tiles + i
    est = (2 * 3 * ts * A_DIM * 2 + 2 * ts * A_DIM * 4 + 2 * ts * A_DIM * 2 + A_HEADS * 128 * 128 * 4
           + 4 * CHUNK * A_DIM * 4 + CHUNK * SUB * A_DIM * 6)
    return pl.pallas_call(
        kern,
        grid=(bsz, n_seq_tiles),
        in_specs=[
            pl.BlockSpec((ts, A_DIM), lambda b, i: (row(b, i), aq_blk)),
            pl.BlockSpec((ts, A_DIM), lambda b, i: (row(b, i), 0)),
            pl.BlockSpec((ts, A_DIM), lambda b, i: (row(b, i), aq_blk + 2)),
            pl.BlockSpec((ts, A_DIM), lambda b, i: (row(b, i), aq_blk + 3)),
            pl.BlockSpec((1, A_DIM), lambda b, i: (0, 0)),
            pl.BlockSpec((1, A_DIM), lambda b, i: (0, 0)),
            pl.BlockSpec((1, A_DIM), lambda b, i: (0, 0)),
        ],
        out_specs=pl.BlockSpec((ts, A_DIM), lambda b, i: (row(b, i), 0)),
        out_shape=jax.ShapeDtypeStruct((t, A_DIM), BF16),
        scratch_shapes=[
            pltpu.VMEM((A_HEADS, A_HEAD_DIM, A_HEAD_DIM), F32),
            pltpu.VMEM((CHUNK, A_DIM), F32),
            pltpu.VMEM((CHUNK, A_DIM), F32),
            pltpu.VMEM((CHUNK, A_DIM), F32),
            pltpu.VMEM((CHUNK, A_DIM), F32),
            pltpu.VMEM((CHUNK * SUB, A_DIM), BF16),
            pltpu.VMEM((CHUNK * SUB, A_DIM), F32),
        ],
        compiler_params=_cparams(("arbitrary", "arbitrary"), est),
        name="hgrn2_scan",
    )(p, af, p, p, log_lb.reshape(1, A_DIM), log1m_lb.reshape(1, A_DIM), onorm_g.reshape(1, A_DIM))


def _attn_kernel(q_ref, k0_ref, k1_ref, k2_ref, v0_ref, v1_ref, v2_ref, bias_ref, o_ref):
    i = pl.program_id(1)
    tq = q_ref.shape[0]
    hd = C_HEAD_DIM
    scale = C_HEAD_DIM ** -0.5
    col = lax.broadcasted_iota(I32, (1, 3 * tq), 1)
    valid = (col >= 2 * tq) | ((col >= tq) & (i >= 1)) | (i >= 2)
    for h in range(C_HEADS):
        cols = slice(h * hd, (h + 1) * hd)
        qh = q_ref[:, cols]
        kh = jnp.concatenate([k0_ref[:, cols], k1_ref[:, cols], k2_ref[:, cols]], axis=0)
        vh = jnp.concatenate([v0_ref[:, cols], v1_ref[:, cols], v2_ref[:, cols]], axis=0)
        s = lax.dot_general(qh, kh, (((1,), (1,)), ((), ())), preferred_element_type=F32) * scale + bias_ref[h]
        s = jnp.where(valid, s, NEG_BIG)
        m = jnp.max(s, axis=-1, keepdims=True)
        e = jnp.exp(s - m)
        pr = e / jnp.sum(e, axis=-1, keepdims=True)
        o_ref[:, cols] = jnp.dot(pr.astype(BF16), vh, preferred_element_type=F32).astype(BF16)


def _attention_bias_table(rel_bias, tq):
    band = (LEFT_CHUNKS + 1) * CHUNK
    assert tq * 2 == LEFT_CHUNKS * CHUNK, "three key slots of tq rows must cover the band"
    tt = np.arange(tq)[:, None]
    ss = np.arange(3 * tq)[None, :]
    sp = ss - (tt // CHUNK) * CHUNK
    inside = (sp >= 0) & (sp < band)
    rel = np.clip((tt % CHUNK) + LEFT_CHUNKS * CHUNK - sp, -REL_CLIP, REL_CLIP) + REL_CLIP
    tab = rel_bias.astype(F32)[:, jnp.asarray(np.where(inside, rel, 0))]
    return jnp.where(jnp.asarray(inside)[None], tab, NEG_BIG)


def _attention(p, rel_bias, bsz, seq, q_blk):
    t = p.shape[0]
    tq = ATT_TQ
    nq = seq // tq
    bias = _attention_bias_table(rel_bias, tq)
    row = lambda b, i, back: b * nq + jnp.maximum(i - back, 0)
    kv_spec = lambda blk, back: pl.BlockSpec((tq, C_DIM), lambda b, i: (row(b, i, back), blk))
    est = 2 * 7 * tq * C_DIM * 2 + 2 * C_HEADS * tq * 3 * tq * 4 + 2 * tq * C_DIM * 2 + 6 * tq * 3 * tq * 4
    return pl.pallas_call(
        _attn_kernel,
        grid=(bsz, nq),
        in_specs=[
            pl.BlockSpec((tq, C_DIM), lambda b, i: (b * nq + i, q_blk)),
            kv_spec(q_blk + 1, 2), kv_spec(q_blk + 1, 1), kv_spec(q_blk + 1, 0),
            kv_spec(q_blk + 2, 2), kv_spec(q_blk + 2, 1), kv_spec(q_blk + 2, 0),
            pl.BlockSpec((C_HEADS, tq, 3 * tq), lambda b, i: (0, 0, 0)),
        ],
        out_specs=pl.BlockSpec((tq, C_DIM), lambda b, i: (b * nq + i, 0)),
        out_shape=jax.ShapeDtypeStruct((t, C_DIM), BF16),
        compiler_params=_cparams(("arbitrary", "arbitrary"), est),
        name="chunk_attention",
    )(p, p, p, p, p, p, p, bias)


def _merge_kernel(x_ref, gmod_ref, a_ref, c_ref, bb_ref, bc_ref, bu_ref, hbc_ref, hbu_ref, cw_ref,
                  ga_ref, gb_ref, gc_ref, wa_ref, wb_ref, wc_ref, wo_ref, o_ref, *, tiles_per_batch):
    i = pl.program_id(0)
    tm = x_ref.shape[0]
    u = bc_ref[...].astype(F32) * bu_ref[...].astype(F32)
    halo = hbc_ref[...].astype(F32) * hbu_ref[...].astype(F32)
    halo = jnp.where(i % tiles_per_batch == 0, 0.0, halo)
    hrows = halo.shape[0]
    hm1 = halo[hrows - 1:hrows]
    hm2 = halo[hrows - 2:hrows - 1]
    r = lax.broadcasted_iota(I32, (tm, 1), 0)
    u1 = jnp.where(r == 0, hm1, pltpu.roll(u, 1, 0))
    u2 = jnp.where(r == 0, hm2, jnp.where(r == 1, hm1, pltpu.roll(u, 2, 0)))
    cw = cw_ref[...]
    conv = u2 * cw[0:1] + u1 * cw[1:2] + u * cw[2:3]
    bfeat = (bb_ref[...].astype(F32) * conv).astype(BF16)
    ya = jnp.dot(a_ref[...], wa_ref[...], preferred_element_type=F32)
    yb = jnp.dot(bfeat, wb_ref[...], preferred_element_type=F32)
    yc = jnp.dot(c_ref[...], wc_ref[...], preferred_element_type=F32)
    merged = (ga_ref[...].astype(F32) * ya + gb_ref[...].astype(F32) * yb + gc_ref[...].astype(F32) * yc)
    y = jnp.dot(merged.astype(BF16), wo_ref[...], preferred_element_type=F32)
    o_ref[...] = x_ref[...] + gmod_ref[0] * y


def _merge(x2, seq, gmod, a_feat, c_feat, p, conv_w, wa, wb, wc, wo, b_blk):
    t, d = x2.shape
    tm = min(256, seq)
    tiles_per_batch = seq // tm
    halo = 16
    hb = tm // halo
    kern = functools.partial(_merge_kernel, tiles_per_batch=tiles_per_batch)
    const = lambda shape: pl.BlockSpec(shape, lambda i: (0,) * len(shape))
    halo_spec = lambda blk: pl.BlockSpec((halo, B_WIDTH), lambda i: (jnp.maximum(i * hb - 1, 0), blk))
    est = (4 * tm * d * 4 + 4 * tm * A_DIM * 2 + 6 * tm * B_WIDTH * 2 + 6 * tm * d * 2
           + 2 * (A_DIM + B_WIDTH + C_DIM + d) * d * 2 + 6 * tm * d * 4)
    return pl.pallas_call(
        kern,
        grid=(t // tm,),
        in_specs=[
            pl.BlockSpec((tm, d), lambda i: (i, 0)),
            pl.BlockSpec((1, 1, d), lambda i: (i // tiles_per_batch, 0, 0)),
            pl.BlockSpec((tm, A_DIM), lambda i: (i, 0)),
            pl.BlockSpec((tm, C_DIM), lambda i: (i, 0)),
            pl.BlockSpec((tm, B_WIDTH), lambda i: (i, b_blk)),
            pl.BlockSpec((tm, B_WIDTH), lambda i: (i, b_blk + 1)),
            pl.BlockSpec((tm, B_WIDTH), lambda i: (i, b_blk + 2)),
            halo_spec(b_blk + 1),
            halo_spec(b_blk + 2),
            const((CONV_W, B_WIDTH)),
            pl.BlockSpec((tm, d), lambda i: (i, 0)),
            pl.BlockSpec((tm, d), lambda i: (i, 1)),
            pl.BlockSpec((tm, d), lambda i: (i, 2)),
            const((A_DIM, d)), const((B_WIDTH, d)), const((C_DIM, d)), const((d, d)),
        ],
        out_specs=pl.BlockSpec((tm, d), lambda i: (i, 0)),
        out_shape=jax.ShapeDtypeStruct((t, d), F32),
        compiler_params=_cparams(("arbitrary",), est),
        name="merge_outproj",
    )(x2, gmod, a_feat, c_feat, p, p, p, p, p, conv_w, p, p, p, wa, wb, wc, wo)


def _first_max(vals, iota, n, axis):
    m = jnp.max(vals, axis=axis, keepdims=True)
    idx = jnp.min(jnp.where(vals == m, iota, n), axis=axis, keepdims=True)
    return m, idx


def _router_kernel(x_ref, g_ref, sc_ref, sh_ref, rwt_ref, rb_ref, h_ref, hp_ref, e_ref, w_ref, rank_ref, cnt_ref,
                   cnt_scr):
    step = pl.program_id(0)

    @pl.when(step == 0)
    def _():
        cnt_scr[...] = jnp.zeros_like(cnt_scr)

    tm = x_ref.shape[0]
    x = x_ref[...]
    y = x * lax.rsqrt(jnp.mean(x * x, axis=-1, keepdims=True) + EPS)
    h = y * g_ref[...] * (1.0 + sc_ref[0]) + sh_ref[0]
    h_ref[...] = h.astype(BF16)
    hp_ref[...] = _pack_bf16_pairs(h)

    nt = (((1,), (1,)), ((), ()))
    h_hi, h_mid, _ = _split3(h)
    w_hi, w_mid, _ = _split3(rwt_ref[...])
    logits = (lax.dot_general(w_hi, h_hi, nt, preferred_element_type=F32)
              + lax.dot_general(w_hi, h_mid, nt, preferred_element_type=F32)
              + lax.dot_general(w_mid, h_hi, nt, preferred_element_type=F32))
    scores = _sigmoid(logits)
    sel = scores + rb_ref[:, 0:1]

    g3 = sel.reshape(N_GROUPS, GROUP_SIZE, tm)
    j3 = lax.broadcasted_iota(I32, g3.shape, 1)
    m1, i1 = _first_max(g3, j3, GROUP_SIZE, 1)
    m2 = jnp.max(jnp.where(j3 == i1, -jnp.inf, g3), axis=1, keepdims=True)
    gs = (m1 + m2).reshape(N_GROUPS, tm)
    gi = lax.broadcasted_iota(I32, gs.shape, 0)
    gmask = jnp.zeros(gs.shape, jnp.bool_)
    for _ in range(TOPK_GROUPS):
        _, idx = _first_max(gs, gi, N_GROUPS, 0)
        hit = gi == idx
        gmask = gmask | hit
        gs = jnp.where(hit, -jnp.inf, gs)
    emask = jnp.broadcast_to(gmask.reshape(N_GROUPS, 1, tm), (N_GROUPS, GROUP_SIZE, tm)).reshape(N_EXPERTS, tm)
    masked = jnp.where(emask, sel, -jnp.inf)

    ei = lax.broadcasted_iota(I32, (N_EXPERTS, tm), 0)
    chosen = jnp.zeros((N_EXPERTS, tm), F32)
    e_rows, w_rows = [], []
    for _ in range(TOP_K):
        _, idx = _first_max(masked, ei, N_EXPERTS, 0)
        hit = ei == idx
        e_rows.append(idx)
        w_rows.append(jnp.sum(jnp.where(hit, scores, 0.0), axis=0, keepdims=True))
        chosen = jnp.where(hit, 1.0, chosen)
        masked = jnp.where(hit, -jnp.inf, masked)
    eidx = jnp.concatenate(e_rows, axis=0)
    wts = jnp.concatenate(w_rows, axis=0)
    wts = wts / jnp.sum(wts, axis=0, keepdims=True) * ROUTED_SCALE
    e_ref[...] = eidx
    w_ref[...] = wts

    tr = lax.broadcasted_iota(I32, (tm, tm), 0)
    tc = lax.broadcasted_iota(I32, (tm, tm), 1)
    before = (tr < tc).astype(BF16)
    base = cnt_scr[:, 0:1]
    rank_all = jnp.dot(chosen.astype(BF16), before, preferred_element_type=F32) + base
    ranks = [jnp.sum(jnp.where(ei == e_rows[k], rank_all, 0.0), axis=0, keepdims=True) for k in range(TOP_K)]
    rank_ref[...] = jnp.concatenate(ranks, axis=0).astype(I32)
    total = base + jnp.sum(chosen, axis=1, keepdims=True)
    cnt_scr[...] = jnp.broadcast_to(total, cnt_scr.shape)
    cnt_ref[...] = jnp.broadcast_to(total, cnt_ref.shape).astype(I32)


def _router(x2, seq, g, sc, sh, router_wt, router_bias):
    t, d = x2.shape
    tm = min(512, seq)
    tiles_per_batch = seq // tm
    est = (2 * tm * d * 4 + 2 * tm * d * 2 + 2 * tm * d * 2 + 2 * N_EXPERTS * d * 4 + tm * tm * 2
           + 8 * tm * d * 4 + 16 * N_EXPERTS * tm * 4)
    return pl.pallas_call(
        _router_kernel,
        grid=(t // tm,),
        in_specs=[
            pl.BlockSpec((tm, d), lambda i: (i, 0)),
            pl.BlockSpec((1, d), lambda i: (0, 0)),
            pl.BlockSpec((1, 1, d), lambda i: (i // tiles_per_batch, 0, 0)),
            pl.BlockSpec((1, 1, d), lambda i: (i // tiles_per_batch, 0, 0)),
            pl.BlockSpec((N_EXPERTS, d), lambda i: (0, 0)),
            pl.BlockSpec((N_EXPERTS, LANES), lambda i: (0, 0)),
        ],
        out_specs=[
            pl.BlockSpec((tm, d), lambda i: (i, 0)),
            pl.BlockSpec((tm, d // 2), lambda i: (i, 0)),
            pl.BlockSpec((TOP_K, tm), lambda i: (0, i)),
            pl.BlockSpec((TOP_K, tm), lambda i: (0, i)),
            pl.BlockSpec((TOP_K, tm), lambda i: (0, i)),
            pl.BlockSpec((N_EXPERTS, LANES), lambda i: (0, 0)),
        ],
        out_shape=[
            jax.ShapeDtypeStruct((t, d), BF16),
            jax.ShapeDtypeStruct((t, d // 2), U32),
            jax.ShapeDtypeStruct((TOP_K, t), I32),
            jax.ShapeDtypeStruct((TOP_K, t), F32),
            jax.ShapeDtypeStruct((TOP_K, t), I32),
            jax.ShapeDtypeStruct((N_EXPERTS, LANES), I32),
        ],
        scratch_shapes=[pltpu.VMEM((N_EXPERTS, LANES), F32)],
        compiler_params=_cparams(("arbitrary",), est),
        name="norm_router",
    )(x2, g.reshape(1, d), sc, sh, router_wt, jnp.broadcast_to(router_bias.reshape(N_EXPERTS, 1), (N_EXPERTS, LANES)))


def _pos_kernel(start_ref, e_ref, rank_ref, pos_ref):
    e = e_ref[...]
    acc = rank_ref[...]
    for ex in range(N_EXPERTS):
        acc = acc + jnp.where(e == ex, start_ref[ex], 0)
    pos_ref[...] = acc


def _positions(pstart, eidx, rank):
    k, t = eidx.shape
    tl = min(4096, t)
    return pl.pallas_call(
        _pos_kernel,
        grid_spec=pltpu.PrefetchScalarGridSpec(
            num_scalar_prefetch=1,
            grid=(t // tl,),
            in_specs=[pl.BlockSpec((k, tl), lambda i, s: (0, i)), pl.BlockSpec((k, tl), lambda i, s: (0, i))],
            out_specs=pl.BlockSpec((k, tl), lambda i, s: (0, i)),
        ),
        out_shape=jax.ShapeDtypeStruct((k, t), I32),
        compiler_params=_cparams(("arbitrary",), 8 * k * tl * 4),
        name="moe_positions",
    )(pstart, eidx, rank)


def _dispatch_kernel(pos_ref, hp_ref, xs_ref, sem, *, tm):
    base = pl.program_id(0) * tm

    def copy(t, k):
        return pltpu.make_async_copy(hp_ref.at[pl.ds(base + t, 1)], xs_ref.at[pl.ds(pos_ref[k, t], 1)], sem)

    def start(t, c):
        for k in range(TOP_K):
            copy(t, k).start()
        return c

    def wait(t, c):
        for k in range(TOP_K):
            copy(t, k).wait()
        return c

    lax.fori_loop(0, tm, start, 0)
    lax.fori_loop(0, tm, wait, 0)


def _dispatch(pos, hp, n_rows):
    t, w = hp.shape
    tm = min(512, t)
    kern = functools.partial(_dispatch_kernel, tm=tm)
    return pl.pallas_call(
        kern,
        grid=(t // tm,),
        in_specs=[
            pl.BlockSpec((TOP_K, tm), lambda i: (0, i), memory_space=pltpu.SMEM),
            pl.BlockSpec(memory_space=pl.ANY),
        ],
        out_specs=pl.BlockSpec(memory_space=pl.ANY),
        out_shape=jax.ShapeDtypeStruct((n_rows, w), U32),
        scratch_shapes=[pltpu.SemaphoreType.DMA(())],
        compiler_params=pltpu.CompilerParams(dimension_semantics=("arbitrary",), has_side_effects=True),
        name="moe_dispatch",
    )(pos, hp)


def _expert_kernel(blk_exp_ref, n_used_ref, xs_ref, wg_ref, wu_ref, wd_ref, ys_ref):
    b = pl.program_id(0)

    @pl.when(b < n_used_ref[0])
    def _():
        lo, hi = _unpack_bf16_pairs(xs_ref[...])
        half = lo.shape[1]
        gate = (jnp.dot(lo, wg_ref[0, :half], preferred_element_type=F32)
                + jnp.dot(hi, wg_ref[0, half:], preferred_element_type=F32))
        up = (jnp.dot(lo, wu_ref[0, :half], preferred_element_type=F32)
              + jnp.dot(hi, wu_ref[0, half:], preferred_element_type=F32))
        hidden = (_silu(gate) * up).astype(BF16)
        ys_ref[...] = _pack_bf16_pairs(jnp.dot(hidden, wd_ref[0], preferred_element_type=F32))

    @pl.when(b >= n_used_ref[0])
    def _():
        ys_ref[...] = jnp.zeros_like(ys_ref)


def _experts(blk_exp, n_used, xs, wg, wu, wd):
    n_rows, half = xs.shape
    d = 2 * half
    de = wg.shape[2]
    nb = n_rows // MOE_BLK
    est = 4 * MOE_BLK * half * 4 + 2 * 3 * d * de * 2 + MOE_BLK * (2 * de + 2 * d) * 4
    return pl.pallas_call(
        _expert_kernel,
        grid_spec=pltpu.PrefetchScalarGridSpec(
            num_scalar_prefetch=2,
            grid=(nb,),
            in_specs=[
                pl.BlockSpec((MOE_BLK, half), lambda b, be, nu: (b, 0)),
                pl.BlockSpec((1, d, de), lambda b, be, nu: (be[b], 0, 0)),
                pl.BlockSpec((1, d, de), lambda b, be, nu: (be[b], 0, 0)),
                pl.BlockSpec((1, de, d), lambda b, be, nu: (be[b], 0, 0)),
            ],
            out_specs=pl.BlockSpec((MOE_BLK, half), lambda b, be, nu: (b, 0)),
        ),
        out_shape=jax.ShapeDtypeStruct((n_rows, half), U32),
        compiler_params=_cparams(("arbitrary",), est),
        name="moe_experts",
    )(blk_exp, n_used, xs, wg, wu, wd)


def _combine_kernel(pos_ref, x_ref, gmod_ref, h_ref, wt_ref, sg_ref, su_ref, sd_ref, ys_ref, o_ref, yg_scr, sem,
                    *, tm):
    def copy(t, k):
        return pltpu.make_async_copy(ys_ref.at[pl.ds(pos_ref[k, t], 1)], yg_scr.at[k, pl.ds(t, 1)], sem)

    def start(t, c):
        for k in range(TOP_K):
            copy(t, k).start()
        return c

    def wait(t, c):
        for k in range(TOP_K):
            copy(t, k).wait()
        return c

    lax.fori_loop(0, tm, start, 0)
    h = h_ref[...]
    hidden = (_silu(jnp.dot(h, sg_ref[...], preferred_element_type=F32))
              * jnp.dot(h, su_ref[...], preferred_element_type=F32)).astype(BF16)
    shared = jnp.dot(hidden, sd_ref[...], preferred_element_type=F32)
    lax.fori_loop(0, tm, wait, 0)
    half = yg_scr.shape[2]
    acc_lo = shared[:, :half]
    acc_hi = shared[:, half:]
    wt = wt_ref[...]
    for k in range(TOP_K):
        lo, hi = _unpack_bf16_pairs(yg_scr[k])
        wk = wt[:, k:k + 1]
        acc_lo = acc_lo + wk * lo.astype(F32)
        acc_hi = acc_hi + wk * hi.astype(F32)
    g = gmod_ref[0]
    o_ref[:, :half] = x_ref[:, :half] + g[:, :half] * acc_lo
    o_ref[:, half:] = x_ref[:, half:] + g[:, half:] * acc_hi


def _combine(pos, x2, seq, gmod, h, wts_t, sg, su, sd, ys):
    t, d = x2.shape
    half = d // 2
    ds_ = sg.shape[1]
    tm = min(256, seq)
    tiles_per_batch = seq // tm
    kern = functools.partial(_combine_kernel, tm=tm)
    const = lambda shape: pl.BlockSpec(shape, lambda i: (0,) * len(shape))
    est = 4 * tm * d * 4 + 2 * tm * d * 2 + 2 * 3 * d * ds_ * 2 + TOP_K * tm * half * 4 + 8 * tm * d * 4
    return pl.pallas_call(
        kern,
        grid=(t // tm,),
        in_specs=[
            pl.BlockSpec((TOP_K, tm), lambda i: (0, i), memory_space=pltpu.SMEM),
            pl.BlockSpec((tm, d), lambda i: (i, 0)),
            pl.BlockSpec((1, 1, d), lambda i: (i // tiles_per_batch, 0, 0)),
            pl.BlockSpec((tm, d), lambda i: (i, 0)),
            pl.BlockSpec((tm, TOP_K), lambda i: (i, 0)),
            const((d, ds_)), const((d, ds_)), const((ds_, d)),
            pl.BlockSpec(memory_space=pl.ANY),
        ],
        out_specs=pl.BlockSpec((tm, d), lambda i: (i, 0)),
        out_shape=jax.ShapeDtypeStruct((t, d), F32),
        scratch_shapes=[pltpu.VMEM((TOP_K, tm, half), U32), pltpu.SemaphoreType.DMA(())],
        compiler_params=_cparams(("arbitrary",), est),
        name="moe_combine",
    )(pos, x2, gmod, h, wts_t, sg, su, sd, ys)


def _final_kernel(x_ref, g_ref, o_ref):
    x = x_ref[...]
    o_ref[...] = x * lax.rsqrt(jnp.mean(x * x, axis=-1, keepdims=True) + EPS) * g_ref[...]


def _final_norm(x2, g):
    t, d = x2.shape
    tm = min(1024, t)
    return pl.pallas_call(
        _final_kernel,
        grid=(t // tm,),
        in_specs=[pl.BlockSpec((tm, d), lambda i: (i, 0)), pl.BlockSpec((1, d), lambda i: (0, 0))],
        out_specs=pl.BlockSpec((tm, d), lambda i: (i, 0)),
        out_shape=jax.ShapeDtypeStruct((t, d), F32),
        compiler_params=_cparams(("arbitrary",), 4 * tm * d * 4),
        name="final_norm",
    )(x2, g.reshape(1, d))


def _moe_layout(counts, n_blocks):
    padded = (counts + MOE_BLK - 1) // MOE_BLK * MOE_BLK
    pends = jnp.cumsum(padded)
    pstart = (pends - padded).astype(I32)
    blk_exp = jnp.minimum(jnp.searchsorted(pends, jnp.arange(n_blocks, dtype=I32) * MOE_BLK, side='right'),
                          N_EXPERTS - 1).astype(I32)
    n_used = (pends[-1:] // MOE_BLK).astype(I32)
    return pstart, blk_exp, n_used


def _layer(x2, bsz, seq, mod, norm1_g, w_in, lb, onorm_g, conv_w, rel_bias, w_a, w_b, w_c, w_out, norm2_g,
           router_w, router_bias, wg, wu, wd, sg, su, sd):
    t, d = x2.shape
    sh1, sc1, g1, sh2, sc2, g2 = [m.reshape(bsz, 1, d) for m in jnp.split(mod, N_MOD, axis=-1)]
    n_mix = w_in.shape[1] - 3 * d
    w_re = jnp.concatenate([w_in[:, n_mix:], w_in[:, :n_mix]], axis=1).astype(BF16)
    p, af = _in_projection(x2, seq, norm1_g, sc1, sh1, w_re, d)
    aq_blk = (3 * d) // A_DIM
    b_blk = (3 * d + 4 * A_DIM) // B_WIDTH
    q_blk = (3 * d + 4 * A_DIM + 3 * B_WIDTH) // C_DIM
    a_feat = _hgrn(p, af, jnp.log(lb), jnp.log1p(-lb), onorm_g, bsz, seq, aq_blk)
    c_feat = _attention(p, rel_bias, bsz, seq, q_blk)
    x2 = _merge(x2, seq, g1, a_feat, c_feat, p, conv_w, w_a.astype(BF16), w_b.astype(BF16), w_c.astype(BF16),
                w_out.astype(BF16), b_blk)

    h, hp, eidx, wts, rank, cnt = _router(x2, seq, norm2_g, sc2, sh2, router_w.T, router_bias)
    n_blocks = (t * TOP_K) // MOE_BLK + N_EXPERTS
    pstart, blk_exp, n_used = _moe_layout(cnt[:, 0], n_blocks)
    pos = _positions(pstart, eidx, rank)
    xs = _dispatch(pos, hp, n_blocks * MOE_BLK)
    ys = _experts(blk_exp, n_used, xs, wg.astype(BF16), wu.astype(BF16), wd.astype(BF16))
    return _combine(pos, x2, seq, g2, h, wts.T, sg.astype(BF16), su.astype(BF16), sd.astype(BF16), ys)


def kernel(x, c, ada_w, ada_b, norm1_g, w_in, hgrn_lb_logits, hgrn_onorm_g, conv_w, rel_bias, w_branch_a, w_branch_b, w_branch_c, w_out, norm2_g, router_w, router_bias, moe_w_gate, moe_w_up, moe_w_down, shared_w_gate, shared_w_up, shared_w_down, final_g):
    bsz, seq, d = x.shape
    depth = ada_w.shape[0]
    mod = _modulation(c, ada_w, ada_b)
    lbs = jnp.cumsum(jax.nn.softmax(hgrn_lb_logits.astype(F32), axis=0), axis=0)
    lbs = lbs - lbs[0]
    x2 = x.reshape(bsz * seq, d)
    for l in range(depth):
        x2 = _layer(x2, bsz, seq, mod[l], norm1_g[l], w_in[l], lbs[l], hgrn_onorm_g[l], conv_w[l], rel_bias[l],
                    w_branch_a[l], w_branch_b[l], w_branch_c[l], w_out[l], norm2_g[l], router_w[l], router_bias[l],
                    moe_w_gate[l], moe_w_up[l], moe_w_down[l], shared_w_gate[l], shared_w_up[l], shared_w_down[l])
    return _final_norm(x2, final_g).reshape(bsz, seq, d)
```

```python
import functools

import numpy as np
import jax
import jax.numpy as jnp
from jax import lax
from jax.experimental import pallas as pl
from jax.experimental.pallas import tpu as pltpu

F32 = jnp.float32
BF16 = jnp.bfloat16
I32 = jnp.int32
U32 = jnp.uint32

CHUNK = 64
EPS = 1e-6
A_HEADS = 6
A_HEAD_DIM = 128
A_DIM = A_HEADS * A_HEAD_DIM
B_WIDTH = 512
CONV_W = 3
C_HEADS = 6
C_HEAD_DIM = 128
C_DIM = C_HEADS * C_HEAD_DIM
LEFT_CHUNKS = 8
REL_CLIP = 256
N_EXPERTS = 64
TOP_K = 8
N_GROUPS = 8
GROUP_SIZE = N_EXPERTS // N_GROUPS
TOPK_GROUPS = 4
ROUTED_SCALE = 2.5
N_MOD = 6

V7X_VMEM_BYTES = 64 * 1024 * 1024
LANES = 128
NEG_BIG = -1e30

SUB = 16
MOE_BLK = 512
ATT_TQ = 256


def _vmem_limit(nbytes):
    return int(min(max(nbytes * 3 // 2, 16 * 1024 * 1024), V7X_VMEM_BYTES - 8 * 1024 * 1024))


def _cparams(sem, nbytes):
    return pltpu.CompilerParams(dimension_semantics=sem, vmem_limit_bytes=_vmem_limit(nbytes))


def _sigmoid(x):
    return 1.0 / (1.0 + jnp.exp(-x))


def _silu(x):
    return x * _sigmoid(x)


def _split3(x):
    hi = x.astype(BF16)
    r1 = x - hi.astype(F32)
    mid = r1.astype(BF16)
    lo = (r1 - mid.astype(F32)).astype(BF16)
    return hi, mid, lo


def _pack_bf16_pairs(x):
    n = x.shape[1] // 2
    xb = x.astype(BF16).astype(F32)
    lo = lax.bitcast_convert_type(xb[:, :n], U32) >> 16
    hi = lax.bitcast_convert_type(xb[:, n:], U32) & jnp.uint32(0xFFFF0000)
    return lo | hi


def _unpack_bf16_pairs(p):
    lo = lax.bitcast_convert_type(p << 16, F32).astype(BF16)
    hi = lax.bitcast_convert_type(p & jnp.uint32(0xFFFF0000), F32).astype(BF16)
    return lo, hi


def _mod_kernel(c_ref, w_ref, b_ref, o_ref):
    c = c_ref[...]
    o_ref[0] = jnp.dot(_silu(c), w_ref[0], preferred_element_type=F32) + b_ref[0]


def _modulation(c, ada_w, ada_b):
    depth, d, n = ada_w.shape
    bsz = c.shape[0]
    rows = 8
    c_pad = jnp.zeros((rows, d), F32).at[:bsz].set(c)
    tn = 1536 if n % 1536 == 0 else n
    out = pl.pallas_call(
        _mod_kernel,
        grid=(depth, n // tn),
        in_specs=[
            pl.BlockSpec((rows, d), lambda l, j: (0, 0)),
            pl.BlockSpec((1, d, tn), lambda l, j: (l, 0, j)),
            pl.BlockSpec((1, 1, tn), lambda l, j: (l, 0, j)),
        ],
        out_specs=pl.BlockSpec((1, rows, tn), lambda l, j: (l, 0, j)),
        out_shape=jax.ShapeDtypeStruct((depth, rows, n), F32),
        compiler_params=_cparams(("arbitrary", "arbitrary"), 2 * d * tn * 4 + 4 * rows * tn * 4),
        name="adaln_mod",
    )(c_pad, ada_w, ada_b.reshape(depth, 1, n))
    return out[:, :bsz]


def _inproj_kernel(x_ref, g_ref, sc_ref, sh_ref, w_ref, p_ref, af_ref, h_scr, *, n_gate_tiles, af_tile):
    j = pl.program_id(1)

    @pl.when(j == 0)
    def _():
        x = x_ref[...]
        y = x * lax.rsqrt(jnp.mean(x * x, axis=-1, keepdims=True) + EPS)
        h = y * g_ref[...] * (1.0 + sc_ref[0]) + sh_ref[0]
        h_scr[...] = h.astype(BF16)

    acc = jnp.dot(h_scr[...], w_ref[...], preferred_element_type=F32)

    @pl.when(j < n_gate_tiles)
    def _():
        p_ref[...] = _sigmoid(acc).astype(BF16)

    @pl.when(j >= n_gate_tiles)
    def _():
        p_ref[...] = acc.astype(BF16)

    @pl.when(j == af_tile)
    def _():
        af_ref[...] = acc


def _in_projection(x2, seq, g, sc, sh, w_bf16, d_model):
    t, d = x2.shape
    n = w_bf16.shape[1]
    tn = A_DIM
    tm = min(1024, seq)
    tiles_per_batch = seq // tm
    n_gate_tiles = (3 * d_model) // tn
    af_tile = n_gate_tiles + 1
    kern = functools.partial(_inproj_kernel, n_gate_tiles=n_gate_tiles, af_tile=af_tile)
    est = 2 * tm * d * 4 + 2 * d * tn * 2 + 2 * tm * tn * 2 + 2 * tm * tn * 4 + tm * d * 2 + 2 * tm * tn * 4
    p, af = pl.pallas_call(
        kern,
        grid=(t // tm, n // tn),
        in_specs=[
            pl.BlockSpec((tm, d), lambda i, j: (i, 0)),
            pl.BlockSpec((1, d), lambda i, j: (0, 0)),
            pl.BlockSpec((1, 1, d), lambda i, j: (i // tiles_per_batch, 0, 0)),
            pl.BlockSpec((1, 1, d), lambda i, j: (i // tiles_per_batch, 0, 0)),
            pl.BlockSpec((d, tn), lambda i, j: (0, j)),
        ],
        out_specs=[
            pl.BlockSpec((tm, tn), lambda i, j: (i, j)),
            pl.BlockSpec((tm, tn), lambda i, j: (i, 0)),
        ],
        out_shape=[jax.ShapeDtypeStruct((t, n), BF16), jax.ShapeDtypeStruct((t, tn), F32)],
        scratch_shapes=[pltpu.VMEM((tm, d), BF16)],
        compiler_params=_cparams(("arbitrary", "arbitrary"), est),
        name="norm_inproj",
    )(x2, g.reshape(1, d), sc, sh, w_bf16)
    return p, af


def _hgrn_kernel(aq_ref, af_ref, ai_ref, ag_ref, loglb_ref, log1mlb_ref, og_ref, o_ref,
                 st_scr, q_scr, k_scr, v_scr, b_scr, p_scr, r_scr, *, n_chunks):
    @pl.when(pl.program_id(1) == 0)
    def _():
        st_scr[...] = jnp.zeros_like(st_scr)

    hd = A_HEAD_DIM
    n_sub = CHUNK // SUB
    row_c = lax.broadcasted_iota(I32, (CHUNK, CHUNK), 0)
    col_c = lax.broadcasted_iota(I32, (CHUNK, CHUNK), 1)
    tri = (col_c <= row_c).astype(BF16)
    ones_blk = jnp.ones((hd, hd), BF16)
    sub_row = lax.broadcasted_iota(I32, (SUB, 1), 0)
    log_lb = loglb_ref[...]
    log1m_lb = log1mlb_ref[...]
    og = og_ref[...]

    def chunk_body(ci, carry):
        r0 = pl.multiple_of(ci * CHUNK, CHUNK)
        rows = pl.ds(r0, CHUNK)
        af = af_ref[rows, :]
        ls = jnp.minimum(af, 0.0) - jnp.log(1.0 + jnp.exp(-jnp.abs(af)))
        t2 = log1m_lb + ls
        mx = jnp.maximum(log_lb, t2)
        lf = mx + jnp.log(1.0 + jnp.exp(-jnp.abs(log_lb - t2)))
        k_scr[...] = 1.0 - jnp.exp(lf)
        hi, mid, lo = _split3(lf)
        b = (jnp.dot(tri, hi, preferred_element_type=F32) + jnp.dot(tri, mid, preferred_element_type=F32)
             + jnp.dot(tri, lo, preferred_element_type=F32))
        b_scr[...] = b
        q_scr[...] = aq_ref[rows, :].astype(F32) * (A_HEAD_DIM ** -0.5)
        v_scr[...] = ai_ref[rows, :].astype(F32)

        for i in range(n_sub):
            s0 = i * SUB
            b_blk = b_scr[pl.ds(s0, SUB), :]
            q_blk = q_scr[pl.ds(s0, SUB), :]
            for s in range(SUB):
                b_s = b_scr[pl.ds(s0 + s, 1), :]
                k_s = k_scr[pl.ds(s0 + s, 1), :]
                e = jnp.exp(jnp.where(sub_row >= s, b_blk - b_s, NEG_BIG))
                p_scr[pl.ds((s0 + s) * SUB, SUB), :] = (q_blk * k_s * e).astype(BF16)
        for h in range(A_HEADS):
            cols = slice(h * hd, (h + 1) * hd)
            r_scr[:, cols] = jnp.dot(p_scr[:, cols], ones_blk, preferred_element_type=F32)

        b_last = b_scr[pl.ds(CHUNK - 1, 1), :]
        outs = []
        for h in range(A_HEADS):
            cols = slice(h * hd, (h + 1) * hd)
            qh = q_scr[:, cols]
            kh = k_scr[:, cols]
            vh = v_scr[:, cols]
            bh = b_scr[:, cols]
            st = st_scr[h]
            qt = (qh * jnp.exp(bh)).astype(BF16)
            inter = lax.dot_general(qt, st.astype(BF16), (((1,), (1,)), ((), ())), preferred_element_type=F32)
            blocks = []
            for i in range(n_sub):
                s0 = i * SUB
                acc = inter[s0:s0 + SUB]
                for s in range(SUB):
                    acc = acc + r_scr[pl.ds((s0 + s) * SUB, SUB), cols] * v_scr[pl.ds(s0 + s, 1), cols]
                if i > 0:
                    beta = b_scr[pl.ds(s0 - 1, 1), cols]
                    qhat = (qh[s0:s0 + SUB] * jnp.exp(bh[s0:s0 + SUB] - beta)).astype(BF16)
                    khat = (kh[:s0] * jnp.exp(beta - bh[:s0])).astype(BF16)
                    a = lax.dot_general(qhat, khat, (((1,), (1,)), ((), ())), preferred_element_type=F32)
                    acc = acc + jnp.dot(a.astype(BF16), vh[:s0].astype(BF16), preferred_element_type=F32)
                blocks.append(acc)
            o = jnp.concatenate(blocks, axis=0)
            bl = b_last[:, cols]
            ktil = (kh * jnp.exp(bl - bh)).astype(BF16)
            upd = lax.dot_general(vh.astype(BF16), ktil, (((0,), (0,)), ((), ())), preferred_element_type=F32)
            st_scr[h] = st * jnp.exp(bl) + upd
            o = o * lax.rsqrt(jnp.mean(o * o, axis=-1, keepdims=True) + EPS)
            outs.append(o)
        o_all = jnp.concatenate(outs, axis=1)
        gate = ag_ref[rows, :].astype(F32)
        o_ref[rows, :] = (o_all * og * _silu(gate)).astype(BF16)
        return carry

    lax.fori_loop(0, n_chunks, chunk_body, 0)


def _hgrn(p, af, log_lb, log1m_lb, onorm_g, bsz, seq, aq_blk):
    t = p.shape[0]
    ts = min(512, seq)
    n_seq_tiles = seq // ts
    n_chunks = ts // CHUNK
    kern = functools.partial(_hgrn_kernel, n_chunks=n_chunks)
    row = lambda b, i: b * n_seq_tiles + i
    est = (2 * 3 * ts * A_DIM * 2 + 2 * ts * A_DIM * 4 + 2 * ts * A_DIM * 2 + A_HEADS * 128 * 128 * 4
           + 4 * CHUNK * A_DIM * 4 + CHUNK * SUB * A_DIM * 6)
    return pl.pallas_call(
        kern,
        grid=(bsz, n_seq_tiles),
        in_specs=[
            pl.BlockSpec((ts, A_DIM), lambda b, i: (row(b, i), aq_blk)),
            pl.BlockSpec((ts, A_DIM), lambda b, i: (row(b, i), 0)),
            pl.BlockSpec((ts, A_DIM), lambda b, i: (row(b, i), aq_blk + 2)),
            pl.BlockSpec((ts, A_DIM), lambda b, i: (row(b, i), aq_blk + 3)),
            pl.BlockSpec((1, A_DIM), lambda b, i: (0, 0)),
            pl.BlockSpec((1, A_DIM), lambda b, i: (0, 0)),
            pl.BlockSpec((1, A_DIM), lambda b, i: (0, 0)),
        ],
        out_specs=pl.BlockSpec((ts, A_DIM), lambda b, i: (row(b, i), 0)),
        out_shape=jax.ShapeDtypeStruct((t, A_DIM), BF16),
        scratch_shapes=[
            pltpu.VMEM((A_HEADS, A_HEAD_DIM, A_HEAD_DIM), F32),
            pltpu.VMEM((CHUNK, A_DIM), F32),
            pltpu.VMEM((CHUNK, A_DIM), F32),
            pltpu.VMEM((CHUNK, A_DIM), F32),
            pltpu.VMEM((CHUNK, A_DIM), F32),
            pltpu.VMEM((CHUNK * SUB, A_DIM), BF16),
            pltpu.VMEM((CHUNK * SUB, A_DIM), F32),
        ],
        compiler_params=_cparams(("arbitrary", "arbitrary"), est),
        name="hgrn2_scan",
    )(p, af, p, p, log_lb.reshape(1, A_DIM), log1m_lb.reshape(1, A_DIM), onorm_g.reshape(1, A_DIM))


def _attn_kernel(q_ref, k0_ref, k1_ref, k2_ref, v0_ref, v1_ref, v2_ref, bias_ref, o_ref):
    i = pl.program_id(1)
    tq = q_ref.shape[0]
    hd = C_HEAD_DIM
    scale = C_HEAD_DIM ** -0.5
    col = lax.broadcasted_iota(I32, (1, 3 * tq), 1)
    valid = (col >= 2 * tq) | ((col >= tq) & (i >= 1)) | (i >= 2)
    for h in range(C_HEADS):
        cols = slice(h * hd, (h + 1) * hd)
        qh = q_ref[:, cols]
        kh = jnp.concatenate([k0_ref[:, cols], k1_ref[:, cols], k2_ref[:, cols]], axis=0)
        vh = jnp.concatenate([v0_ref[:, cols], v1_ref[:, cols], v2_ref[:, cols]], axis=0)
        s = lax.dot_general(qh, kh, (((1,), (1,)), ((), ())), preferred_element_type=F32) * scale + bias_ref[h]
        s = jnp.where(valid, s, NEG_BIG)
        m = jnp.max(s, axis=-1, keepdims=True)
        e = jnp.exp(s - m)
        pr = e / jnp.sum(e, axis=-1, keepdims=True)
        o_ref[:, cols] = jnp.dot(pr.astype(BF16), vh, preferred_element_type=F32).astype(BF16)


def _attention_bias_table(rel_bias, tq):
    band = (LEFT_CHUNKS + 1) * CHUNK
    assert tq * 2 == LEFT_CHUNKS * CHUNK, "three key slots of tq rows must cover the band"
    diff = np.arange(CHUNK + band - 1) - (band - 1)
    ext = rel_bias.astype(F32)[:, np.clip(diff + LEFT_CHUNKS * CHUNK, -REL_CLIP, REL_CLIP) + REL_CLIP]
    rev = ext[:, ::-1]
    chunk_tab = jnp.stack([rev[:, CHUNK - 1 - tt:CHUNK - 1 - tt + band] for tt in range(CHUNK)], axis=1)
    rows = [jnp.pad(chunk_tab, ((0, 0), (0, 0), (c * CHUNK, 3 * tq - band - c * CHUNK)), constant_values=NEG_BIG)
            for c in range(tq // CHUNK)]
    return jnp.concatenate(rows, axis=1)


def _attention(p, rel_bias, bsz, seq, q_blk):
    t = p.shape[0]
    tq = ATT_TQ
    nq = seq // tq
    bias = _attention_bias_table(rel_bias, tq)
    row = lambda b, i, back: b * nq + jnp.maximum(i - back, 0)
    kv_spec = lambda blk, back: pl.BlockSpec((tq, C_DIM), lambda b, i: (row(b, i, back), blk))
    est = 2 * 7 * tq * C_DIM * 2 + 2 * C_HEADS * tq * 3 * tq * 4 + 2 * tq * C_DIM * 2 + 6 * tq * 3 * tq * 4
    return pl.pallas_call(
        _attn_kernel,
        grid=(bsz, nq),
        in_specs=[
            pl.BlockSpec((tq, C_DIM), lambda b, i: (b * nq + i, q_blk)),
            kv_spec(q_blk + 1, 2), kv_spec(q_blk + 1, 1), kv_spec(q_blk + 1, 0),
            kv_spec(q_blk + 2, 2), kv_spec(q_blk + 2, 1), kv_spec(q_blk + 2, 0),
            pl.BlockSpec((C_HEADS, tq, 3 * tq), lambda b, i: (0, 0, 0)),
        ],
        out_specs=pl.BlockSpec((tq, C_DIM), lambda b, i: (b * nq + i, 0)),
        out_shape=jax.ShapeDtypeStruct((t, C_DIM), BF16),
        compiler_params=_cparams(("arbitrary", "arbitrary"), est),
        name="chunk_attention",
    )(p, p, p, p, p, p, p, bias)


def _merge_kernel(x_ref, gmod_ref, a_ref, c_ref, bb_ref, bc_ref, bu_ref, hbc_ref, hbu_ref, cw_ref,
                  ga_ref, gb_ref, gc_ref, wa_ref, wb_ref, wc_ref, wo_ref, o_ref, *, tiles_per_batch):
    i = pl.program_id(0)
    tm = x_ref.shape[0]
    u = bc_ref[...].astype(F32) * bu_ref[...].astype(F32)
    halo = hbc_ref[...].astype(F32) * hbu_ref[...].astype(F32)
    halo = jnp.where(i % tiles_per_batch == 0, 0.0, halo)
    hrows = halo.shape[0]
    hm1 = halo[hrows - 1:hrows]
    hm2 = halo[hrows - 2:hrows - 1]
    r = lax.broadcasted_iota(I32, (tm, 1), 0)
    u1 = jnp.where(r == 0, hm1, pltpu.roll(u, 1, 0))
    u2 = jnp.where(r == 0, hm2, jnp.where(r == 1, hm1, pltpu.roll(u, 2, 0)))
    cw = cw_ref[...]
    conv = u2 * cw[0:1] + u1 * cw[1:2] + u * cw[2:3]
    bfeat = (bb_ref[...].astype(F32) * conv).astype(BF16)
    ya = jnp.dot(a_ref[...], wa_ref[...], preferred_element_type=F32)
    yb = jnp.dot(bfeat, wb_ref[...], preferred_element_type=F32)
    yc = jnp.dot(c_ref[...], wc_ref[...], preferred_element_type=F32)
    merged = (ga_ref[...].astype(F32) * ya + gb_ref[...].astype(F32) * yb + gc_ref[...].astype(F32) * yc)
    y = jnp.dot(merged.astype(BF16), wo_ref[...], preferred_element_type=F32)
    o_ref[...] = x_ref[...] + gmod_ref[0] * y


def _merge(x2, seq, gmod, a_feat, c_feat, p, conv_w, wa, wb, wc, wo, b_blk):
    t, d = x2.shape
    tm = min(256, seq)
    tiles_per_batch = seq // tm
    halo = 16
    hb = tm // halo
    kern = functools.partial(_merge_kernel, tiles_per_batch=tiles_per_batch)
    const = lambda shape: pl.BlockSpec(shape, lambda i: (0,) * len(shape))
    halo_spec = lambda blk: pl.BlockSpec((halo, B_WIDTH), lambda i: (jnp.maximum(i * hb - 1, 0), blk))
    est = (4 * tm * d * 4 + 4 * tm * A_DIM * 2 + 6 * tm * B_WIDTH * 2 + 6 * tm * d * 2
           + 2 * (A_DIM + B_WIDTH + C_DIM + d) * d * 2 + 6 * tm * d * 4)
    return pl.pallas_call(
        kern,
        grid=(t // tm,),
        in_specs=[
            pl.BlockSpec((tm, d), lambda i: (i, 0)),
            pl.BlockSpec((1, 1, d), lambda i: (i // tiles_per_batch, 0, 0)),
            pl.BlockSpec((tm, A_DIM), lambda i: (i, 0)),
            pl.BlockSpec((tm, C_DIM), lambda i: (i, 0)),
            pl.BlockSpec((tm, B_WIDTH), lambda i: (i, b_blk)),
            pl.BlockSpec((tm, B_WIDTH), lambda i: (i, b_blk + 1)),
            pl.BlockSpec((tm, B_WIDTH), lambda i: (i, b_blk + 2)),
            halo_spec(b_blk + 1),
            halo_spec(b_blk + 2),
            const((CONV_W, B_WIDTH)),
            pl.BlockSpec((tm, d), lambda i: (i, 0)),
            pl.BlockSpec((tm, d), lambda i: (i, 1)),
            pl.BlockSpec((tm, d), lambda i: (i, 2)),
            const((A_DIM, d)), const((B_WIDTH, d)), const((C_DIM, d)), const((d, d)),
        ],
        out_specs=pl.BlockSpec((tm, d), lambda i: (i, 0)),
        out_shape=jax.ShapeDtypeStruct((t, d), F32),
        compiler_params=_cparams(("arbitrary",), est),
        name="merge_outproj",
    )(x2, gmod, a_feat, c_feat, p, p, p, p, p, conv_w, p, p, p, wa, wb, wc, wo)


def _first_max(vals, iota, n, axis):
    m = jnp.max(vals, axis=axis, keepdims=True)
    idx = jnp.min(jnp.where(vals == m, iota, n), axis=axis, keepdims=True)
    return m, idx


def _router_kernel(x_ref, g_ref, sc_ref, sh_ref, rwt_ref, rb_ref, h_ref, hp_ref, e_ref, w_ref, rank_ref, cnt_ref,
                   cnt_scr):
    step = pl.program_id(0)

    @pl.when(step == 0)
    def _():
        cnt_scr[...] = jnp.zeros_like(cnt_scr)

    tm = x_ref.shape[0]
    x = x_ref[...]
    y = x * lax.rsqrt(jnp.mean(x * x, axis=-1, keepdims=True) + EPS)
    h = y * g_ref[...] * (1.0 + sc_ref[0]) + sh_ref[0]
    h_ref[...] = h.astype(BF16)
    hp_ref[...] = _pack_bf16_pairs(h)

    nt = (((1,), (1,)), ((), ()))
    h_hi, h_mid, _ = _split3(h)
    w_hi, w_mid, _ = _split3(rwt_ref[...])
    logits = (lax.dot_general(w_hi, h_hi, nt, preferred_element_type=F32)
              + lax.dot_general(w_hi, h_mid, nt, preferred_element_type=F32)
              + lax.dot_general(w_mid, h_hi, nt, preferred_element_type=F32))
    scores = _sigmoid(logits)
    sel = scores + rb_ref[:, 0:1]

    g3 = sel.reshape(N_GROUPS, GROUP_SIZE, tm)
    j3 = lax.broadcasted_iota(I32, g3.shape, 1)
    m1, i1 = _first_max(g3, j3, GROUP_SIZE, 1)
    m2 = jnp.max(jnp.where(j3 == i1, -jnp.inf, g3), axis=1, keepdims=True)
    gs = (m1 + m2).reshape(N_GROUPS, tm)
    gi = lax.broadcasted_iota(I32, gs.shape, 0)
    gmask = jnp.zeros(gs.shape, jnp.bool_)
    for _ in range(TOPK_GROUPS):
        _, idx = _first_max(gs, gi, N_GROUPS, 0)
        hit = gi == idx
        gmask = gmask | hit
        gs = jnp.where(hit, -jnp.inf, gs)
    emask = jnp.broadcast_to(gmask.reshape(N_GROUPS, 1, tm), (N_GROUPS, GROUP_SIZE, tm)).reshape(N_EXPERTS, tm)
    masked = jnp.where(emask, sel, -jnp.inf)

    ei = lax.broadcasted_iota(I32, (N_EXPERTS, tm), 0)
    chosen = jnp.zeros((N_EXPERTS, tm), F32)
    e_rows, w_rows = [], []
    for _ in range(TOP_K):
        _, idx = _first_max(masked, ei, N_EXPERTS, 0)
        hit = ei == idx
        e_rows.append(idx)
        w_rows.append(jnp.sum(jnp.where(hit, scores, 0.0), axis=0, keepdims=True))
        chosen = jnp.where(hit, 1.0, chosen)
        masked = jnp.where(hit, -jnp.inf, masked)
    eidx = jnp.concatenate(e_rows, axis=0)
    wts = jnp.concatenate(w_rows, axis=0)
    wts = wts / jnp.sum(wts, axis=0, keepdims=True) * ROUTED_SCALE
    e_ref[...] = eidx
    w_ref[...] = wts

    tr = lax.broadcasted_iota(I32, (tm, tm), 0)
    tc = lax.broadcasted_iota(I32, (tm, tm), 1)
    before = (tr < tc).astype(BF16)
    base = cnt_scr[:, 0:1]
    rank_all = jnp.dot(chosen.astype(BF16), before, preferred_element_type=F32) + base
    ranks = [jnp.sum(jnp.where(ei == e_rows[k], rank_all, 0.0), axis=0, keepdims=True) for k in range(TOP_K)]
    rank_ref[...] = jnp.concatenate(ranks, axis=0).astype(I32)
    total = base + jnp.sum(chosen, axis=1, keepdims=True)
    cnt_scr[...] = jnp.broadcast_to(total, cnt_scr.shape)
    cnt_ref[...] = jnp.broadcast_to(total, cnt_ref.shape).astype(I32)


def _router(x2, seq, g, sc, sh, router_wt, router_bias):
    t, d = x2.shape
    tm = min(512, seq)
    tiles_per_batch = seq // tm
    est = (2 * tm * d * 4 + 2 * tm * d * 2 + 2 * tm * d * 2 + 2 * N_EXPERTS * d * 4 + tm * tm * 2
           + 8 * tm * d * 4 + 16 * N_EXPERTS * tm * 4)
    return pl.pallas_call(
        _router_kernel,
        grid=(t // tm,),
        in_specs=[
            pl.BlockSpec((tm, d), lambda i: (i, 0)),
            pl.BlockSpec((1, d), lambda i: (0, 0)),
            pl.BlockSpec((1, 1, d), lambda i: (i // tiles_per_batch, 0, 0)),
            pl.BlockSpec((1, 1, d), lambda i: (i // tiles_per_batch, 0, 0)),
            pl.BlockSpec((N_EXPERTS, d), lambda i: (0, 0)),
            pl.BlockSpec((N_EXPERTS, LANES), lambda i: (0, 0)),
        ],
        out_specs=[
            pl.BlockSpec((tm, d), lambda i: (i, 0)),
            pl.BlockSpec((tm, d // 2), lambda i: (i, 0)),
            pl.BlockSpec((TOP_K, tm), lambda i: (0, i)),
            pl.BlockSpec((TOP_K, tm), lambda i: (0, i)),
            pl.BlockSpec((TOP_K, tm), lambda i: (0, i)),
            pl.BlockSpec((N_EXPERTS, LANES), lambda i: (0, 0)),
        ],
        out_shape=[
            jax.ShapeDtypeStruct((t, d), BF16),
            jax.ShapeDtypeStruct((t, d // 2), U32),
            jax.ShapeDtypeStruct((TOP_K, t), I32),
            jax.ShapeDtypeStruct((TOP_K, t), F32),
            jax.ShapeDtypeStruct((TOP_K, t), I32),
            jax.ShapeDtypeStruct((N_EXPERTS, LANES), I32),
        ],
        scratch_shapes=[pltpu.VMEM((N_EXPERTS, LANES), F32)],
        compiler_params=_cparams(("arbitrary",), est),
        name="norm_router",
    )(x2, g.reshape(1, d), sc, sh, router_wt, jnp.broadcast_to(router_bias.reshape(N_EXPERTS, 1), (N_EXPERTS, LANES)))


def _pos_kernel(start_ref, e_ref, rank_ref, pos_ref):
    e = e_ref[...]
    acc = rank_ref[...]
    for ex in range(N_EXPERTS):
        acc = acc + jnp.where(e == ex, start_ref[ex], 0)
    pos_ref[...] = acc


def _positions(pstart, eidx, rank):
    k, t = eidx.shape
    tl = min(4096, t)
    return pl.pallas_call(
        _pos_kernel,
        grid_spec=pltpu.PrefetchScalarGridSpec(
            num_scalar_prefetch=1,
            grid=(t // tl,),
            in_specs=[pl.BlockSpec((k, tl), lambda i, s: (0, i)), pl.BlockSpec((k, tl), lambda i, s: (0, i))],
            out_specs=pl.BlockSpec((k, tl), lambda i, s: (0, i)),
        ),
        out_shape=jax.ShapeDtypeStruct((k, t), I32),
        compiler_params=_cparams(("arbitrary",), 8 * k * tl * 4),
        name="moe_positions",
    )(pstart, eidx, rank)


def _dispatch_kernel(pos_ref, hp_ref, xs_ref, sem, *, tm):
    def copy(t, k):
        return pltpu.make_async_copy(hp_ref.at[pl.ds(t, 1)], xs_ref.at[pl.ds(pos_ref[k, t], 1)], sem)

    def start(t, c):
        for k in range(TOP_K):
            copy(t, k).start()
        return c

    def wait(t, c):
        for k in range(TOP_K):
            copy(t, k).wait()
        return c

    lax.fori_loop(0, tm, start, 0)
    lax.fori_loop(0, tm, wait, 0)


def _dispatch(pos, hp, n_rows):
    t, w = hp.shape
    tm = min(512, t)
    kern = functools.partial(_dispatch_kernel, tm=tm)
    return pl.pallas_call(
        kern,
        grid=(t // tm,),
        in_specs=[
            pl.BlockSpec((TOP_K, tm), lambda i: (0, i), memory_space=pltpu.SMEM),
            pl.BlockSpec((tm, w), lambda i: (i, 0)),
        ],
        out_specs=pl.BlockSpec(memory_space=pl.ANY),
        out_shape=jax.ShapeDtypeStruct((n_rows, w), U32),
        scratch_shapes=[pltpu.SemaphoreType.DMA(())],
        compiler_params=pltpu.CompilerParams(dimension_semantics=("arbitrary",), has_side_effects=True,
                                             vmem_limit_bytes=_vmem_limit(4 * tm * w * 4)),
        name="moe_dispatch",
    )(pos, hp)


def _expert_kernel(blk_exp_ref, n_used_ref, xs_ref, wg_ref, wu_ref, wd_ref, ys_ref, wg_scr, wu_scr, wd_scr):
    b = pl.program_id(0)
    new_expert = (b == 0) | (blk_exp_ref[b] != blk_exp_ref[jnp.maximum(b - 1, 0)])

    @pl.when(new_expert)
    def _():
        wg_scr[...] = wg_ref[0].astype(BF16)
        wu_scr[...] = wu_ref[0].astype(BF16)
        wd_scr[...] = wd_ref[0].astype(BF16)

    @pl.when(b < n_used_ref[0])
    def _():
        lo, hi = _unpack_bf16_pairs(xs_ref[...])
        half = lo.shape[1]
        gate = (jnp.dot(lo, wg_scr[:half], preferred_element_type=F32)
                + jnp.dot(hi, wg_scr[half:], preferred_element_type=F32))
        up = (jnp.dot(lo, wu_scr[:half], preferred_element_type=F32)
              + jnp.dot(hi, wu_scr[half:], preferred_element_type=F32))
        hidden = (_silu(gate) * up).astype(BF16)
        ys_ref[...] = _pack_bf16_pairs(jnp.dot(hidden, wd_scr[...], preferred_element_type=F32))

    @pl.when(b >= n_used_ref[0])
    def _():
        ys_ref[...] = jnp.zeros_like(ys_ref)


def _experts(blk_exp, n_used, xs, wg, wu, wd):
    n_rows, half = xs.shape
    d = 2 * half
    de = wg.shape[2]
    nb = n_rows // MOE_BLK
    est = 4 * MOE_BLK * half * 4 + 2 * 3 * d * de * 4 + 3 * d * de * 2 + MOE_BLK * (2 * de + d) * 4
    return pl.pallas_call(
        _expert_kernel,
        grid_spec=pltpu.PrefetchScalarGridSpec(
            num_scalar_prefetch=2,
            grid=(nb,),
            in_specs=[
                pl.BlockSpec((MOE_BLK, half), lambda b, be, nu: (b, 0)),
                pl.BlockSpec((1, d, de), lambda b, be, nu: (be[b], 0, 0)),
                pl.BlockSpec((1, d, de), lambda b, be, nu: (be[b], 0, 0)),
                pl.BlockSpec((1, de, d), lambda b, be, nu: (be[b], 0, 0)),
            ],
            out_specs=pl.BlockSpec((MOE_BLK, half), lambda b, be, nu: (b, 0)),
            scratch_shapes=[pltpu.VMEM((d, de), BF16), pltpu.VMEM((d, de), BF16), pltpu.VMEM((de, d), BF16)],
        ),
        out_shape=jax.ShapeDtypeStruct((n_rows, half), U32),
        compiler_params=_cparams(("arbitrary",), est),
        name="moe_experts",
    )(blk_exp, n_used, xs, wg, wu, wd)


def _combine_kernel(pos_ref, x_ref, gmod_ref, h_ref, wt_ref, sg_ref, su_ref, sd_ref, ys_ref, o_ref, yg_scr, sem,
                    *, tm):
    def copy(t, k):
        return pltpu.make_async_copy(ys_ref.at[pl.ds(pos_ref[k, t], 1)], yg_scr.at[k, pl.ds(t, 1)], sem)

    def start(t, c):
        for k in range(TOP_K):
            copy(t, k).start()
        return c

    def wait(t, c):
        for k in range(TOP_K):
            copy(t, k).wait()
        return c

    lax.fori_loop(0, tm, start, 0)
    h = h_ref[...]
    hidden = (_silu(jnp.dot(h, sg_ref[...], preferred_element_type=F32))
              * jnp.dot(h, su_ref[...], preferred_element_type=F32)).astype(BF16)
    shared = jnp.dot(hidden, sd_ref[...], preferred_element_type=F32)
    lax.fori_loop(0, tm, wait, 0)
    half = yg_scr.shape[2]
    acc_lo = shared[:, :half]
    acc_hi = shared[:, half:]
    wt = wt_ref[...]
    for k in range(TOP_K):
        lo, hi = _unpack_bf16_pairs(yg_scr[k])
        wk = wt[:, k:k + 1]
        acc_lo = acc_lo + wk * lo.astype(F32)
        acc_hi = acc_hi + wk * hi.astype(F32)
    g = gmod_ref[0]
    o_ref[:, :half] = x_ref[:, :half] + g[:, :half] * acc_lo
    o_ref[:, half:] = x_ref[:, half:] + g[:, half:] * acc_hi


def _combine(pos, x2, seq, gmod, h, wts_t, sg, su, sd, ys):
    t, d = x2.shape
    half = d // 2
    ds_ = sg.shape[1]
    tm = min(256, seq)
    tiles_per_batch = seq // tm
    kern = functools.partial(_combine_kernel, tm=tm)
    const = lambda shape: pl.BlockSpec(shape, lambda i: (0,) * len(shape))
    est = 4 * tm * d * 4 + 2 * tm * d * 2 + 2 * 3 * d * ds_ * 2 + TOP_K * tm * half * 4 + 8 * tm * d * 4
    return pl.pallas_call(
        kern,
        grid=(t // tm,),
        in_specs=[
            pl.BlockSpec((TOP_K, tm), lambda i: (0, i), memory_space=pltpu.SMEM),
            pl.BlockSpec((tm, d), lambda i: (i, 0)),
            pl.BlockSpec((1, 1, d), lambda i: (i // tiles_per_batch, 0, 0)),
            pl.BlockSpec((tm, d), lambda i: (i, 0)),
            pl.BlockSpec((tm, TOP_K), lambda i: (i, 0)),
            const((d, ds_)), const((d, ds_)), const((ds_, d)),
            pl.BlockSpec(memory_space=pl.ANY),
        ],
        out_specs=pl.BlockSpec((tm, d), lambda i: (i, 0)),
        out_shape=jax.ShapeDtypeStruct((t, d), F32),
        scratch_shapes=[pltpu.VMEM((TOP_K, tm, half), U32), pltpu.SemaphoreType.DMA(())],
        compiler_params=_cparams(("arbitrary",), est),
        name="moe_combine",
    )(pos, x2, gmod, h, wts_t, sg, su, sd, ys)


def _final_kernel(x_ref, g_ref, o_ref):
    x = x_ref[...]
    o_ref[...] = x * lax.rsqrt(jnp.mean(x * x, axis=-1, keepdims=True) + EPS) * g_ref[...]


def _final_norm(x2, g):
    t, d = x2.shape
    tm = min(1024, t)
    return pl.pallas_call(
        _final_kernel,
        grid=(t // tm,),
        in_specs=[pl.BlockSpec((tm, d), lambda i: (i, 0)), pl.BlockSpec((1, d), lambda i: (0, 0))],
        out_specs=pl.BlockSpec((tm, d), lambda i: (i, 0)),
        out_shape=jax.ShapeDtypeStruct((t, d), F32),
        compiler_params=_cparams(("arbitrary",), 4 * tm * d * 4),
        name="final_norm",
    )(x2, g.reshape(1, d))


def _moe_layout(counts, n_blocks):
    padded = (counts + MOE_BLK - 1) // MOE_BLK * MOE_BLK
    pends = jnp.cumsum(padded)
    pstart = (pends - padded).astype(I32)
    blk_row = jnp.arange(n_blocks, dtype=I32)[:, None] * MOE_BLK
    blk_exp = jnp.minimum(jnp.sum((pends[None, :] <= blk_row).astype(I32), axis=1), N_EXPERTS - 1)
    n_used = (pends[-1:] // MOE_BLK).astype(I32)
    return pstart, blk_exp, n_used


def _layer(x2, bsz, seq, mod, norm1_g, w_in, lb, onorm_g, conv_w, rel_bias, w_a, w_b, w_c, w_out, norm2_g,
           router_w, router_bias, wg, wu, wd, sg, su, sd):
    t, d = x2.shape
    sh1, sc1, g1, sh2, sc2, g2 = [m.reshape(bsz, 1, d) for m in jnp.split(mod, N_MOD, axis=-1)]
    n_mix = w_in.shape[1] - 3 * d
    w_re = jnp.concatenate([w_in[:, n_mix:], w_in[:, :n_mix]], axis=1).astype(BF16)
    p, af = _in_projection(x2, seq, norm1_g, sc1, sh1, w_re, d)
    aq_blk = (3 * d) // A_DIM
    b_blk = (3 * d + 4 * A_DIM) // B_WIDTH
    q_blk = (3 * d + 4 * A_DIM + 3 * B_WIDTH) // C_DIM
    a_feat = _hgrn(p, af, jnp.log(lb), jnp.log1p(-lb), onorm_g, bsz, seq, aq_blk)
    c_feat = _attention(p, rel_bias, bsz, seq, q_blk)
    x2 = _merge(x2, seq, g1, a_feat, c_feat, p, conv_w, w_a.astype(BF16), w_b.astype(BF16), w_c.astype(BF16),
                w_out.astype(BF16), b_blk)

    h, hp, eidx, wts, rank, cnt = _router(x2, seq, norm2_g, sc2, sh2, router_w.T, router_bias)
    n_blocks = (t * TOP_K) // MOE_BLK + N_EXPERTS
    pstart, blk_exp, n_used = _moe_layout(cnt[:, 0], n_blocks)
    pos = _positions(pstart, eidx, rank)
    xs = _dispatch(pos, hp, n_blocks * MOE_BLK)
    ys = _experts(blk_exp, n_used, xs, wg, wu, wd)
    return _combine(pos, x2, seq, g2, h, wts.T, sg.astype(BF16), su.astype(BF16), sd.astype(BF16), ys)


def kernel(x, c, ada_w, ada_b, norm1_g, w_in, hgrn_lb_logits, hgrn_onorm_g, conv_w, rel_bias, w_branch_a, w_branch_b, w_branch_c, w_out, norm2_g, router_w, router_bias, moe_w_gate, moe_w_up, moe_w_down, shared_w_gate, shared_w_up, shared_w_down, final_g):
    bsz, seq, d = x.shape
    depth = ada_w.shape[0]
    mod = _modulation(c, ada_w, ada_b)
    lbs = jnp.cumsum(jax.nn.softmax(hgrn_lb_logits.astype(F32), axis=0), axis=0)
    lbs = lbs - lbs[0]
    x2 = x.reshape(bsz * seq, d)
    for l in range(depth):
        x2 = _layer(x2, bsz, seq, mod[l], norm1_g[l], w_in[l], lbs[l], hgrn_onorm_g[l], conv_w[l], rel_bias[l],
                    w_branch_a[l], w_branch_b[l], w_branch_c[l], w_out[l], norm2_g[l], router_w[l], router_bias[l],
                    moe_w_gate[l], moe_w_up[l], moe_w_down[l], shared_w_gate[l], shared_w_up[l], shared_w_down[l])
    return _final_norm(x2, final_g).reshape(bsz, seq, d)
```

```python
import functools

import numpy as np
import jax
import jax.numpy as jnp
from jax import lax
from jax.experimental import pallas as pl
from jax.experimental.pallas import tpu as pltpu

F32 = jnp.float32
BF16 = jnp.bfloat16
I32 = jnp.int32
U32 = jnp.uint32

CHUNK = 64
EPS = 1e-6
A_HEADS = 6
A_HEAD_DIM = 128
A_DIM = A_HEADS * A_HEAD_DIM
B_WIDTH = 512
CONV_W = 3
C_HEADS = 6
C_HEAD_DIM = 128
C_DIM = C_HEADS * C_HEAD_DIM
LEFT_CHUNKS = 8
REL_CLIP = 256
N_EXPERTS = 64
TOP_K = 8
N_GROUPS = 8
GROUP_SIZE = N_EXPERTS // N_GROUPS
TOPK_GROUPS = 4
ROUTED_SCALE = 2.5
N_MOD = 6

V7X_VMEM_BYTES = 64 * 1024 * 1024
LANES = 128
NEG_BIG = -1e30

SUB = 16
MOE_BLK = 512
ATT_TQ = 256


def _vmem_limit(nbytes):
    return int(min(max(nbytes * 3 // 2, 16 * 1024 * 1024), V7X_VMEM_BYTES - 8 * 1024 * 1024))


def _cparams(sem, nbytes):
    return pltpu.CompilerParams(dimension_semantics=sem, vmem_limit_bytes=_vmem_limit(nbytes))


def _sigmoid(x):
    return 1.0 / (1.0 + jnp.exp(-x))


def _silu(x):
    return x * _sigmoid(x)


def _split3(x):
    hi = x.astype(BF16)
    r1 = x - hi.astype(F32)
    mid = r1.astype(BF16)
    lo = (r1 - mid.astype(F32)).astype(BF16)
    return hi, mid, lo


def _pack_bf16_pairs(x):
    n = x.shape[1] // 2
    xb = x.astype(BF16).astype(F32)
    lo = lax.bitcast_convert_type(xb[:, :n], U32) >> 16
    hi = lax.bitcast_convert_type(xb[:, n:], U32) & jnp.uint32(0xFFFF0000)
    return lo | hi


def _unpack_bf16_pairs(p):
    lo = lax.bitcast_convert_type(p << 16, F32).astype(BF16)
    hi = lax.bitcast_convert_type(p & jnp.uint32(0xFFFF0000), F32).astype(BF16)
    return lo, hi


def _store_token_tiles(ref, first_token, mat):
    n, w = mat.shape
    per = w // LANES
    for s in range(per):
        ref[pl.ds(first_token * per + s, n, stride=per), :] = mat[:, s * LANES:(s + 1) * LANES]


def _load_token_tiles(ref, first_token, n, per):
    return jnp.concatenate([ref[pl.ds(first_token * per + s, n, stride=per), :] for s in range(per)], axis=1)


def _mod_kernel(c_ref, w_ref, b_ref, o_ref):
    c = c_ref[...]
    o_ref[0] = jnp.dot(_silu(c), w_ref[0], preferred_element_type=F32) + b_ref[0]


def _modulation(c, ada_w, ada_b):
    depth, d, n = ada_w.shape
    bsz = c.shape[0]
    rows = 8
    c_pad = jnp.zeros((rows, d), F32).at[:bsz].set(c)
    tn = 1536 if n % 1536 == 0 else n
    out = pl.pallas_call(
        _mod_kernel,
        grid=(depth, n // tn),
        in_specs=[
            pl.BlockSpec((rows, d), lambda l, j: (0, 0)),
            pl.BlockSpec((1, d, tn), lambda l, j: (l, 0, j)),
            pl.BlockSpec((1, 1, tn), lambda l, j: (l, 0, j)),
        ],
        out_specs=pl.BlockSpec((1, rows, tn), lambda l, j: (l, 0, j)),
        out_shape=jax.ShapeDtypeStruct((depth, rows, n), F32),
        compiler_params=_cparams(("arbitrary", "arbitrary"), 2 * d * tn * 4 + 4 * rows * tn * 4),
        name="adaln_mod",
    )(c_pad, ada_w, ada_b.reshape(depth, 1, n))
    return out[:, :bsz]


def _norm_kernel(x_ref, g_ref, sc_ref, sh_ref, h_ref):
    x = x_ref[...]
    y = x * lax.rsqrt(jnp.mean(x * x, axis=-1, keepdims=True) + EPS)
    h_ref[...] = (y * g_ref[...] * (1.0 + sc_ref[0]) + sh_ref[0]).astype(BF16)


def _norm_mod(x2, seq, g, sc, sh):
    t, d = x2.shape
    tm = min(512, seq)
    tiles_per_batch = seq // tm
    return pl.pallas_call(
        _norm_kernel,
        grid=(t // tm,),
        in_specs=[
            pl.BlockSpec((tm, d), lambda i: (i, 0)),
            pl.BlockSpec((1, d), lambda i: (0, 0)),
            pl.BlockSpec((1, 1, d), lambda i: (i // tiles_per_batch, 0, 0)),
            pl.BlockSpec((1, 1, d), lambda i: (i // tiles_per_batch, 0, 0)),
        ],
        out_specs=pl.BlockSpec((tm, d), lambda i: (i, 0)),
        out_shape=jax.ShapeDtypeStruct((t, d), BF16),
        compiler_params=_cparams(("arbitrary",), 6 * tm * d * 4),
        name="norm_mod",
    )(x2, g.reshape(1, d), sc, sh)


def _proj_kernel(cmap_ref, h_ref, w_ref, o_ref, w_scr, *, act):
    @pl.when(pl.program_id(1) == 0)
    def _():
        w_scr[...] = w_ref[...].astype(BF16)

    acc = jnp.dot(h_ref[...], w_scr[...], preferred_element_type=F32)
    if act == "sigmoid":
        acc = _sigmoid(acc)
    o_ref[...] = acc.astype(o_ref.dtype)


def _project(h, w, col_blocks, act, out_dtype, name):
    t, d = h.shape
    tn = A_DIM
    nj = len(col_blocks)
    tm = min(1024, t)
    osz = jnp.dtype(out_dtype).itemsize
    est = 2 * tm * d * 2 + 2 * d * tn * 4 + d * tn * 2 + 2 * tm * tn * osz + 2 * tm * tn * 4
    return pl.pallas_call(
        functools.partial(_proj_kernel, act=act),
        grid_spec=pltpu.PrefetchScalarGridSpec(
            num_scalar_prefetch=1,
            grid=(nj, t // tm),
            in_specs=[
                pl.BlockSpec((tm, d), lambda j, i, cm: (i, 0)),
                pl.BlockSpec((d, tn), lambda j, i, cm: (0, cm[j])),
            ],
            out_specs=pl.BlockSpec((tm, tn), lambda j, i, cm: (i, j)),
            scratch_shapes=[pltpu.VMEM((d, tn), BF16)],
        ),
        out_shape=jax.ShapeDtypeStruct((t, nj * tn), out_dtype),
        compiler_params=_cparams(("arbitrary", "arbitrary"), est),
        name=name,
    )(jnp.asarray(col_blocks, I32), h, w)


def _hgrn_kernel(aq_ref, af_ref, ai_ref, ag_ref, loglb_ref, log1mlb_ref, og_ref, o_ref,
                 st_scr, q_scr, k_scr, v_scr, b_scr, p_scr, r_scr, *, n_chunks):
    @pl.when(pl.program_id(1) == 0)
    def _():
        st_scr[...] = jnp.zeros_like(st_scr)

    hd = A_HEAD_DIM
    n_sub = CHUNK // SUB
    row_c = lax.broadcasted_iota(I32, (CHUNK, CHUNK), 0)
    col_c = lax.broadcasted_iota(I32, (CHUNK, CHUNK), 1)
    tri = (col_c <= row_c).astype(BF16)
    ones_blk = jnp.ones((hd, hd), BF16)
    sub_row = lax.broadcasted_iota(I32, (SUB, 1), 0)
    log_lb = loglb_ref[...]
    log1m_lb = log1mlb_ref[...]
    og = og_ref[...]

    def chunk_body(ci, carry):
        r0 = pl.multiple_of(ci * CHUNK, CHUNK)
        rows = pl.ds(r0, CHUNK)
        af = af_ref[rows, :]
        ls = jnp.minimum(af, 0.0) - jnp.log(1.0 + jnp.exp(-jnp.abs(af)))
        t2 = log1m_lb + ls
        mx = jnp.maximum(log_lb, t2)
        lf = mx + jnp.log(1.0 + jnp.exp(-jnp.abs(log_lb - t2)))
        k_scr[...] = 1.0 - jnp.exp(lf)
        hi, mid, lo = _split3(lf)
        b = (jnp.dot(tri, hi, preferred_element_type=F32) + jnp.dot(tri, mid, preferred_element_type=F32)
             + jnp.dot(tri, lo, preferred_element_type=F32))
        b_scr[...] = b
        q_scr[...] = aq_ref[rows, :].astype(F32) * (A_HEAD_DIM ** -0.5)
        v_scr[...] = ai_ref[rows, :].astype(F32)

        for i in range(n_sub):
            s0 = i * SUB
            b_blk = b_scr[pl.ds(s0, SUB), :]
            q_blk = q_scr[pl.ds(s0, SUB), :]
            for s in range(SUB):
                b_s = b_scr[pl.ds(s0 + s, 1), :]
                k_s = k_scr[pl.ds(s0 + s, 1), :]
                e = jnp.exp(jnp.where(sub_row >= s, b_blk - b_s, NEG_BIG))
                p_scr[pl.ds((s0 + s) * SUB, SUB), :] = (q_blk * k_s * e).astype(BF16)
        for h in range(A_HEADS):
            cols = slice(h * hd, (h + 1) * hd)
            r_scr[:, cols] = jnp.dot(p_scr[:, cols], ones_blk, preferred_element_type=F32)

        b_last = b_scr[pl.ds(CHUNK - 1, 1), :]
        outs = []
        for h in range(A_HEADS):
            cols = slice(h * hd, (h + 1) * hd)
            qh = q_scr[:, cols]
            kh = k_scr[:, cols]
            vh = v_scr[:, cols]
            bh = b_scr[:, cols]
            st = st_scr[h]
            qt = (qh * jnp.exp(bh)).astype(BF16)
            inter = lax.dot_general(qt, st.astype(BF16), (((1,), (1,)), ((), ())), preferred_element_type=F32)
            blocks = []
            for i in range(n_sub):
                s0 = i * SUB
                acc = inter[s0:s0 + SUB]
                for s in range(SUB):
                    acc = acc + r_scr[pl.ds((s0 + s) * SUB, SUB), cols] * v_scr[pl.ds(s0 + s, 1), cols]
                if i > 0:
                    beta = b_scr[pl.ds(s0 - 1, 1), cols]
                    qhat = (qh[s0:s0 + SUB] * jnp.exp(bh[s0:s0 + SUB] - beta)).astype(BF16)
                    khat = (kh[:s0] * jnp.exp(beta - bh[:s0])).astype(BF16)
                    a = lax.dot_general(qhat, khat, (((1,), (1,)), ((), ())), preferred_element_type=F32)
                    acc = acc + jnp.dot(a.astype(BF16), vh[:s0].astype(BF16), preferred_element_type=F32)
                blocks.append(acc)
            o = jnp.concatenate(blocks, axis=0)
            bl = b_last[:, cols]
            ktil = (kh * jnp.exp(bl - bh)).astype(BF16)
            upd = lax.dot_general(vh.astype(BF16), ktil, (((0,), (0,)), ((), ())), preferred_element_type=F32)
            st_scr[h] = st * jnp.exp(bl) + upd
            o = o * lax.rsqrt(jnp.mean(o * o, axis=-1, keepdims=True) + EPS)
            outs.append(o)
        o_all = jnp.concatenate(outs, axis=1)
        gate = ag_ref[rows, :].astype(F32)
        o_ref[rows, :] = (o_all * og * _silu(gate)).astype(BF16)
        return carry

    lax.fori_loop(0, n_chunks, chunk_body, 0)


def _hgrn(p, af, log_lb, log1m_lb, onorm_g, bsz, seq, aq_blk):
    t = p.shape[0]
    ts = min(512, seq)
    n_seq_tiles = seq // ts
    n_chunks = ts // CHUNK
    kern = functools.partial(_hgrn_kernel, n_chunks=n_chunks)
    row = lambda b, i: b * n_seq_tiles + i
    est = (2 * 3 * ts * A_DIM * 2 + 2 * ts * A_DIM * 4 + 2 * ts * A_DIM * 2 + A_HEADS * 128 * 128 * 4
           + 4 * CHUNK * A_DIM * 4 + CHUNK * SUB * A_DIM * 6)
    return pl.pallas_call(
        kern,
        grid=(bsz, n_seq_tiles),
        in_specs=[
            pl.BlockSpec((ts, A_DIM), lambda b, i: (row(b, i), aq_blk)),
            pl.BlockSpec((ts, A_DIM), lambda b, i: (row(b, i), 0)),
            pl.BlockSpec((ts, A_DIM), lambda b, i: (row(b, i), aq_blk + 1)),
            pl.BlockSpec((ts, A_DIM), lambda b, i: (row(b, i), aq_blk + 2)),
            pl.BlockSpec((1, A_DIM), lambda b, i: (0, 0)),
            pl.BlockSpec((1, A_DIM), lambda b, i: (0, 0)),
            pl.BlockSpec((1, A_DIM), lambda b, i: (0, 0)),
        ],
        out_specs=pl.BlockSpec((ts, A_DIM), lambda b, i: (row(b, i), 0)),
        out_shape=jax.ShapeDtypeStruct((t, A_DIM), BF16),
        scratch_shapes=[
            pltpu.VMEM((A_HEADS, A_HEAD_DIM, A_HEAD_DIM), F32),
            pltpu.VMEM((CHUNK, A_DIM), F32),
            pltpu.VMEM((CHUNK, A_DIM), F32),
            pltpu.VMEM((CHUNK, A_DIM), F32),
            pltpu.VMEM((CHUNK, A_DIM), F32),
            pltpu.VMEM((CHUNK * SUB, A_DIM), BF16),
            pltpu.VMEM((CHUNK * SUB, A_DIM), F32),
        ],
        compiler_params=_cparams(("arbitrary", "arbitrary"), est),
        name="hgrn2_scan",
    )(p, af, p, p, log_lb.reshape(1, A_DIM), log1m_lb.reshape(1, A_DIM), onorm_g.reshape(1, A_DIM))


def _attn_kernel(q_ref, k0_ref, k1_ref, k2_ref, v0_ref, v1_ref, v2_ref, bias_ref, o_ref):
    i = pl.program_id(1)
    tq = q_ref.shape[0]
    hd = C_HEAD_DIM
    scale = C_HEAD_DIM ** -0.5
    col = lax.broadcasted_iota(I32, (1, 3 * tq), 1)
    valid = (col >= 2 * tq) | ((col >= tq) & (i >= 1)) | (i >= 2)
    for h in range(C_HEADS):
        cols = slice(h * hd, (h + 1) * hd)
        qh = q_ref[:, cols]
        kh = jnp.concatenate([k0_ref[:, cols], k1_ref[:, cols], k2_ref[:, cols]], axis=0)
        vh = jnp.concatenate([v0_ref[:, cols], v1_ref[:, cols], v2_ref[:, cols]], axis=0)
        s = lax.dot_general(qh, kh, (((1,), (1,)), ((), ())), preferred_element_type=F32) * scale + bias_ref[h]
        s = jnp.where(valid, s, NEG_BIG)
        m = jnp.max(s, axis=-1, keepdims=True)
        e = jnp.exp(s - m)
        pr = e / jnp.sum(e, axis=-1, keepdims=True)
        o_ref[:, cols] = jnp.dot(pr.astype(BF16), vh, preferred_element_type=F32).astype(BF16)


def _attention_bias_table(rel_bias, tq):
    band = (LEFT_CHUNKS + 1) * CHUNK
    assert tq * 2 == LEFT_CHUNKS * CHUNK, "three key slots of tq rows must cover the band"
    diff = np.arange(CHUNK + band - 1) - (band - 1)
    ext = rel_bias.astype(F32)[:, np.clip(diff + LEFT_CHUNKS * CHUNK, -REL_CLIP, REL_CLIP) + REL_CLIP]
    rev = ext[:, ::-1]
    chunk_tab = jnp.stack([rev[:, CHUNK - 1 - tt:CHUNK - 1 - tt + band] for tt in range(CHUNK)], axis=1)
    rows = [jnp.pad(chunk_tab, ((0, 0), (0, 0), (c * CHUNK, 3 * tq - band - c * CHUNK)), constant_values=NEG_BIG)
            for c in range(tq // CHUNK)]
    return jnp.concatenate(rows, axis=1)


def _attention(p, rel_bias, bsz, seq, q_blk):
    t = p.shape[0]
    tq = ATT_TQ
    nq = seq // tq
    bias = _attention_bias_table(rel_bias, tq)
    row = lambda b, i, back: b * nq + jnp.maximum(i - back, 0)
    kv_spec = lambda blk, back: pl.BlockSpec((tq, C_DIM), lambda b, i: (row(b, i, back), blk))
    est = 2 * 7 * tq * C_DIM * 2 + 2 * C_HEADS * tq * 3 * tq * 4 + 2 * tq * C_DIM * 2 + 6 * tq * 3 * tq * 4
    return pl.pallas_call(
        _attn_kernel,
        grid=(bsz, nq),
        in_specs=[
            pl.BlockSpec((tq, C_DIM), lambda b, i: (b * nq + i, q_blk)),
            kv_spec(q_blk + 1, 2), kv_spec(q_blk + 1, 1), kv_spec(q_blk + 1, 0),
            kv_spec(q_blk + 2, 2), kv_spec(q_blk + 2, 1), kv_spec(q_blk + 2, 0),
            pl.BlockSpec((C_HEADS, tq, 3 * tq), lambda b, i: (0, 0, 0)),
        ],
        out_specs=pl.BlockSpec((tq, C_DIM), lambda b, i: (b * nq + i, 0)),
        out_shape=jax.ShapeDtypeStruct((t, C_DIM), BF16),
        compiler_params=_cparams(("arbitrary", "arbitrary"), est),
        name="chunk_attention",
    )(p, p, p, p, p, p, p, bias)


def _merge_kernel(x_ref, gmod_ref, a_ref, c_ref, bb_ref, bc_ref, bu_ref, hbc_ref, hbu_ref, cw_ref,
                  ga_ref, gb_ref, gc_ref, wa_ref, wb_ref, wc_ref, wo_ref, o_ref, *, tiles_per_batch):
    i = pl.program_id(0)
    tm = x_ref.shape[0]
    u = bc_ref[...].astype(F32) * bu_ref[...].astype(F32)
    halo = hbc_ref[...].astype(F32) * hbu_ref[...].astype(F32)
    halo = jnp.where(i % tiles_per_batch == 0, 0.0, halo)
    hrows = halo.shape[0]
    hm1 = halo[hrows - 1:hrows]
    hm2 = halo[hrows - 2:hrows - 1]
    r = lax.broadcasted_iota(I32, (tm, 1), 0)
    u1 = jnp.where(r == 0, hm1, pltpu.roll(u, 1, 0))
    u2 = jnp.where(r == 0, hm2, jnp.where(r == 1, hm1, pltpu.roll(u, 2, 0)))
    cw = cw_ref[...]
    conv = u2 * cw[0:1] + u1 * cw[1:2] + u * cw[2:3]
    bfeat = (bb_ref[...].astype(F32) * conv).astype(BF16)
    ya = jnp.dot(a_ref[...], wa_ref[...], preferred_element_type=F32)
    yb = jnp.dot(bfeat, wb_ref[...], preferred_element_type=F32)
    yc = jnp.dot(c_ref[...], wc_ref[...], preferred_element_type=F32)
    merged = (ga_ref[...].astype(F32) * ya + gb_ref[...].astype(F32) * yb + gc_ref[...].astype(F32) * yc)
    y = jnp.dot(merged.astype(BF16), wo_ref[...], preferred_element_type=F32)
    o_ref[...] = x_ref[...] + gmod_ref[0] * y


def _merge(x2, seq, gmod, a_feat, c_feat, p, pg, conv_w, wa, wb, wc, wo, b_blk):
    t, d = x2.shape
    tm = min(256, seq)
    tiles_per_batch = seq // tm
    halo = 16
    hb = tm // halo
    kern = functools.partial(_merge_kernel, tiles_per_batch=tiles_per_batch)
    const = lambda shape: pl.BlockSpec(shape, lambda i: (0,) * len(shape))
    halo_spec = lambda blk: pl.BlockSpec((halo, B_WIDTH), lambda i: (jnp.maximum(i * hb - 1, 0), blk))
    est = (4 * tm * d * 4 + 4 * tm * A_DIM * 2 + 6 * tm * B_WIDTH * 2 + 6 * tm * d * 2
           + 2 * (A_DIM + B_WIDTH + C_DIM + d) * d * 2 + 6 * tm * d * 4)
    return pl.pallas_call(
        kern,
        grid=(t // tm,),
        in_specs=[
            pl.BlockSpec((tm, d), lambda i: (i, 0)),
            pl.BlockSpec((1, 1, d), lambda i: (i // tiles_per_batch, 0, 0)),
            pl.BlockSpec((tm, A_DIM), lambda i: (i, 0)),
            pl.BlockSpec((tm, C_DIM), lambda i: (i, 0)),
            pl.BlockSpec((tm, B_WIDTH), lambda i: (i, b_blk)),
            pl.BlockSpec((tm, B_WIDTH), lambda i: (i, b_blk + 1)),
            pl.BlockSpec((tm, B_WIDTH), lambda i: (i, b_blk + 2)),
            halo_spec(b_blk + 1),
            halo_spec(b_blk + 2),
            const((CONV_W, B_WIDTH)),
            pl.BlockSpec((tm, d), lambda i: (i, 0)),
            pl.BlockSpec((tm, d), lambda i: (i, 1)),
            pl.BlockSpec((tm, d), lambda i: (i, 2)),
            const((A_DIM, d)), const((B_WIDTH, d)), const((C_DIM, d)), const((d, d)),
        ],
        out_specs=pl.BlockSpec((tm, d), lambda i: (i, 0)),
        out_shape=jax.ShapeDtypeStruct((t, d), F32),
        compiler_params=_cparams(("arbitrary",), est),
        name="merge_outproj",
    )(x2, gmod, a_feat, c_feat, p, p, p, p, p, conv_w, pg, pg, pg, wa, wb, wc, wo)


def _first_max(vals, iota, n, axis):
    m = jnp.max(vals, axis=axis, keepdims=True)
    idx = jnp.min(jnp.where(vals == m, iota, n), axis=axis, keepdims=True)
    return m, idx


def _router_kernel(x_ref, g_ref, sc_ref, sh_ref, rwt_ref, rb_ref, h_ref, hp_ref, e_ref, w_ref, rank_ref, cnt_ref,
                   cnt_scr):
    step = pl.program_id(0)

    @pl.when(step == 0)
    def _():
        cnt_scr[...] = jnp.zeros_like(cnt_scr)

    tm = x_ref.shape[0]
    x = x_ref[...]
    y = x * lax.rsqrt(jnp.mean(x * x, axis=-1, keepdims=True) + EPS)
    h = y * g_ref[...] * (1.0 + sc_ref[0]) + sh_ref[0]
    h_ref[...] = h.astype(BF16)
    _store_token_tiles(hp_ref, 0, _pack_bf16_pairs(h))

    nt = (((1,), (1,)), ((), ()))
    h_hi, h_mid, _ = _split3(h)
    w_hi, w_mid, _ = _split3(rwt_ref[...])
    logits = (lax.dot_general(w_hi, h_hi, nt, preferred_element_type=F32)
              + lax.dot_general(w_hi, h_mid, nt, preferred_element_type=F32)
              + lax.dot_general(w_mid, h_hi, nt, preferred_element_type=F32))
    scores = _sigmoid(logits)
    sel = scores + rb_ref[:, 0:1]

    g3 = sel.reshape(N_GROUPS, GROUP_SIZE, tm)
    j3 = lax.broadcasted_iota(I32, g3.shape, 1)
    m1, i1 = _first_max(g3, j3, GROUP_SIZE, 1)
    m2 = jnp.max(jnp.where(j3 == i1, -jnp.inf, g3), axis=1, keepdims=True)
    gs = (m1 + m2).reshape(N_GROUPS, tm)
    gi = lax.broadcasted_iota(I32, gs.shape, 0)
    gmask = jnp.zeros(gs.shape, jnp.bool_)
    for _ in range(TOPK_GROUPS):
        _, idx = _first_max(gs, gi, N_GROUPS, 0)
        hit = gi == idx
        gmask = gmask | hit
        gs = jnp.where(hit, -jnp.inf, gs)
    emask = jnp.broadcast_to(gmask.reshape(N_GROUPS, 1, tm), (N_GROUPS, GROUP_SIZE, tm)).reshape(N_EXPERTS, tm)
    masked = jnp.where(emask, sel, -jnp.inf)

    ei = lax.broadcasted_iota(I32, (N_EXPERTS, tm), 0)
    chosen = jnp.zeros((N_EXPERTS, tm), F32)
    e_rows, w_rows = [], []
    for _ in range(TOP_K):
        _, idx = _first_max(masked, ei, N_EXPERTS, 0)
        hit = ei == idx
        e_rows.append(idx)
        w_rows.append(jnp.sum(jnp.where(hit, scores, 0.0), axis=0, keepdims=True))
        chosen = jnp.where(hit, 1.0, chosen)
        masked = jnp.where(hit, -jnp.inf, masked)
    eidx = jnp.concatenate(e_rows, axis=0)
    wts = jnp.concatenate(w_rows, axis=0)
    wts = wts / jnp.sum(wts, axis=0, keepdims=True) * ROUTED_SCALE
    e_ref[...] = eidx
    w_ref[...] = wts

    tr = lax.broadcasted_iota(I32, (tm, tm), 0)
    tc = lax.broadcasted_iota(I32, (tm, tm), 1)
    before = (tr < tc).astype(BF16)
    base = cnt_scr[:, 0:1]
    rank_all = jnp.dot(chosen.astype(BF16), before, preferred_element_type=F32) + base
    ranks = [jnp.sum(jnp.where(ei == e_rows[k], rank_all, 0.0), axis=0, keepdims=True) for k in range(TOP_K)]
    rank_ref[...] = jnp.concatenate(ranks, axis=0).astype(I32)
    total = base + jnp.sum(chosen, axis=1, keepdims=True)
    cnt_scr[...] = jnp.broadcast_to(total, cnt_scr.shape)
    cnt_ref[...] = jnp.broadcast_to(total, cnt_ref.shape).astype(I32)


def _router(x2, seq, g, sc, sh, router_wt, router_bias):
    t, d = x2.shape
    tm = min(512, seq)
    tiles_per_batch = seq // tm
    est = (2 * tm * d * 4 + 2 * tm * d * 2 + 2 * tm * d * 2 + 2 * N_EXPERTS * d * 4 + tm * tm * 2
           + 8 * tm * d * 4 + 16 * N_EXPERTS * tm * 4)
    return pl.pallas_call(
        _router_kernel,
        grid=(t // tm,),
        in_specs=[
            pl.BlockSpec((tm, d), lambda i: (i, 0)),
            pl.BlockSpec((1, d), lambda i: (0, 0)),
            pl.BlockSpec((1, 1, d), lambda i: (i // tiles_per_batch, 0, 0)),
            pl.BlockSpec((1, 1, d), lambda i: (i // tiles_per_batch, 0, 0)),
            pl.BlockSpec((N_EXPERTS, d), lambda i: (0, 0)),
            pl.BlockSpec((N_EXPERTS, LANES), lambda i: (0, 0)),
        ],
        out_specs=[
            pl.BlockSpec((tm, d), lambda i: (i, 0)),
            pl.BlockSpec((tm * d // (2 * LANES), LANES), lambda i: (i, 0)),
            pl.BlockSpec((TOP_K, tm), lambda i: (0, i)),
            pl.BlockSpec((TOP_K, tm), lambda i: (0, i)),
            pl.BlockSpec((TOP_K, tm), lambda i: (0, i)),
            pl.BlockSpec((N_EXPERTS, LANES), lambda i: (0, 0)),
        ],
        out_shape=[
            jax.ShapeDtypeStruct((t, d), BF16),
            jax.ShapeDtypeStruct((t * d // (2 * LANES), LANES), U32),
            jax.ShapeDtypeStruct((TOP_K, t), I32),
            jax.ShapeDtypeStruct((TOP_K, t), F32),
            jax.ShapeDtypeStruct((TOP_K, t), I32),
            jax.ShapeDtypeStruct((N_EXPERTS, LANES), I32),
        ],
        scratch_shapes=[pltpu.VMEM((N_EXPERTS, LANES), F32)],
        compiler_params=_cparams(("arbitrary",), est),
        name="norm_router",
    )(x2, g.reshape(1, d), sc, sh, router_wt, jnp.broadcast_to(router_bias.reshape(N_EXPERTS, 1), (N_EXPERTS, LANES)))


def _pos_kernel(start_ref, e_ref, rank_ref, pos_ref):
    e = e_ref[...]
    acc = rank_ref[...]
    for ex in range(N_EXPERTS):
        acc = acc + jnp.where(e == ex, start_ref[ex], 0)
    pos_ref[...] = acc


def _positions(pstart, eidx, rank):
    k, t = eidx.shape
    tl = min(4096, t)
    return pl.pallas_call(
        _pos_kernel,
        grid_spec=pltpu.PrefetchScalarGridSpec(
            num_scalar_prefetch=1,
            grid=(t // tl,),
            in_specs=[pl.BlockSpec((k, tl), lambda i, s: (0, i)), pl.BlockSpec((k, tl), lambda i, s: (0, i))],
            out_specs=pl.BlockSpec((k, tl), lambda i, s: (0, i)),
        ),
        out_shape=jax.ShapeDtypeStruct((k, t), I32),
        compiler_params=_cparams(("arbitrary",), 8 * k * tl * 4),
        name="moe_positions",
    )(pstart, eidx, rank)


def _dispatch_kernel(pos_ref, hp_ref, xs_ref, sem, *, tm, per):
    def copy(t, k):
        src = hp_ref.at[pl.ds(pl.multiple_of(t * per, per), per)]
        dst = xs_ref.at[pl.ds(pl.multiple_of(pos_ref[k, t] * per, per), per)]
        return pltpu.make_async_copy(src, dst, sem)

    def start(t, c):
        for k in range(TOP_K):
            copy(t, k).start()
        return c

    def wait(t, c):
        for k in range(TOP_K):
            copy(t, k).wait()
        return c

    lax.fori_loop(0, tm, start, 0, unroll=2)
    lax.fori_loop(0, tm, wait, 0, unroll=2)


def _dispatch(pos, hp, n_rows, per):
    t = hp.shape[0] // per
    tm = min(512, t)
    kern = functools.partial(_dispatch_kernel, tm=tm, per=per)
    return pl.pallas_call(
        kern,
        grid=(t // tm,),
        in_specs=[
            pl.BlockSpec((TOP_K, tm), lambda i: (0, i), memory_space=pltpu.SMEM),
            pl.BlockSpec((tm * per, LANES), lambda i: (i, 0)),
        ],
        out_specs=pl.BlockSpec(memory_space=pl.ANY),
        out_shape=jax.ShapeDtypeStruct((n_rows * per, LANES), U32),
        scratch_shapes=[pltpu.SemaphoreType.DMA(())],
        compiler_params=pltpu.CompilerParams(dimension_semantics=("arbitrary",), has_side_effects=True,
                                             vmem_limit_bytes=_vmem_limit(4 * tm * per * LANES * 4)),
        name="moe_dispatch",
    )(pos, hp)


def _expert_kernel(blk_exp_ref, n_used_ref, xs_ref, wg_ref, wu_ref, wd_ref, ys_ref, wg_scr, wu_scr, wd_scr, *, per):
    b = pl.program_id(0)
    new_expert = (b == 0) | (blk_exp_ref[b] != blk_exp_ref[jnp.maximum(b - 1, 0)])

    @pl.when(new_expert)
    def _():
        wg_scr[...] = wg_ref[0].astype(BF16)
        wu_scr[...] = wu_ref[0].astype(BF16)
        wd_scr[...] = wd_ref[0].astype(BF16)

    @pl.when(b < n_used_ref[0])
    def _():
        lo, hi = _unpack_bf16_pairs(_load_token_tiles(xs_ref, 0, MOE_BLK, per))
        half = lo.shape[1]
        gate = (jnp.dot(lo, wg_scr[:half], preferred_element_type=F32)
                + jnp.dot(hi, wg_scr[half:], preferred_element_type=F32))
        up = (jnp.dot(lo, wu_scr[:half], preferred_element_type=F32)
              + jnp.dot(hi, wu_scr[half:], preferred_element_type=F32))
        hidden = (_silu(gate) * up).astype(BF16)
        _store_token_tiles(ys_ref, 0, _pack_bf16_pairs(jnp.dot(hidden, wd_scr[...], preferred_element_type=F32)))

    @pl.when(b >= n_used_ref[0])
    def _():
        ys_ref[...] = jnp.zeros_like(ys_ref)


def _experts(blk_exp, n_used, xs, wg, wu, wd, per):
    half = per * LANES
    d = 2 * half
    de = wg.shape[2]
    nb = xs.shape[0] // (MOE_BLK * per)
    est = 4 * MOE_BLK * half * 4 + 2 * 3 * d * de * 4 + 3 * d * de * 2 + MOE_BLK * (2 * de + d) * 4
    return pl.pallas_call(
        functools.partial(_expert_kernel, per=per),
        grid_spec=pltpu.PrefetchScalarGridSpec(
            num_scalar_prefetch=2,
            grid=(nb,),
            in_specs=[
                pl.BlockSpec((MOE_BLK * per, LANES), lambda b, be, nu: (b, 0)),
                pl.BlockSpec((1, d, de), lambda b, be, nu: (be[b], 0, 0)),
                pl.BlockSpec((1, d, de), lambda b, be, nu: (be[b], 0, 0)),
                pl.BlockSpec((1, de, d), lambda b, be, nu: (be[b], 0, 0)),
            ],
            out_specs=pl.BlockSpec((MOE_BLK * per, LANES), lambda b, be, nu: (b, 0)),
            scratch_shapes=[pltpu.VMEM((d, de), BF16), pltpu.VMEM((d, de), BF16), pltpu.VMEM((de, d), BF16)],
        ),
        out_shape=jax.ShapeDtypeStruct(xs.shape, U32),
        compiler_params=_cparams(("arbitrary",), est),
        name="moe_experts",
    )(blk_exp, n_used, xs, wg, wu, wd)


def _combine_kernel(pos_ref, x_ref, gmod_ref, h_ref, wt_ref, sg_ref, su_ref, sd_ref, ys_ref, o_ref, yg_scr, sem,
                    *, tm, per):
    def copy(t, k):
        src = ys_ref.at[pl.ds(pl.multiple_of(pos_ref[k, t] * per, per), per)]
        dst = yg_scr.at[pl.ds(pl.multiple_of((k * tm + t) * per, per), per)]
        return pltpu.make_async_copy(src, dst, sem)

    def start(t, c):
        for k in range(TOP_K):
            copy(t, k).start()
        return c

    def wait(t, c):
        for k in range(TOP_K):
            copy(t, k).wait()
        return c

    lax.fori_loop(0, tm, start, 0, unroll=2)
    h = h_ref[...]
    hidden = (_silu(jnp.dot(h, sg_ref[...], preferred_element_type=F32))
              * jnp.dot(h, su_ref[...], preferred_element_type=F32)).astype(BF16)
    shared = jnp.dot(hidden, sd_ref[...], preferred_element_type=F32)
    lax.fori_loop(0, tm, wait, 0, unroll=2)
    half = per * LANES
    acc_lo = shared[:, :half]
    acc_hi = shared[:, half:]
    wt = wt_ref[...]
    for k in range(TOP_K):
        lo, hi = _unpack_bf16_pairs(_load_token_tiles(yg_scr, k * tm, tm, per))
        wk = wt[:, k:k + 1]
        acc_lo = acc_lo + wk * lo.astype(F32)
        acc_hi = acc_hi + wk * hi.astype(F32)
    g = gmod_ref[0]
    o_ref[:, :half] = x_ref[:, :half] + g[:, :half] * acc_lo
    o_ref[:, half:] = x_ref[:, half:] + g[:, half:] * acc_hi


def _combine(pos, x2, seq, gmod, h, wts_t, sg, su, sd, ys, per):
    t, d = x2.shape
    half = d // 2
    ds_ = sg.shape[1]
    tm = min(256, seq)
    tiles_per_batch = seq // tm
    kern = functools.partial(_combine_kernel, tm=tm, per=per)
    const = lambda shape: pl.BlockSpec(shape, lambda i: (0,) * len(shape))
    est = 4 * tm * d * 4 + 2 * tm * d * 2 + 2 * 3 * d * ds_ * 2 + TOP_K * tm * half * 4 + 8 * tm * d * 4
    return pl.pallas_call(
        kern,
        grid=(t // tm,),
        in_specs=[
            pl.BlockSpec((TOP_K, tm), lambda i: (0, i), memory_space=pltpu.SMEM),
            pl.BlockSpec((tm, d), lambda i: (i, 0)),
            pl.BlockSpec((1, 1, d), lambda i: (i // tiles_per_batch, 0, 0)),
            pl.BlockSpec((tm, d), lambda i: (i, 0)),
            pl.BlockSpec((tm, TOP_K), lambda i: (i, 0)),
            const((d, ds_)), const((d, ds_)), const((ds_, d)),
            pl.BlockSpec(memory_space=pl.ANY),
        ],
        out_specs=pl.BlockSpec((tm, d), lambda i: (i, 0)),
        out_shape=jax.ShapeDtypeStruct((t, d), F32),
        scratch_shapes=[pltpu.VMEM((TOP_K * tm * per, LANES), U32), pltpu.SemaphoreType.DMA(())],
        compiler_params=_cparams(("arbitrary",), est),
        name="moe_combine",
    )(pos, x2, gmod, h, wts_t, sg, su, sd, ys)


def _final_kernel(x_ref, g_ref, o_ref):
    x = x_ref[...]
    o_ref[...] = x * lax.rsqrt(jnp.mean(x * x, axis=-1, keepdims=True) + EPS) * g_ref[...]


def _final_norm(x2, g):
    t, d = x2.shape
    tm = min(1024, t)
    return pl.pallas_call(
        _final_kernel,
        grid=(t // tm,),
        in_specs=[pl.BlockSpec((tm, d), lambda i: (i, 0)), pl.BlockSpec((1, d), lambda i: (0, 0))],
        out_specs=pl.BlockSpec((tm, d), lambda i: (i, 0)),
        out_shape=jax.ShapeDtypeStruct((t, d), F32),
        compiler_params=_cparams(("arbitrary",), 4 * tm * d * 4),
        name="final_norm",
    )(x2, g.reshape(1, d))


def _moe_layout(counts, n_blocks):
    padded = (counts + MOE_BLK - 1) // MOE_BLK * MOE_BLK
    pends = jnp.cumsum(padded)
    pstart = (pends - padded).astype(I32)
    blk_row = jnp.arange(n_blocks, dtype=I32)[:, None] * MOE_BLK
    blk_exp = jnp.minimum(jnp.sum((pends[None, :] <= blk_row).astype(I32), axis=1), N_EXPERTS - 1)
    n_used = (pends[-1:] // MOE_BLK).astype(I32)
    return pstart, blk_exp, n_used


def _layer(x2, bsz, seq, mod, norm1_g, w_in, lb, onorm_g, conv_w, rel_bias, w_a, w_b, w_c, w_out, norm2_g,
           router_w, router_bias, wg, wu, wd, sg, su, sd):
    t, d = x2.shape
    sh1, sc1, g1, sh2, sc2, g2 = [m.reshape(bsz, 1, d) for m in jnp.split(mod, N_MOD, axis=-1)]
    n_mix_blocks = (w_in.shape[1] - 3 * d) // A_DIM
    h1 = _norm_mod(x2, seq, norm1_g, sc1, sh1)
    pg = _project(h1, w_in, tuple(range(n_mix_blocks, w_in.shape[1] // A_DIM)), "sigmoid", BF16, "proj_gates")
    af = _project(h1, w_in, (1,), "none", F32, "proj_forget")
    p = _project(h1, w_in, (4, 5, 0, 2, 3, 6, 7, 8), "none", BF16, "proj_mixers")
    b_blk = 0
    aq_blk = (3 * B_WIDTH) // A_DIM
    q_blk = aq_blk + 3
    a_feat = _hgrn(p, af, jnp.log(lb), jnp.log1p(-lb), onorm_g, bsz, seq, aq_blk)
    c_feat = _attention(p, rel_bias, bsz, seq, q_blk)
    x2 = _merge(x2, seq, g1, a_feat, c_feat, p, pg, conv_w, w_a.astype(BF16), w_b.astype(BF16), w_c.astype(BF16),
                w_out.astype(BF16), b_blk)

    h, hp, eidx, wts, rank, cnt = _router(x2, seq, norm2_g, sc2, sh2, router_w.T, router_bias)
    n_blocks = (t * TOP_K) // MOE_BLK + N_EXPERTS
    pstart, blk_exp, n_used = _moe_layout(cnt[:, 0], n_blocks)
    pos = _positions(pstart, eidx, rank)
    per = d // (2 * LANES)
    xs = _dispatch(pos, hp, n_blocks * MOE_BLK, per)
    ys = _experts(blk_exp, n_used, xs, wg, wu, wd, per)
    return _combine(pos, x2, seq, g2, h, wts.T, sg.astype(BF16), su.astype(BF16), sd.astype(BF16), ys, per)


def kernel(x, c, ada_w, ada_b, norm1_g, w_in, hgrn_lb_logits, hgrn_onorm_g, conv_w, rel_bias, w_branch_a, w_branch_b, w_branch_c, w_out, norm2_g, router_w, router_bias, moe_w_gate, moe_w_up, moe_w_down, shared_w_gate, shared_w_up, shared_w_down, final_g):
    bsz, seq, d = x.shape
    depth = ada_w.shape[0]
    mod = _modulation(c, ada_w, ada_b)
    lbs = jnp.cumsum(jax.nn.softmax(hgrn_lb_logits.astype(F32), axis=0), axis=0)
    lbs = lbs - lbs[0]
    x2 = x.reshape(bsz * seq, d)
    for l in range(depth):
        x2 = _layer(x2, bsz, seq, mod[l], norm1_g[l], w_in[l], lbs[l], hgrn_onorm_g[l], conv_w[l], rel_bias[l],
                    w_branch_a[l], w_branch_b[l], w_branch_c[l], w_out[l], norm2_g[l], router_w[l], router_bias[l],
                    moe_w_gate[l], moe_w_up[l], moe_w_down[l], shared_w_gate[l], shared_w_up[l], shared_w_down[l])
    return _final_norm(x2, final_g).reshape(bsz, seq, d)
```

```python
import functools

import numpy as np
import jax
import jax.numpy as jnp
from jax import lax
from jax.experimental import pallas as pl
from jax.experimental.pallas import tpu as pltpu

F32 = jnp.float32
BF16 = jnp.bfloat16
I32 = jnp.int32
U32 = jnp.uint32

CHUNK = 64
EPS = 1e-6
A_HEADS = 6
A_HEAD_DIM = 128
A_DIM = A_HEADS * A_HEAD_DIM
B_WIDTH = 512
CONV_W = 3
C_HEADS = 6
C_HEAD_DIM = 128
C_DIM = C_HEADS * C_HEAD_DIM
LEFT_CHUNKS = 8
REL_CLIP = 256
N_EXPERTS = 64
TOP_K = 8
N_GROUPS = 8
GROUP_SIZE = N_EXPERTS // N_GROUPS
TOPK_GROUPS = 4
ROUTED_SCALE = 2.5
N_MOD = 6

V7X_VMEM_BYTES = 64 * 1024 * 1024
LANES = 128
NEG_BIG = -1e30

SUB = 16
MOE_BLK = 512
ATT_TQ = 256


def _vmem_limit(nbytes):
    return int(min(max(nbytes * 3 // 2, 16 * 1024 * 1024), V7X_VMEM_BYTES - 8 * 1024 * 1024))


def _cparams(sem, nbytes):
    return pltpu.CompilerParams(dimension_semantics=sem, vmem_limit_bytes=_vmem_limit(nbytes))


def _sigmoid(x):
    return 1.0 / (1.0 + jnp.exp(-x))


def _silu(x):
    return x * _sigmoid(x)


def _split3(x):
    hi = x.astype(BF16)
    r1 = x - hi.astype(F32)
    mid = r1.astype(BF16)
    lo = (r1 - mid.astype(F32)).astype(BF16)
    return hi, mid, lo


def _pack_bf16_pairs(x):
    n = x.shape[1] // 2
    xb = x.astype(BF16).astype(F32)
    lo = lax.bitcast_convert_type(xb[:, :n], U32) >> 16
    hi = lax.bitcast_convert_type(xb[:, n:], U32) & jnp.uint32(0xFFFF0000)
    return lo | hi


def _unpack_bf16_pairs(p):
    lo = lax.bitcast_convert_type(p << 16, F32).astype(BF16)
    hi = lax.bitcast_convert_type(p & jnp.uint32(0xFFFF0000), F32).astype(BF16)
    return lo, hi


def _store_token_tiles(ref, first_token, mat):
    n, w = mat.shape
    per = w // LANES
    for s in range(per):
        ref[pl.ds(first_token * per + s, n, stride=per), :] = mat[:, s * LANES:(s + 1) * LANES]


def _load_token_tiles(ref, first_token, n, per):
    return jnp.concatenate([ref[pl.ds(first_token * per + s, n, stride=per), :] for s in range(per)], axis=1)


def _mod_kernel(c_ref, w_ref, b_ref, o_ref):
    c = c_ref[...]
    o_ref[0] = jnp.dot(_silu(c), w_ref[0], preferred_element_type=F32) + b_ref[0]


def _modulation(c, ada_w, ada_b):
    depth, d, n = ada_w.shape
    bsz = c.shape[0]
    rows = 8
    c_pad = jnp.zeros((rows, d), F32).at[:bsz].set(c)
    tn = 1536 if n % 1536 == 0 else n
    out = pl.pallas_call(
        _mod_kernel,
        grid=(depth, n // tn),
        in_specs=[
            pl.BlockSpec((rows, d), lambda l, j: (0, 0)),
            pl.BlockSpec((1, d, tn), lambda l, j: (l, 0, j)),
            pl.BlockSpec((1, 1, tn), lambda l, j: (l, 0, j)),
        ],
        out_specs=pl.BlockSpec((1, rows, tn), lambda l, j: (l, 0, j)),
        out_shape=jax.ShapeDtypeStruct((depth, rows, n), F32),
        compiler_params=_cparams(("arbitrary", "arbitrary"), 2 * d * tn * 4 + 4 * rows * tn * 4),
        name="adaln_mod",
    )(c_pad, ada_w, ada_b.reshape(depth, 1, n))
    return out[:, :bsz]


def _norm_kernel(x_ref, g_ref, sc_ref, sh_ref, h_ref):
    x = x_ref[...]
    y = x * lax.rsqrt(jnp.mean(x * x, axis=-1, keepdims=True) + EPS)
    h_ref[...] = (y * g_ref[...] * (1.0 + sc_ref[0]) + sh_ref[0]).astype(BF16)


def _norm_mod(x2, seq, g, sc, sh):
    t, d = x2.shape
    tm = min(512, seq)
    tiles_per_batch = seq // tm
    return pl.pallas_call(
        _norm_kernel,
        grid=(t // tm,),
        in_specs=[
            pl.BlockSpec((tm, d), lambda i: (i, 0)),
            pl.BlockSpec((1, d), lambda i: (0, 0)),
            pl.BlockSpec((1, 1, d), lambda i: (i // tiles_per_batch, 0, 0)),
            pl.BlockSpec((1, 1, d), lambda i: (i // tiles_per_batch, 0, 0)),
        ],
        out_specs=pl.BlockSpec((tm, d), lambda i: (i, 0)),
        out_shape=jax.ShapeDtypeStruct((t, d), BF16),
        compiler_params=_cparams(("arbitrary",), 6 * tm * d * 4),
        name="norm_mod",
    )(x2, g.reshape(1, d), sc, sh)


def _proj_kernel(cmap_ref, h_ref, w_ref, o_ref, w_scr, *, act):
    @pl.when(pl.program_id(1) == 0)
    def _():
        w_scr[...] = w_ref[0].astype(BF16)

    acc = jnp.dot(h_ref[...], w_scr[...], preferred_element_type=F32)
    if act == "sigmoid":
        acc = _sigmoid(acc)
    o_ref[...] = acc.astype(o_ref.dtype)


def _project(h, w_all, layer, col_blocks, act, out_dtype, name):
    t, d = h.shape
    tn = A_DIM
    nj = len(col_blocks)
    tm = min(1024, t)
    osz = jnp.dtype(out_dtype).itemsize
    est = 2 * tm * d * 2 + 2 * d * tn * 4 + d * tn * 2 + 2 * tm * tn * osz + 2 * tm * tn * 4
    return pl.pallas_call(
        functools.partial(_proj_kernel, act=act),
        grid_spec=pltpu.PrefetchScalarGridSpec(
            num_scalar_prefetch=1,
            grid=(nj, t // tm),
            in_specs=[
                pl.BlockSpec((tm, d), lambda j, i, cm: (i, 0)),
                pl.BlockSpec((1, d, tn), lambda j, i, cm: (layer, 0, cm[j])),
            ],
            out_specs=pl.BlockSpec((tm, tn), lambda j, i, cm: (i, j)),
            scratch_shapes=[pltpu.VMEM((d, tn), BF16)],
        ),
        out_shape=jax.ShapeDtypeStruct((t, nj * tn), out_dtype),
        compiler_params=_cparams(("arbitrary", "arbitrary"), est),
        name=name,
    )(jnp.asarray(col_blocks, I32), h, w_all)


def _hgrn_kernel(aq_ref, af_ref, ai_ref, ag_ref, loglb_ref, log1mlb_ref, og_ref, o_ref,
                 st_scr, q_scr, k_scr, v_scr, b_scr, p_scr, r_scr, *, n_chunks):
    @pl.when(pl.program_id(1) == 0)
    def _():
        st_scr[...] = jnp.zeros_like(st_scr)

    hd = A_HEAD_DIM
    n_sub = CHUNK // SUB
    row_c = lax.broadcasted_iota(I32, (CHUNK, CHUNK), 0)
    col_c = lax.broadcasted_iota(I32, (CHUNK, CHUNK), 1)
    tri = (col_c <= row_c).astype(BF16)
    ones_blk = jnp.ones((hd, hd), BF16)
    sub_row = lax.broadcasted_iota(I32, (SUB, 1), 0)
    log_lb = loglb_ref[...]
    log1m_lb = log1mlb_ref[...]
    og = og_ref[...]

    def chunk_body(ci, carry):
        r0 = pl.multiple_of(ci * CHUNK, CHUNK)
        rows = pl.ds(r0, CHUNK)
        af = af_ref[rows, :]
        ls = jnp.minimum(af, 0.0) - jnp.log(1.0 + jnp.exp(-jnp.abs(af)))
        t2 = log1m_lb + ls
        mx = jnp.maximum(log_lb, t2)
        lf = mx + jnp.log(1.0 + jnp.exp(-jnp.abs(log_lb - t2)))
        k_scr[...] = 1.0 - jnp.exp(lf)
        hi, mid, lo = _split3(lf)
        b = (jnp.dot(tri, hi, preferred_element_type=F32) + jnp.dot(tri, mid, preferred_element_type=F32)
             + jnp.dot(tri, lo, preferred_element_type=F32))
        b_scr[...] = b
        q_scr[...] = aq_ref[rows, :].astype(F32) * (A_HEAD_DIM ** -0.5)
        v_scr[...] = ai_ref[rows, :].astype(F32)

        for i in range(n_sub):
            s0 = i * SUB
            b_blk = b_scr[pl.ds(s0, SUB), :]
            q_blk = q_scr[pl.ds(s0, SUB), :]
            for s in range(SUB):
                b_s = b_scr[pl.ds(s0 + s, 1), :]
                k_s = k_scr[pl.ds(s0 + s, 1), :]
                e = jnp.exp(jnp.where(sub_row >= s, b_blk - b_s, NEG_BIG))
                p_scr[pl.ds((s0 + s) * SUB, SUB), :] = (q_blk * k_s * e).astype(BF16)
        for h in range(A_HEADS):
            cols = slice(h * hd, (h + 1) * hd)
            r_scr[:, cols] = jnp.dot(p_scr[:, cols], ones_blk, preferred_element_type=F32)

        b_last = b_scr[pl.ds(CHUNK - 1, 1), :]
        outs = []
        for h in range(A_HEADS):
            cols = slice(h * hd, (h + 1) * hd)
            qh = q_scr[:, cols]
            kh = k_scr[:, cols]
            vh = v_scr[:, cols]
            bh = b_scr[:, cols]
            st = st_scr[h]
            qt = (qh * jnp.exp(bh)).astype(BF16)
            inter = lax.dot_general(qt, st.astype(BF16), (((1,), (1,)), ((), ())), preferred_element_type=F32)
            blocks = []
            for i in range(n_sub):
                s0 = i * SUB
                acc = inter[s0:s0 + SUB]
                for s in range(SUB):
                    acc = acc + r_scr[pl.ds((s0 + s) * SUB, SUB), cols] * v_scr[pl.ds(s0 + s, 1), cols]
                if i > 0:
                    beta = b_scr[pl.ds(s0 - 1, 1), cols]
                    qhat = (qh[s0:s0 + SUB] * jnp.exp(bh[s0:s0 + SUB] - beta)).astype(BF16)
                    khat = (kh[:s0] * jnp.exp(beta - bh[:s0])).astype(BF16)
                    a = lax.dot_general(qhat, khat, (((1,), (1,)), ((), ())), preferred_element_type=F32)
                    acc = acc + jnp.dot(a.astype(BF16), vh[:s0].astype(BF16), preferred_element_type=F32)
                blocks.append(acc)
            o = jnp.concatenate(blocks, axis=0)
            bl = b_last[:, cols]
            ktil = (kh * jnp.exp(bl - bh)).astype(BF16)
            upd = lax.dot_general(vh.astype(BF16), ktil, (((0,), (0,)), ((), ())), preferred_element_type=F32)
            st_scr[h] = st * jnp.exp(bl) + upd
            o = o * lax.rsqrt(jnp.mean(o * o, axis=-1, keepdims=True) + EPS)
            outs.append(o)
        o_all = jnp.concatenate(outs, axis=1)
        gate = ag_ref[rows, :].astype(F32)
        o_ref[rows, :] = (o_all * og * _silu(gate)).astype(BF16)
        return carry

    lax.fori_loop(0, n_chunks, chunk_body, 0)


def _hgrn(p, af, log_lb, log1m_lb, onorm_g, bsz, seq, aq_blk):
    t = p.shape[0]
    ts = min(512, seq)
    n_seq_tiles = seq // ts
    n_chunks = ts // CHUNK
    kern = functools.partial(_hgrn_kernel, n_chunks=n_chunks)
    row = lambda b, i: b * n_seq_tiles + i
    est = (2 * 3 * ts * A_DIM * 2 + 2 * ts * A_DIM * 4 + 2 * ts * A_DIM * 2 + A_HEADS * 128 * 128 * 4
           + 4 * CHUNK * A_DIM * 4 + CHUNK * SUB * A_DIM * 6)
    return pl.pallas_call(
        kern,
        grid=(bsz, n_seq_tiles),
        in_specs=[
            pl.BlockSpec((ts, A_DIM), lambda b, i: (row(b, i), aq_blk)),
            pl.BlockSpec((ts, A_DIM), lambda b, i: (row(b, i), 0)),
            pl.BlockSpec((ts, A_DIM), lambda b, i: (row(b, i), aq_blk + 1)),
            pl.BlockSpec((ts, A_DIM), lambda b, i: (row(b, i), aq_blk + 2)),
            pl.BlockSpec((1, A_DIM), lambda b, i: (0, 0)),
            pl.BlockSpec((1, A_DIM), lambda b, i: (0, 0)),
            pl.BlockSpec((1, A_DIM), lambda b, i: (0, 0)),
        ],
        out_specs=pl.BlockSpec((ts, A_DIM), lambda b, i: (row(b, i), 0)),
        out_shape=jax.ShapeDtypeStruct((t, A_DIM), BF16),
        scratch_shapes=[
            pltpu.VMEM((A_HEADS, A_HEAD_DIM, A_HEAD_DIM), F32),
            pltpu.VMEM((CHUNK, A_DIM), F32),
            pltpu.VMEM((CHUNK, A_DIM), F32),
            pltpu.VMEM((CHUNK, A_DIM), F32),
            pltpu.VMEM((CHUNK, A_DIM), F32),
            pltpu.VMEM((CHUNK * SUB, A_DIM), BF16),
            pltpu.VMEM((CHUNK * SUB, A_DIM), F32),
        ],
        compiler_params=_cparams(("arbitrary", "arbitrary"), est),
        name="hgrn2_scan",
    )(p, af, p, p, log_lb.reshape(1, A_DIM), log1m_lb.reshape(1, A_DIM), onorm_g.reshape(1, A_DIM))


def _attn_kernel(q_ref, k0_ref, k1_ref, k2_ref, v0_ref, v1_ref, v2_ref, bias_ref, o_ref):
    i = pl.program_id(1)
    tq = q_ref.shape[0]
    hd = C_HEAD_DIM
    scale = C_HEAD_DIM ** -0.5
    col = lax.broadcasted_iota(I32, (1, 3 * tq), 1)
    valid = (col >= 2 * tq) | ((col >= tq) & (i >= 1)) | (i >= 2)
    for h in range(C_HEADS):
        cols = slice(h * hd, (h + 1) * hd)
        qh = q_ref[:, cols]
        kh = jnp.concatenate([k0_ref[:, cols], k1_ref[:, cols], k2_ref[:, cols]], axis=0)
        vh = jnp.concatenate([v0_ref[:, cols], v1_ref[:, cols], v2_ref[:, cols]], axis=0)
        s = lax.dot_general(qh, kh, (((1,), (1,)), ((), ())), preferred_element_type=F32) * scale + bias_ref[h]
        s = jnp.where(valid, s, NEG_BIG)
        m = jnp.max(s, axis=-1, keepdims=True)
        e = jnp.exp(s - m)
        pr = e / jnp.sum(e, axis=-1, keepdims=True)
        o_ref[:, cols] = jnp.dot(pr.astype(BF16), vh, preferred_element_type=F32).astype(BF16)


def _attention_bias_table(rel_bias, tq):
    band = (LEFT_CHUNKS + 1) * CHUNK
    assert tq * 2 == LEFT_CHUNKS * CHUNK, "three key slots of tq rows must cover the band"
    diff = np.arange(CHUNK + band - 1) - (band - 1)
    ext = rel_bias.astype(F32)[:, np.clip(diff + LEFT_CHUNKS * CHUNK, -REL_CLIP, REL_CLIP) + REL_CLIP]
    rev = ext[:, ::-1]
    chunk_tab = jnp.stack([rev[:, CHUNK - 1 - tt:CHUNK - 1 - tt + band] for tt in range(CHUNK)], axis=1)
    rows = [jnp.pad(chunk_tab, ((0, 0), (0, 0), (c * CHUNK, 3 * tq - band - c * CHUNK)), constant_values=NEG_BIG)
            for c in range(tq // CHUNK)]
    return jnp.concatenate(rows, axis=1)


def _attention(p, rel_bias, bsz, seq, q_blk):
    t = p.shape[0]
    tq = ATT_TQ
    nq = seq // tq
    bias = _attention_bias_table(rel_bias, tq)
    row = lambda b, i, back: b * nq + jnp.maximum(i - back, 0)
    kv_spec = lambda blk, back: pl.BlockSpec((tq, C_DIM), lambda b, i: (row(b, i, back), blk))
    est = 2 * 7 * tq * C_DIM * 2 + 2 * C_HEADS * tq * 3 * tq * 4 + 2 * tq * C_DIM * 2 + 6 * tq * 3 * tq * 4
    return pl.pallas_call(
        _attn_kernel,
        grid=(bsz, nq),
        in_specs=[
            pl.BlockSpec((tq, C_DIM), lambda b, i: (b * nq + i, q_blk)),
            kv_spec(q_blk + 1, 2), kv_spec(q_blk + 1, 1), kv_spec(q_blk + 1, 0),
            kv_spec(q_blk + 2, 2), kv_spec(q_blk + 2, 1), kv_spec(q_blk + 2, 0),
            pl.BlockSpec((C_HEADS, tq, 3 * tq), lambda b, i: (0, 0, 0)),
        ],
        out_specs=pl.BlockSpec((tq, C_DIM), lambda b, i: (b * nq + i, 0)),
        out_shape=jax.ShapeDtypeStruct((t, C_DIM), BF16),
        compiler_params=_cparams(("arbitrary", "arbitrary"), est),
        name="chunk_attention",
    )(p, p, p, p, p, p, p, bias)


def _merge_kernel(x_ref, gmod_ref, a_ref, c_ref, bb_ref, bc_ref, bu_ref, hbc_ref, hbu_ref, cw_ref,
                  ga_ref, gb_ref, gc_ref, wa_ref, wb_ref, wc_ref, wo_ref, o_ref, *, tiles_per_batch):
    i = pl.program_id(0)
    tm = x_ref.shape[0]
    u = bc_ref[...].astype(F32) * bu_ref[...].astype(F32)
    halo = hbc_ref[...].astype(F32) * hbu_ref[...].astype(F32)
    halo = jnp.where(i % tiles_per_batch == 0, 0.0, halo)
    hrows = halo.shape[0]
    hm1 = halo[hrows - 1:hrows]
    hm2 = halo[hrows - 2:hrows - 1]
    r = lax.broadcasted_iota(I32, (tm, 1), 0)
    u1 = jnp.where(r == 0, hm1, pltpu.roll(u, 1, 0))
    u2 = jnp.where(r == 0, hm2, jnp.where(r == 1, hm1, pltpu.roll(u, 2, 0)))
    cw = cw_ref[...]
    conv = u2 * cw[0:1] + u1 * cw[1:2] + u * cw[2:3]
    bfeat = (bb_ref[...].astype(F32) * conv).astype(BF16)
    ya = jnp.dot(a_ref[...], wa_ref[...], preferred_element_type=F32)
    yb = jnp.dot(bfeat, wb_ref[...], preferred_element_type=F32)
    yc = jnp.dot(c_ref[...], wc_ref[...], preferred_element_type=F32)
    merged = (ga_ref[...].astype(F32) * ya + gb_ref[...].astype(F32) * yb + gc_ref[...].astype(F32) * yc)
    y = jnp.dot(merged.astype(BF16), wo_ref[...], preferred_element_type=F32)
    o_ref[...] = x_ref[...] + gmod_ref[0] * y


def _merge(x2, seq, gmod, a_feat, c_feat, p, pg, conv_w, wa, wb, wc, wo, b_blk):
    t, d = x2.shape
    tm = min(256, seq)
    tiles_per_batch = seq // tm
    halo = 16
    hb = tm // halo
    kern = functools.partial(_merge_kernel, tiles_per_batch=tiles_per_batch)
    const = lambda shape: pl.BlockSpec(shape, lambda i: (0,) * len(shape))
    halo_spec = lambda blk: pl.BlockSpec((halo, B_WIDTH), lambda i: (jnp.maximum(i * hb - 1, 0), blk))
    est = (4 * tm * d * 4 + 4 * tm * A_DIM * 2 + 6 * tm * B_WIDTH * 2 + 6 * tm * d * 2
           + 2 * (A_DIM + B_WIDTH + C_DIM + d) * d * 2 + 6 * tm * d * 4)
    return pl.pallas_call(
        kern,
        grid=(t // tm,),
        in_specs=[
            pl.BlockSpec((tm, d), lambda i: (i, 0)),
            pl.BlockSpec((1, 1, d), lambda i: (i // tiles_per_batch, 0, 0)),
            pl.BlockSpec((tm, A_DIM), lambda i: (i, 0)),
            pl.BlockSpec((tm, C_DIM), lambda i: (i, 0)),
            pl.BlockSpec((tm, B_WIDTH), lambda i: (i, b_blk)),
            pl.BlockSpec((tm, B_WIDTH), lambda i: (i, b_blk + 1)),
            pl.BlockSpec((tm, B_WIDTH), lambda i: (i, b_blk + 2)),
            halo_spec(b_blk + 1),
            halo_spec(b_blk + 2),
            const((CONV_W, B_WIDTH)),
            pl.BlockSpec((tm, d), lambda i: (i, 0)),
            pl.BlockSpec((tm, d), lambda i: (i, 1)),
            pl.BlockSpec((tm, d), lambda i: (i, 2)),
            const((A_DIM, d)), const((B_WIDTH, d)), const((C_DIM, d)), const((d, d)),
        ],
        out_specs=pl.BlockSpec((tm, d), lambda i: (i, 0)),
        out_shape=jax.ShapeDtypeStruct((t, d), F32),
        compiler_params=_cparams(("arbitrary",), est),
        name="merge_outproj",
    )(x2, gmod, a_feat, c_feat, p, p, p, p, p, conv_w, pg, pg, pg, wa, wb, wc, wo)


def _first_max(vals, iota, n, axis):
    m = jnp.max(vals, axis=axis, keepdims=True)
    idx = jnp.min(jnp.where(vals == m, iota, n), axis=axis, keepdims=True)
    return m, idx


def _router_kernel(x_ref, g_ref, sc_ref, sh_ref, rwt_ref, rb_ref, h_ref, hp_ref, e_ref, w_ref, rank_ref, cnt_ref,
                   cnt_scr):
    step = pl.program_id(0)

    @pl.when(step == 0)
    def _():
        cnt_scr[...] = jnp.zeros_like(cnt_scr)

    tm = x_ref.shape[0]
    x = x_ref[...]
    y = x * lax.rsqrt(jnp.mean(x * x, axis=-1, keepdims=True) + EPS)
    h = y * g_ref[...] * (1.0 + sc_ref[0]) + sh_ref[0]
    h_ref[...] = h.astype(BF16)
    _store_token_tiles(hp_ref, 0, _pack_bf16_pairs(h))

    nt = (((1,), (1,)), ((), ()))
    h_hi, h_mid, _ = _split3(h)
    w_hi, w_mid, _ = _split3(rwt_ref[...])
    logits = (lax.dot_general(w_hi, h_hi, nt, preferred_element_type=F32)
              + lax.dot_general(w_hi, h_mid, nt, preferred_element_type=F32)
              + lax.dot_general(w_mid, h_hi, nt, preferred_element_type=F32))
    scores = _sigmoid(logits)
    sel = scores + rb_ref[:, 0:1]

    g3 = sel.reshape(N_GROUPS, GROUP_SIZE, tm)
    j3 = lax.broadcasted_iota(I32, g3.shape, 1)
    m1, i1 = _first_max(g3, j3, GROUP_SIZE, 1)
    m2 = jnp.max(jnp.where(j3 == i1, -jnp.inf, g3), axis=1, keepdims=True)
    gs = (m1 + m2).reshape(N_GROUPS, tm)
    gi = lax.broadcasted_iota(I32, gs.shape, 0)
    gmask = jnp.zeros(gs.shape, jnp.bool_)
    for _ in range(TOPK_GROUPS):
        _, idx = _first_max(gs, gi, N_GROUPS, 0)
        hit = gi == idx
        gmask = gmask | hit
        gs = jnp.where(hit, -jnp.inf, gs)
    emask = jnp.broadcast_to(gmask.reshape(N_GROUPS, 1, tm), (N_GROUPS, GROUP_SIZE, tm)).reshape(N_EXPERTS, tm)
    masked = jnp.where(emask, sel, -jnp.inf)

    ei = lax.broadcasted_iota(I32, (N_EXPERTS, tm), 0)
    chosen = jnp.zeros((N_EXPERTS, tm), F32)
    e_rows, w_rows = [], []
    for _ in range(TOP_K):
        _, idx = _first_max(masked, ei, N_EXPERTS, 0)
        hit = ei == idx
        e_rows.append(idx)
        w_rows.append(jnp.sum(jnp.where(hit, scores, 0.0), axis=0, keepdims=True))
        chosen = jnp.where(hit, 1.0, chosen)
        masked = jnp.where(hit, -jnp.inf, masked)
    eidx = jnp.concatenate(e_rows, axis=0)
    wts = jnp.concatenate(w_rows, axis=0)
    wts = wts / jnp.sum(wts, axis=0, keepdims=True) * ROUTED_SCALE
    e_ref[...] = eidx
    w_ref[...] = wts

    tr = lax.broadcasted_iota(I32, (tm, tm), 0)
    tc = lax.broadcasted_iota(I32, (tm, tm), 1)
    before = (tr < tc).astype(BF16)
    base = cnt_scr[:, 0:1]
    rank_all = jnp.dot(chosen.astype(BF16), before, preferred_element_type=F32) + base
    ranks = [jnp.sum(jnp.where(ei == e_rows[k], rank_all, 0.0), axis=0, keepdims=True) for k in range(TOP_K)]
    rank_ref[...] = jnp.concatenate(ranks, axis=0).astype(I32)
    total = base + jnp.sum(chosen, axis=1, keepdims=True)
    cnt_scr[...] = jnp.broadcast_to(total, cnt_scr.shape)
    cnt_ref[...] = jnp.broadcast_to(total, cnt_ref.shape).astype(I32)


def _router(x2, seq, g, sc, sh, router_wt, router_bias):
    t, d = x2.shape
    tm = min(512, seq)
    tiles_per_batch = seq // tm
    est = (2 * tm * d * 4 + 2 * tm * d * 2 + 2 * tm * d * 2 + 2 * N_EXPERTS * d * 4 + tm * tm * 2
           + 8 * tm * d * 4 + 16 * N_EXPERTS * tm * 4)
    return pl.pallas_call(
        _router_kernel,
        grid=(t // tm,),
        in_specs=[
            pl.BlockSpec((tm, d), lambda i: (i, 0)),
            pl.BlockSpec((1, d), lambda i: (0, 0)),
            pl.BlockSpec((1, 1, d), lambda i: (i // tiles_per_batch, 0, 0)),
            pl.BlockSpec((1, 1, d), lambda i: (i // tiles_per_batch, 0, 0)),
            pl.BlockSpec((N_EXPERTS, d), lambda i: (0, 0)),
            pl.BlockSpec((N_EXPERTS, LANES), lambda i: (0, 0)),
        ],
        out_specs=[
            pl.BlockSpec((tm, d), lambda i: (i, 0)),
            pl.BlockSpec((tm * d // (2 * LANES), LANES), lambda i: (i, 0)),
            pl.BlockSpec((TOP_K, tm), lambda i: (0, i)),
            pl.BlockSpec((TOP_K, tm), lambda i: (0, i)),
            pl.BlockSpec((TOP_K, tm), lambda i: (0, i)),
            pl.BlockSpec((N_EXPERTS, LANES), lambda i: (0, 0)),
        ],
        out_shape=[
            jax.ShapeDtypeStruct((t, d), BF16),
            jax.ShapeDtypeStruct((t * d // (2 * LANES), LANES), U32),
            jax.ShapeDtypeStruct((TOP_K, t), I32),
            jax.ShapeDtypeStruct((TOP_K, t), F32),
            jax.ShapeDtypeStruct((TOP_K, t), I32),
            jax.ShapeDtypeStruct((N_EXPERTS, LANES), I32),
        ],
        scratch_shapes=[pltpu.VMEM((N_EXPERTS, LANES), F32)],
        compiler_params=_cparams(("arbitrary",), est),
        name="norm_router",
    )(x2, g.reshape(1, d), sc, sh, router_wt, jnp.broadcast_to(router_bias.reshape(N_EXPERTS, 1), (N_EXPERTS, LANES)))


def _pos_kernel(start_ref, e_ref, rank_ref, pos_ref):
    e = e_ref[...]
    acc = rank_ref[...]
    for ex in range(N_EXPERTS):
        acc = acc + jnp.where(e == ex, start_ref[ex], 0)
    pos_ref[...] = acc


def _positions(pstart, eidx, rank):
    k, t = eidx.shape
    tl = min(4096, t)
    return pl.pallas_call(
        _pos_kernel,
        grid_spec=pltpu.PrefetchScalarGridSpec(
            num_scalar_prefetch=1,
            grid=(t // tl,),
            in_specs=[pl.BlockSpec((k, tl), lambda i, s: (0, i)), pl.BlockSpec((k, tl), lambda i, s: (0, i))],
            out_specs=pl.BlockSpec((k, tl), lambda i, s: (0, i)),
        ),
        out_shape=jax.ShapeDtypeStruct((k, t), I32),
        compiler_params=_cparams(("arbitrary",), 8 * k * tl * 4),
        name="moe_positions",
    )(pstart, eidx, rank)


def _dispatch_kernel(pos_ref, hp_ref, xs_ref, sem, *, tm, per):
    def copy(t, k):
        src = hp_ref.at[pl.ds(pl.multiple_of(t * per, per), per)]
        dst = xs_ref.at[pl.ds(pl.multiple_of(pos_ref[k, t] * per, per), per)]
        return pltpu.make_async_copy(src, dst, sem)

    def start(t, c):
        for k in range(TOP_K):
            copy(t, k).start(priority=k % 2)
        return c

    def wait(t, c):
        for k in range(TOP_K):
            copy(t, k).wait()
        return c

    lax.fori_loop(0, tm, start, 0, unroll=2)
    lax.fori_loop(0, tm, wait, 0, unroll=2)


def _dispatch(pos, hp, n_rows, per):
    t = hp.shape[0] // per
    tm = min(512, t)
    kern = functools.partial(_dispatch_kernel, tm=tm, per=per)
    return pl.pallas_call(
        kern,
        grid=(t // tm,),
        in_specs=[
            pl.BlockSpec((TOP_K, tm), lambda i: (0, i), memory_space=pltpu.SMEM),
            pl.BlockSpec((tm * per, LANES), lambda i: (i, 0)),
        ],
        out_specs=pl.BlockSpec(memory_space=pl.ANY),
        out_shape=jax.ShapeDtypeStruct((n_rows * per, LANES), U32),
        scratch_shapes=[pltpu.SemaphoreType.DMA(())],
        compiler_params=pltpu.CompilerParams(dimension_semantics=("arbitrary",), has_side_effects=True,
                                             vmem_limit_bytes=_vmem_limit(4 * tm * per * LANES * 4)),
        name="moe_dispatch",
    )(pos, hp)


def _expert_kernel(blk_exp_ref, n_used_ref, xs_ref, wg_ref, wu_ref, wd_ref, ys_ref, wg_scr, wu_scr, wd_scr, *, per):
    b = pl.program_id(0)
    new_expert = (b == 0) | (blk_exp_ref[b] != blk_exp_ref[jnp.maximum(b - 1, 0)])

    @pl.when(new_expert)
    def _():
        wg_scr[...] = wg_ref[0, 0].astype(BF16)
        wu_scr[...] = wu_ref[0, 0].astype(BF16)
        wd_scr[...] = wd_ref[0, 0].astype(BF16)

    @pl.when(b < n_used_ref[0])
    def _():
        lo, hi = _unpack_bf16_pairs(_load_token_tiles(xs_ref, 0, MOE_BLK, per))
        half = lo.shape[1]
        gate = (jnp.dot(lo, wg_scr[:half], preferred_element_type=F32)
                + jnp.dot(hi, wg_scr[half:], preferred_element_type=F32))
        up = (jnp.dot(lo, wu_scr[:half], preferred_element_type=F32)
              + jnp.dot(hi, wu_scr[half:], preferred_element_type=F32))
        hidden = (_silu(gate) * up).astype(BF16)
        _store_token_tiles(ys_ref, 0, _pack_bf16_pairs(jnp.dot(hidden, wd_scr[...], preferred_element_type=F32)))

    @pl.when(b >= n_used_ref[0])
    def _():
        ys_ref[...] = jnp.zeros_like(ys_ref)


def _experts(blk_exp, n_used, xs, wg, wu, wd, layer, per):
    half = per * LANES
    d = 2 * half
    de = wg.shape[3]
    nb = xs.shape[0] // (MOE_BLK * per)
    est = 4 * MOE_BLK * half * 4 + 2 * 3 * d * de * 4 + 3 * d * de * 2 + MOE_BLK * (2 * de + d) * 4
    return pl.pallas_call(
        functools.partial(_expert_kernel, per=per),
        grid_spec=pltpu.PrefetchScalarGridSpec(
            num_scalar_prefetch=2,
            grid=(nb,),
            in_specs=[
                pl.BlockSpec((MOE_BLK * per, LANES), lambda b, be, nu: (b, 0)),
                pl.BlockSpec((1, 1, d, de), lambda b, be, nu: (layer, be[b], 0, 0)),
                pl.BlockSpec((1, 1, d, de), lambda b, be, nu: (layer, be[b], 0, 0)),
                pl.BlockSpec((1, 1, de, d), lambda b, be, nu: (layer, be[b], 0, 0)),
            ],
            out_specs=pl.BlockSpec((MOE_BLK * per, LANES), lambda b, be, nu: (b, 0)),
            scratch_shapes=[pltpu.VMEM((d, de), BF16), pltpu.VMEM((d, de), BF16), pltpu.VMEM((de, d), BF16)],
        ),
        out_shape=jax.ShapeDtypeStruct(xs.shape, U32),
        compiler_params=_cparams(("arbitrary",), est),
        name="moe_experts",
    )(blk_exp, n_used, xs, wg, wu, wd)


def _combine_kernel(pos_ref, x_ref, gmod_ref, h_ref, wt_ref, sg_ref, su_ref, sd_ref, ys_ref, o_ref, yg_scr, sem,
                    *, tm, per):
    def copy(t, k):
        src = ys_ref.at[pl.ds(pl.multiple_of(pos_ref[k, t] * per, per), per)]
        dst = yg_scr.at[pl.ds(pl.multiple_of((k * tm + t) * per, per), per)]
        return pltpu.make_async_copy(src, dst, sem)

    def start(t, c):
        for k in range(TOP_K):
            copy(t, k).start(priority=k % 2)
        return c

    def wait(t, c):
        for k in range(TOP_K):
            copy(t, k).wait()
        return c

    lax.fori_loop(0, tm, start, 0, unroll=2)
    h = h_ref[...]
    hidden = (_silu(jnp.dot(h, sg_ref[...], preferred_element_type=F32))
              * jnp.dot(h, su_ref[...], preferred_element_type=F32)).astype(BF16)
    shared = jnp.dot(hidden, sd_ref[...], preferred_element_type=F32)
    lax.fori_loop(0, tm, wait, 0, unroll=2)
    half = per * LANES
    acc_lo = shared[:, :half]
    acc_hi = shared[:, half:]
    wt = wt_ref[...]
    for k in range(TOP_K):
        packed = _load_token_tiles(yg_scr, k * tm, tm, per)
        wk = wt[:, k:k + 1]
        acc_lo = acc_lo + wk * lax.bitcast_convert_type(packed << 16, F32)
        acc_hi = acc_hi + wk * lax.bitcast_convert_type(packed & jnp.uint32(0xFFFF0000), F32)
    g = gmod_ref[0]
    o_ref[:, :half] = x_ref[:, :half] + g[:, :half] * acc_lo
    o_ref[:, half:] = x_ref[:, half:] + g[:, half:] * acc_hi


def _combine(pos, x2, seq, gmod, h, wts_t, sg, su, sd, ys, per):
    t, d = x2.shape
    half = d // 2
    ds_ = sg.shape[1]
    tm = min(256, seq)
    tiles_per_batch = seq // tm
    kern = functools.partial(_combine_kernel, tm=tm, per=per)
    const = lambda shape: pl.BlockSpec(shape, lambda i: (0,) * len(shape))
    est = 4 * tm * d * 4 + 2 * tm * d * 2 + 2 * 3 * d * ds_ * 2 + TOP_K * tm * half * 4 + 8 * tm * d * 4
    return pl.pallas_call(
        kern,
        grid=(t // tm,),
        in_specs=[
            pl.BlockSpec((TOP_K, tm), lambda i: (0, i), memory_space=pltpu.SMEM),
            pl.BlockSpec((tm, d), lambda i: (i, 0)),
            pl.BlockSpec((1, 1, d), lambda i: (i // tiles_per_batch, 0, 0)),
            pl.BlockSpec((tm, d), lambda i: (i, 0)),
            pl.BlockSpec((tm, TOP_K), lambda i: (i, 0)),
            const((d, ds_)), const((d, ds_)), const((ds_, d)),
            pl.BlockSpec(memory_space=pl.ANY),
        ],
        out_specs=pl.BlockSpec((tm, d), lambda i: (i, 0)),
        out_shape=jax.ShapeDtypeStruct((t, d), F32),
        scratch_shapes=[pltpu.VMEM((TOP_K * tm * per, LANES), U32), pltpu.SemaphoreType.DMA(())],
        compiler_params=_cparams(("arbitrary",), est),
        name="moe_combine",
    )(pos, x2, gmod, h, wts_t, sg, su, sd, ys)


def _final_kernel(x_ref, g_ref, o_ref):
    x = x_ref[...]
    o_ref[...] = x * lax.rsqrt(jnp.mean(x * x, axis=-1, keepdims=True) + EPS) * g_ref[...]


def _final_norm(x2, g):
    t, d = x2.shape
    tm = min(1024, t)
    return pl.pallas_call(
        _final_kernel,
        grid=(t // tm,),
        in_specs=[pl.BlockSpec((tm, d), lambda i: (i, 0)), pl.BlockSpec((1, d), lambda i: (0, 0))],
        out_specs=pl.BlockSpec((tm, d), lambda i: (i, 0)),
        out_shape=jax.ShapeDtypeStruct((t, d), F32),
        compiler_params=_cparams(("arbitrary",), 4 * tm * d * 4),
        name="final_norm",
    )(x2, g.reshape(1, d))


def _moe_layout(counts, n_blocks):
    padded = (counts + MOE_BLK - 1) // MOE_BLK * MOE_BLK
    pends = jnp.cumsum(padded)
    pstart = (pends - padded).astype(I32)
    blk_row = jnp.arange(n_blocks, dtype=I32)[:, None] * MOE_BLK
    blk_exp = jnp.minimum(jnp.sum((pends[None, :] <= blk_row).astype(I32), axis=1), N_EXPERTS - 1)
    n_used = (pends[-1:] // MOE_BLK).astype(I32)
    return pstart, blk_exp, n_used


def _layer(x2, bsz, seq, layer, mod, norm1_g, w_in, lb, onorm_g, conv_w, rel_bias, w_a, w_b, w_c, w_out, norm2_g,
           router_w, router_bias, wg, wu, wd, sg, su, sd):
    t, d = x2.shape
    sh1, sc1, g1, sh2, sc2, g2 = [m.reshape(bsz, 1, d) for m in jnp.split(mod, N_MOD, axis=-1)]
    n_col_blocks = w_in.shape[2] // A_DIM
    n_mix_blocks = n_col_blocks - (3 * d) // A_DIM
    h1 = _norm_mod(x2, seq, norm1_g, sc1, sh1)
    pg = _project(h1, w_in, layer, tuple(range(n_mix_blocks, n_col_blocks)), "sigmoid", BF16, "proj_gates")
    af = _project(h1, w_in, layer, (1,), "none", F32, "proj_forget")
    p = _project(h1, w_in, layer, (4, 5, 0, 2, 3, 6, 7, 8), "none", BF16, "proj_mixers")
    b_blk = 0
    aq_blk = (3 * B_WIDTH) // A_DIM
    q_blk = aq_blk + 3
    a_feat = _hgrn(p, af, jnp.log(lb), jnp.log1p(-lb), onorm_g, bsz, seq, aq_blk)
    c_feat = _attention(p, rel_bias, bsz, seq, q_blk)
    x2 = _merge(x2, seq, g1, a_feat, c_feat, p, pg, conv_w, w_a.astype(BF16), w_b.astype(BF16), w_c.astype(BF16),
                w_out.astype(BF16), b_blk)

    h, hp, eidx, wts, rank, cnt = _router(x2, seq, norm2_g, sc2, sh2, router_w.T, router_bias)
    n_blocks = (t * TOP_K) // MOE_BLK + N_EXPERTS
    pstart, blk_exp, n_used = _moe_layout(cnt[:, 0], n_blocks)
    pos = _positions(pstart, eidx, rank)
    per = d // (2 * LANES)
    xs = _dispatch(pos, hp, n_blocks * MOE_BLK, per)
    ys = _experts(blk_exp, n_used, xs, wg, wu, wd, layer, per)
    return _combine(pos, x2, seq, g2, h, wts.T, sg.astype(BF16), su.astype(BF16), sd.astype(BF16), ys, per)


def kernel(x, c, ada_w, ada_b, norm1_g, w_in, hgrn_lb_logits, hgrn_onorm_g, conv_w, rel_bias, w_branch_a, w_branch_b, w_branch_c, w_out, norm2_g, router_w, router_bias, moe_w_gate, moe_w_up, moe_w_down, shared_w_gate, shared_w_up, shared_w_down, final_g):
    bsz, seq, d = x.shape
    depth = ada_w.shape[0]
    mod = _modulation(c, ada_w, ada_b)
    lbs = jnp.cumsum(jax.nn.softmax(hgrn_lb_logits.astype(F32), axis=0), axis=0)
    lbs = lbs - lbs[0]
    x2 = x.reshape(bsz * seq, d)
    for l in range(depth):
        x2 = _layer(x2, bsz, seq, l, mod[l], norm1_g[l], w_in, lbs[l], hgrn_onorm_g[l], conv_w[l], rel_bias[l],
                    w_branch_a[l], w_branch_b[l], w_branch_c[l], w_out[l], norm2_g[l], router_w[l], router_bias[l],
                    moe_w_gate, moe_w_up, moe_w_down, shared_w_gate[l], shared_w_up[l], shared_w_down[l])
    return _final_norm(x2, final_g).reshape(bsz, seq, d)
```

```python
import functools

import numpy as np
import jax
import jax.numpy as jnp
from jax import lax
from jax.experimental import pallas as pl
from jax.experimental.pallas import tpu as pltpu

F32 = jnp.float32
BF16 = jnp.bfloat16
I32 = jnp.int32
U32 = jnp.uint32

CHUNK = 64
EPS = 1e-6
A_HEADS = 6
A_HEAD_DIM = 128
A_DIM = A_HEADS * A_HEAD_DIM
B_WIDTH = 512
CONV_W = 3
C_HEADS = 6
C_HEAD_DIM = 128
C_DIM = C_HEADS * C_HEAD_DIM
LEFT_CHUNKS = 8
REL_CLIP = 256
N_EXPERTS = 64
TOP_K = 8
N_GROUPS = 8
GROUP_SIZE = N_EXPERTS // N_GROUPS
TOPK_GROUPS = 4
ROUTED_SCALE = 2.5
N_MOD = 6

V7X_VMEM_BYTES = 64 * 1024 * 1024
LANES = 128
NEG_BIG = -1e30

SUB = 16
FACTOR_SAFE_LOG = 80.0
MOE_BLK = 512
ATT_TQ = 256


def _vmem_limit(nbytes):
    return int(min(max(nbytes * 3 // 2, 16 * 1024 * 1024), V7X_VMEM_BYTES - 8 * 1024 * 1024))


def _cparams(sem, nbytes):
    return pltpu.CompilerParams(dimension_semantics=sem, vmem_limit_bytes=_vmem_limit(nbytes))


def _sigmoid(x):
    return 1.0 / (1.0 + jnp.exp(-x))


def _silu(x):
    return x * _sigmoid(x)


def _split3(x):
    hi = x.astype(BF16)
    r1 = x - hi.astype(F32)
    mid = r1.astype(BF16)
    lo = (r1 - mid.astype(F32)).astype(BF16)
    return hi, mid, lo


def _pack_bf16_pairs(x):
    n = x.shape[1] // 2
    xb = x.astype(BF16).astype(F32)
    lo = lax.bitcast_convert_type(xb[:, :n], U32) >> 16
    hi = lax.bitcast_convert_type(xb[:, n:], U32) & jnp.uint32(0xFFFF0000)
    return lo | hi


def _unpack_bf16_pairs(p):
    lo = lax.bitcast_convert_type(p << 16, F32).astype(BF16)
    hi = lax.bitcast_convert_type(p & jnp.uint32(0xFFFF0000), F32).astype(BF16)
    return lo, hi


def _store_token_tiles(ref, first_token, mat):
    n, w = mat.shape
    per = w // LANES
    for s in range(per):
        ref[pl.ds(first_token * per + s, n, stride=per), :] = mat[:, s * LANES:(s + 1) * LANES]


def _load_token_tiles(ref, first_token, n, per):
    return jnp.concatenate([ref[pl.ds(first_token * per + s, n, stride=per), :] for s in range(per)], axis=1)


def _mod_kernel(c_ref, w_ref, b_ref, o_ref):
    c = c_ref[...]
    o_ref[0] = jnp.dot(_silu(c), w_ref[0], preferred_element_type=F32) + b_ref[0]


def _modulation(c, ada_w, ada_b):
    depth, d, n = ada_w.shape
    bsz = c.shape[0]
    rows = 8
    c_pad = jnp.zeros((rows, d), F32).at[:bsz].set(c)
    tn = 1536 if n % 1536 == 0 else n
    out = pl.pallas_call(
        _mod_kernel,
        grid=(depth, n // tn),
        in_specs=[
            pl.BlockSpec((rows, d), lambda l, j: (0, 0)),
            pl.BlockSpec((1, d, tn), lambda l, j: (l, 0, j)),
            pl.BlockSpec((1, 1, tn), lambda l, j: (l, 0, j)),
        ],
        out_specs=pl.BlockSpec((1, rows, tn), lambda l, j: (l, 0, j)),
        out_shape=jax.ShapeDtypeStruct((depth, rows, n), F32),
        compiler_params=_cparams(("arbitrary", "arbitrary"), 2 * d * tn * 4 + 4 * rows * tn * 4),
        name="adaln_mod",
    )(c_pad, ada_w, ada_b.reshape(depth, 1, n))
    return out[:, :bsz]


def _norm_kernel(x_ref, g_ref, sc_ref, sh_ref, h_ref):
    x = x_ref[...]
    y = x * lax.rsqrt(jnp.mean(x * x, axis=-1, keepdims=True) + EPS)
    h_ref[...] = (y * g_ref[...] * (1.0 + sc_ref[0]) + sh_ref[0]).astype(BF16)


def _norm_mod(x2, seq, g, sc, sh):
    t, d = x2.shape
    tm = min(512, seq)
    tiles_per_batch = seq // tm
    return pl.pallas_call(
        _norm_kernel,
        grid=(t // tm,),
        in_specs=[
            pl.BlockSpec((tm, d), lambda i: (i, 0)),
            pl.BlockSpec((1, d), lambda i: (0, 0)),
            pl.BlockSpec((1, 1, d), lambda i: (i // tiles_per_batch, 0, 0)),
            pl.BlockSpec((1, 1, d), lambda i: (i // tiles_per_batch, 0, 0)),
        ],
        out_specs=pl.BlockSpec((tm, d), lambda i: (i, 0)),
        out_shape=jax.ShapeDtypeStruct((t, d), BF16),
        compiler_params=_cparams(("arbitrary",), 6 * tm * d * 4),
        name="norm_mod",
    )(x2, g.reshape(1, d), sc, sh)


def _proj_kernel(cmap_ref, h_ref, w_ref, o_ref, w_scr, *, act):
    @pl.when(pl.program_id(1) == 0)
    def _():
        w_scr[...] = w_ref[0].astype(BF16)

    acc = jnp.dot(h_ref[...], w_scr[...], preferred_element_type=F32)
    if act == "sigmoid":
        acc = _sigmoid(acc)
    o_ref[...] = acc.astype(o_ref.dtype)


def _project(h, w_all, layer, col_blocks, act, out_dtype, name):
    t, d = h.shape
    tn = A_DIM
    nj = len(col_blocks)
    tm = min(1024, t)
    osz = jnp.dtype(out_dtype).itemsize
    est = 2 * tm * d * 2 + 2 * d * tn * 4 + d * tn * 2 + 2 * tm * tn * osz + 2 * tm * tn * 4
    return pl.pallas_call(
        functools.partial(_proj_kernel, act=act),
        grid_spec=pltpu.PrefetchScalarGridSpec(
            num_scalar_prefetch=1,
            grid=(nj, t // tm),
            in_specs=[
                pl.BlockSpec((tm, d), lambda j, i, cm: (i, 0)),
                pl.BlockSpec((1, d, tn), lambda j, i, cm: (layer, 0, cm[j])),
            ],
            out_specs=pl.BlockSpec((tm, tn), lambda j, i, cm: (i, j)),
            scratch_shapes=[pltpu.VMEM((d, tn), BF16)],
        ),
        out_shape=jax.ShapeDtypeStruct((t, nj * tn), out_dtype),
        compiler_params=_cparams(("arbitrary", "arbitrary"), est),
        name=name,
    )(jnp.asarray(col_blocks, I32), h, w_all)


def _hgrn_kernel(aq_ref, af_ref, ai_ref, ag_ref, loglb_ref, log1mlb_ref, og_ref, sel_ref, o_ref,
                 st_scr, lf_scr, q_scr, k_scr, v_scr, b_scr, p_scr, *, n_chunks):
    @pl.when(pl.program_id(1) == 0)
    def _():
        st_scr[...] = jnp.zeros_like(st_scr)

    hd = A_HEAD_DIM
    n_sub = CHUNK // SUB
    nt = (((1,), (1,)), ((), ()))
    row_c = lax.broadcasted_iota(I32, (CHUNK, CHUNK), 0)
    col_c = lax.broadcasted_iota(I32, (CHUNK, CHUNK), 1)
    tri = (col_c <= row_c).astype(BF16)
    sub_row = lax.broadcasted_iota(I32, (SUB, 1), 0)
    chunk_row = lax.broadcasted_iota(I32, (CHUNK, 1), 0)
    log_lb = loglb_ref[...]
    log1m_lb = log1mlb_ref[...]
    og = og_ref[...]

    def log_f_body(ci, lowest):
        rows = pl.ds(pl.multiple_of(ci * CHUNK, CHUNK), CHUNK)
        af = af_ref[rows, :]
        ls = jnp.minimum(af, 0.0) - jnp.log(1.0 + jnp.exp(-jnp.abs(af)))
        t2 = log1m_lb + ls
        lf = jnp.maximum(log_lb, t2) + jnp.log(1.0 + jnp.exp(-jnp.abs(log_lb - t2)))
        lf_scr[rows, :] = lf
        return jnp.minimum(lowest, jnp.sum(lf, axis=0, keepdims=True))

    lowest_sum = lax.fori_loop(0, n_chunks, log_f_body, jnp.zeros((1, A_DIM), F32), unroll=2)
    safe = jnp.min(lowest_sum) > -FACTOR_SAFE_LOG

    def prep(ci):
        r0 = pl.multiple_of(ci * CHUNK, CHUNK)
        rows = pl.ds(r0, CHUNK)
        lf = lf_scr[rows, :]
        hi, mid, lo = _split3(lf)
        b = (jnp.dot(tri, hi, preferred_element_type=F32) + jnp.dot(tri, mid, preferred_element_type=F32)
             + jnp.dot(tri, lo, preferred_element_type=F32))
        q = aq_ref[rows, :].astype(F32) * (A_HEAD_DIM ** -0.5)
        return rows, 1.0 - jnp.exp(lf), b, q, ai_ref[rows, :]

    def head_tail(h, o, kh, vh, bh, bl):
        ktil = (kh * jnp.exp(bl - bh)).astype(BF16)
        upd = lax.dot_general(vh, ktil, (((0,), (0,)), ((), ())), preferred_element_type=F32)
        st_scr[h] = st_scr[h] * jnp.exp(bl) + upd
        return o * lax.rsqrt(jnp.mean(o * o, axis=-1, keepdims=True) + EPS)

    def finish(rows, outs):
        o_all = jnp.concatenate(outs, axis=1)
        gate = ag_ref[rows, :].astype(F32)
        o_ref[rows, :] = (o_all * og * _silu(gate)).astype(BF16)

    def fast_chunk(ci, carry):
        rows, k_all, b_all, q_all, v_all = prep(ci)
        qt_all = (q_all * jnp.exp(b_all)).astype(BF16)
        kt_all = (k_all * jnp.exp(-b_all)).astype(BF16)
        b_last = b_all[CHUNK - 1:CHUNK]
        outs = []
        for h in range(A_HEADS):
            cols = slice(h * hd, (h + 1) * hd)
            vh = v_all[:, cols]
            qt = qt_all[:, cols]
            a = lax.dot_general(qt, kt_all[:, cols], nt, preferred_element_type=F32)
            a = jnp.where(col_c <= row_c, a, 0.0).astype(BF16)
            o = (jnp.dot(a, vh, preferred_element_type=F32)
                 + lax.dot_general(qt, st_scr[h].astype(BF16), nt, preferred_element_type=F32))
            outs.append(head_tail(h, o, k_all[:, cols], vh, b_all[:, cols], b_last[:, cols]))
        finish(rows, outs)
        return carry

    def exact_chunk(ci, carry):
        rows, k_new, b_new, q_new, v_new = prep(ci)
        k_scr[...] = k_new
        b_scr[...] = b_new
        q_scr[...] = q_new
        v_scr[...] = v_new.astype(F32)

        for i in range(n_sub):
            s0 = i * SUB
            b_blk = b_scr[pl.ds(s0, SUB), :]
            q_blk = q_scr[pl.ds(s0, SUB), :]
            for s in range(SUB):
                b_s = b_scr[pl.ds(s0 + s, 1), :]
                k_s = k_scr[pl.ds(s0 + s, 1), :]
                e = jnp.exp(jnp.where(sub_row >= s, b_blk - b_s, NEG_BIG))
                pv = (q_blk * k_s * e).astype(BF16)
                for h in range(A_HEADS):
                    p_scr[pl.ds((i * A_HEADS + h) * SUB, SUB), s * hd:(s + 1) * hd] = pv[:, h * hd:(h + 1) * hd]
        diag = jnp.dot(p_scr[...], sel_ref[...], preferred_element_type=F32)

        b_all = b_scr[...]
        q_all = q_scr[...]
        k_all = k_scr[...]
        qhat, khat = [], []
        for i in range(1, n_sub):
            s0 = i * SUB
            beta = b_scr[pl.ds(s0 - 1, 1), :]
            qhat.append((q_all[s0:s0 + SUB] * jnp.exp(b_all[s0:s0 + SUB] - beta)).astype(BF16))
            khat.append(jnp.where(chunk_row < s0, k_all * jnp.exp(jnp.minimum(beta - b_all, 0.0)), 0.0).astype(BF16))
        zero_blk = jnp.zeros((SUB, hd), BF16)

        b_last = b_scr[pl.ds(CHUNK - 1, 1), :]
        outs = []
        for h in range(A_HEADS):
            cols = slice(h * hd, (h + 1) * hd)
            qh = q_all[:, cols]
            kh = k_all[:, cols]
            vh = v_scr[:, cols].astype(BF16)
            bh = b_all[:, cols]
            st = st_scr[h]
            lhs_rows = [jnp.concatenate([zero_blk] * (n_sub - 1), axis=1)]
            for i in range(1, n_sub):
                parts = [zero_blk] * (n_sub - 1)
                parts[i - 1] = qhat[i - 1][:, cols]
                lhs_rows.append(jnp.concatenate(parts, axis=1))
            lhs = jnp.concatenate(lhs_rows, axis=0)
            rhs = jnp.concatenate([kk[:, cols] for kk in khat], axis=1)
            a_off = lax.dot_general(lhs, rhs, nt, preferred_element_type=F32)
            a_diag = jnp.concatenate(
                [pltpu.roll(diag[(i * A_HEADS + h) * SUB:(i * A_HEADS + h + 1) * SUB], i * SUB, 1) if i else
                 diag[h * SUB:(h + 1) * SUB] for i in range(n_sub)], axis=0)
            a = (a_diag[:, :CHUNK] + a_off).astype(BF16)
            qt = (qh * jnp.exp(bh)).astype(BF16)
            o = (jnp.dot(a, vh, preferred_element_type=F32)
                 + lax.dot_general(qt, st.astype(BF16), nt, preferred_element_type=F32))
            outs.append(head_tail(h, o, kh, vh, bh, b_last[:, cols]))
        finish(rows, outs)
        return carry

    @pl.when(safe)
    def _():
        lax.fori_loop(0, n_chunks, fast_chunk, 0, unroll=4)

    @pl.when(jnp.logical_not(safe))
    def _():
        lax.fori_loop(0, n_chunks, exact_chunk, 0)


def _hgrn(p, af, log_lb, log1m_lb, onorm_g, bsz, seq, aq_blk):
    t = p.shape[0]
    ts = min(512, seq)
    n_seq_tiles = seq // ts
    n_chunks = ts // CHUNK
    kern = functools.partial(_hgrn_kernel, n_chunks=n_chunks)
    row = lambda b, i: b * n_seq_tiles + i
    est = (2 * 3 * ts * A_DIM * 2 + 2 * ts * A_DIM * 4 + 2 * ts * A_DIM * 2 + A_HEADS * 128 * 128 * 4
           + 4 * CHUNK * A_DIM * 4 + CHUNK * SUB * A_DIM * 6 + 3 * ts * A_DIM * 4)
    sel_rows = np.arange(SUB * A_HEAD_DIM)
    selector = jnp.asarray(sel_rows[:, None] // A_HEAD_DIM == np.arange(A_HEAD_DIM)[None, :], BF16)
    return pl.pallas_call(
        kern,
        grid=(bsz, n_seq_tiles),
        in_specs=[
            pl.BlockSpec((ts, A_DIM), lambda b, i: (row(b, i), aq_blk)),
            pl.BlockSpec((ts, A_DIM), lambda b, i: (row(b, i), 0)),
            pl.BlockSpec((ts, A_DIM), lambda b, i: (row(b, i), aq_blk + 1)),
            pl.BlockSpec((ts, A_DIM), lambda b, i: (row(b, i), aq_blk + 2)),
            pl.BlockSpec((1, A_DIM), lambda b, i: (0, 0)),
            pl.BlockSpec((1, A_DIM), lambda b, i: (0, 0)),
            pl.BlockSpec((1, A_DIM), lambda b, i: (0, 0)),
            pl.BlockSpec((SUB * A_HEAD_DIM, A_HEAD_DIM), lambda b, i: (0, 0)),
        ],
        out_specs=pl.BlockSpec((ts, A_DIM), lambda b, i: (row(b, i), 0)),
        out_shape=jax.ShapeDtypeStruct((t, A_DIM), BF16),
        scratch_shapes=[
            pltpu.VMEM((A_HEADS, A_HEAD_DIM, A_HEAD_DIM), F32),
            pltpu.VMEM((ts, A_DIM), F32),
            pltpu.VMEM((CHUNK, A_DIM), F32),
            pltpu.VMEM((CHUNK, A_DIM), F32),
            pltpu.VMEM((CHUNK, A_DIM), F32),
            pltpu.VMEM((CHUNK, A_DIM), F32),
            pltpu.VMEM((CHUNK * A_HEADS, SUB * A_HEAD_DIM), BF16),
        ],
        compiler_params=_cparams(("arbitrary", "arbitrary"), est),
        name="hgrn2_scan",
    )(p, af, p, p, log_lb.reshape(1, A_DIM), log1m_lb.reshape(1, A_DIM), onorm_g.reshape(1, A_DIM), selector)


def _attn_kernel(q_ref, k0_ref, k1_ref, k2_ref, v0_ref, v1_ref, v2_ref, bias_ref, o_ref):
    i = pl.program_id(1)
    tq = q_ref.shape[0]
    hd = C_HEAD_DIM
    scale = C_HEAD_DIM ** -0.5
    col = lax.broadcasted_iota(I32, (1, 3 * tq), 1)
    valid = (col >= 2 * tq) | ((col >= tq) & (i >= 1)) | (i >= 2)
    for h in range(C_HEADS):
        cols = slice(h * hd, (h + 1) * hd)
        qh = q_ref[:, cols]
        kh = jnp.concatenate([k0_ref[:, cols], k1_ref[:, cols], k2_ref[:, cols]], axis=0)
        vh = jnp.concatenate([v0_ref[:, cols], v1_ref[:, cols], v2_ref[:, cols]], axis=0)
        s = lax.dot_general(qh, kh, (((1,), (1,)), ((), ())), preferred_element_type=F32) * scale + bias_ref[h]
        s = jnp.where(valid, s, NEG_BIG)
        m = jnp.max(s, axis=-1, keepdims=True)
        e = jnp.exp(s - m)
        pr = e / jnp.sum(e, axis=-1, keepdims=True)
        o_ref[:, cols] = jnp.dot(pr.astype(BF16), vh, preferred_element_type=F32).astype(BF16)


def _attention_bias_table(rel_bias, tq):
    band = (LEFT_CHUNKS + 1) * CHUNK
    assert tq * 2 == LEFT_CHUNKS * CHUNK, "three key slots of tq rows must cover the band"
    diff = np.arange(CHUNK + band - 1) - (band - 1)
    ext = rel_bias.astype(F32)[:, np.clip(diff + LEFT_CHUNKS * CHUNK, -REL_CLIP, REL_CLIP) + REL_CLIP]
    rev = ext[:, ::-1]
    chunk_tab = jnp.stack([rev[:, CHUNK - 1 - tt:CHUNK - 1 - tt + band] for tt in range(CHUNK)], axis=1)
    rows = [jnp.pad(chunk_tab, ((0, 0), (0, 0), (c * CHUNK, 3 * tq - band - c * CHUNK)), constant_values=NEG_BIG)
            for c in range(tq // CHUNK)]
    return jnp.concatenate(rows, axis=1)


def _attention(p, rel_bias, bsz, seq, q_blk):
    t = p.shape[0]
    tq = ATT_TQ
    nq = seq // tq
    bias = _attention_bias_table(rel_bias, tq)
    row = lambda b, i, back: b * nq + jnp.maximum(i - back, 0)
    kv_spec = lambda blk, back: pl.BlockSpec((tq, C_DIM), lambda b, i: (row(b, i, back), blk))
    est = 2 * 7 * tq * C_DIM * 2 + 2 * C_HEADS * tq * 3 * tq * 4 + 2 * tq * C_DIM * 2 + 6 * tq * 3 * tq * 4
    return pl.pallas_call(
        _attn_kernel,
        grid=(bsz, nq),
        in_specs=[
            pl.BlockSpec((tq, C_DIM), lambda b, i: (b * nq + i, q_blk)),
            kv_spec(q_blk + 1, 2), kv_spec(q_blk + 1, 1), kv_spec(q_blk + 1, 0),
            kv_spec(q_blk + 2, 2), kv_spec(q_blk + 2, 1), kv_spec(q_blk + 2, 0),
            pl.BlockSpec((C_HEADS, tq, 3 * tq), lambda b, i: (0, 0, 0)),
        ],
        out_specs=pl.BlockSpec((tq, C_DIM), lambda b, i: (b * nq + i, 0)),
        out_shape=jax.ShapeDtypeStruct((t, C_DIM), BF16),
        compiler_params=_cparams(("arbitrary", "arbitrary"), est),
        name="chunk_attention",
    )(p, p, p, p, p, p, p, bias)


def _merge_kernel(x_ref, gmod_ref, a_ref, c_ref, bb_ref, bc_ref, bu_ref, hbc_ref, hbu_ref, cw_ref,
                  ga_ref, gb_ref, gc_ref, wa_ref, wb_ref, wc_ref, wo_ref, o_ref, *, tiles_per_batch):
    i = pl.program_id(0)
    tm = x_ref.shape[0]
    u = bc_ref[...].astype(F32) * bu_ref[...].astype(F32)
    halo = hbc_ref[...].astype(F32) * hbu_ref[...].astype(F32)
    halo = jnp.where(i % tiles_per_batch == 0, 0.0, halo)
    hrows = halo.shape[0]
    hm1 = halo[hrows - 1:hrows]
    hm2 = halo[hrows - 2:hrows - 1]
    r = lax.broadcasted_iota(I32, (tm, 1), 0)
    u1 = jnp.where(r == 0, hm1, pltpu.roll(u, 1, 0))
    u2 = jnp.where(r == 0, hm2, jnp.where(r == 1, hm1, pltpu.roll(u, 2, 0)))
    cw = cw_ref[...]
    conv = u2 * cw[0:1] + u1 * cw[1:2] + u * cw[2:3]
    bfeat = (bb_ref[...].astype(F32) * conv).astype(BF16)
    ya = jnp.dot(a_ref[...], wa_ref[...], preferred_element_type=F32)
    yb = jnp.dot(bfeat, wb_ref[...], preferred_element_type=F32)
    yc = jnp.dot(c_ref[...], wc_ref[...], preferred_element_type=F32)
    merged = (ga_ref[...].astype(F32) * ya + gb_ref[...].astype(F32) * yb + gc_ref[...].astype(F32) * yc)
    y = jnp.dot(merged.astype(BF16), wo_ref[...], preferred_element_type=F32)
    o_ref[...] = x_ref[...] + gmod_ref[0] * y


def _merge(x2, seq, gmod, a_feat, c_feat, p, pg, conv_w, wa, wb, wc, wo, b_blk):
    t, d = x2.shape
    tm = min(256, seq)
    tiles_per_batch = seq // tm
    halo = 16
    hb = tm // halo
    kern = functools.partial(_merge_kernel, tiles_per_batch=tiles_per_batch)
    const = lambda shape: pl.BlockSpec(shape, lambda i: (0,) * len(shape))
    halo_spec = lambda blk: pl.BlockSpec((halo, B_WIDTH), lambda i: (jnp.maximum(i * hb - 1, 0), blk))
    est = (4 * tm * d * 4 + 4 * tm * A_DIM * 2 + 6 * tm * B_WIDTH * 2 + 6 * tm * d * 2
           + 2 * (A_DIM + B_WIDTH + C_DIM + d) * d * 2 + 6 * tm * d * 4)
    return pl.pallas_call(
        kern,
        grid=(t // tm,),
        in_specs=[
            pl.BlockSpec((tm, d), lambda i: (i, 0)),
            pl.BlockSpec((1, 1, d), lambda i: (i // tiles_per_batch, 0, 0)),
            pl.BlockSpec((tm, A_DIM), lambda i: (i, 0)),
            pl.BlockSpec((tm, C_DIM), lambda i: (i, 0)),
            pl.BlockSpec((tm, B_WIDTH), lambda i: (i, b_blk)),
            pl.BlockSpec((tm, B_WIDTH), lambda i: (i, b_blk + 1)),
            pl.BlockSpec((tm, B_WIDTH), lambda i: (i, b_blk + 2)),
            halo_spec(b_blk + 1),
            halo_spec(b_blk + 2),
            const((CONV_W, B_WIDTH)),
            pl.BlockSpec((tm, d), lambda i: (i, 0)),
            pl.BlockSpec((tm, d), lambda i: (i, 1)),
            pl.BlockSpec((tm, d), lambda i: (i, 2)),
            const((A_DIM, d)), const((B_WIDTH, d)), const((C_DIM, d)), const((d, d)),
        ],
        out_specs=pl.BlockSpec((tm, d), lambda i: (i, 0)),
        out_shape=jax.ShapeDtypeStruct((t, d), F32),
        compiler_params=_cparams(("arbitrary",), est),
        name="merge_outproj",
    )(x2, gmod, a_feat, c_feat, p, p, p, p, p, conv_w, pg, pg, pg, wa, wb, wc, wo)


def _first_max(vals, iota, n, axis):
    m = jnp.max(vals, axis=axis, keepdims=True)
    idx = jnp.min(jnp.where(vals == m, iota, n), axis=axis, keepdims=True)
    return m, idx


def _router_kernel(x_ref, g_ref, sc_ref, sh_ref, rwt_ref, rb_ref, h_ref, hp_ref, e_ref, w_ref, rank_ref, cnt_ref,
                   cnt_scr):
    step = pl.program_id(0)

    @pl.when(step == 0)
    def _():
        cnt_scr[...] = jnp.zeros_like(cnt_scr)

    tm = x_ref.shape[0]
    x = x_ref[...]
    y = x * lax.rsqrt(jnp.mean(x * x, axis=-1, keepdims=True) + EPS)
    h = y * g_ref[...] * (1.0 + sc_ref[0]) + sh_ref[0]
    h_ref[...] = h.astype(BF16)
    _store_token_tiles(hp_ref, 0, _pack_bf16_pairs(h))

    nt = (((1,), (1,)), ((), ()))
    h_hi, h_mid, _ = _split3(h)
    w_hi, w_mid, _ = _split3(rwt_ref[...])
    logits = (lax.dot_general(w_hi, h_hi, nt, preferred_element_type=F32)
              + lax.dot_general(w_hi, h_mid, nt, preferred_element_type=F32)
              + lax.dot_general(w_mid, h_hi, nt, preferred_element_type=F32))
    scores = _sigmoid(logits)
    sel = scores + rb_ref[:, 0:1]

    g3 = sel.reshape(N_GROUPS, GROUP_SIZE, tm)
    j3 = lax.broadcasted_iota(I32, g3.shape, 1)
    m1, i1 = _first_max(g3, j3, GROUP_SIZE, 1)
    m2 = jnp.max(jnp.where(j3 == i1, -jnp.inf, g3), axis=1, keepdims=True)
    gs = (m1 + m2).reshape(N_GROUPS, tm)
    gi = lax.broadcasted_iota(I32, gs.shape, 0)
    gmask = jnp.zeros(gs.shape, jnp.bool_)
    for _ in range(TOPK_GROUPS):
        _, idx = _first_max(gs, gi, N_GROUPS, 0)
        hit = gi == idx
        gmask = gmask | hit
        gs = jnp.where(hit, -jnp.inf, gs)
    emask = jnp.broadcast_to(gmask.reshape(N_GROUPS, 1, tm), (N_GROUPS, GROUP_SIZE, tm)).reshape(N_EXPERTS, tm)
    masked = jnp.where(emask, sel, -jnp.inf)

    ei = lax.broadcasted_iota(I32, (N_EXPERTS, tm), 0)
    chosen = jnp.zeros((N_EXPERTS, tm), F32)
    e_rows, w_rows = [], []
    for _ in range(TOP_K):
        _, idx = _first_max(masked, ei, N_EXPERTS, 0)
        hit = ei == idx
        e_rows.append(idx)
        w_rows.append(jnp.sum(jnp.where(hit, scores, 0.0), axis=0, keepdims=True))
        chosen = jnp.where(hit, 1.0, chosen)
        masked = jnp.where(hit, -jnp.inf, masked)
    eidx = jnp.concatenate(e_rows, axis=0)
    wts = jnp.concatenate(w_rows, axis=0)
    wts = wts / jnp.sum(wts, axis=0, keepdims=True) * ROUTED_SCALE
    e_ref[...] = eidx
    w_ref[...] = wts

    tr = lax.broadcasted_iota(I32, (tm, tm), 0)
    tc = lax.broadcasted_iota(I32, (tm, tm), 1)
    before = (tr < tc).astype(BF16)
    base = cnt_scr[:, 0:1]
    rank_all = jnp.dot(chosen.astype(BF16), before, preferred_element_type=F32) + base
    ranks = [jnp.sum(jnp.where(ei == e_rows[k], rank_all, 0.0), axis=0, keepdims=True) for k in range(TOP_K)]
    rank_ref[...] = jnp.concatenate(ranks, axis=0).astype(I32)
    total = base + jnp.sum(chosen, axis=1, keepdims=True)
    cnt_scr[...] = jnp.broadcast_to(total, cnt_scr.shape)
    cnt_ref[...] = jnp.broadcast_to(total, cnt_ref.shape).astype(I32)


def _router(x2, seq, g, sc, sh, router_wt, router_bias):
    t, d = x2.shape
    tm = min(512, seq)
    tiles_per_batch = seq // tm
    est = (2 * tm * d * 4 + 2 * tm * d * 2 + 2 * tm * d * 2 + 2 * N_EXPERTS * d * 4 + tm * tm * 2
           + 8 * tm * d * 4 + 16 * N_EXPERTS * tm * 4)
    return pl.pallas_call(
        _router_kernel,
        grid=(t // tm,),
        in_specs=[
            pl.BlockSpec((tm, d), lambda i: (i, 0)),
            pl.BlockSpec((1, d), lambda i: (0, 0)),
            pl.BlockSpec((1, 1, d), lambda i: (i // tiles_per_batch, 0, 0)),
            pl.BlockSpec((1, 1, d), lambda i: (i // tiles_per_batch, 0, 0)),
            pl.BlockSpec((N_EXPERTS, d), lambda i: (0, 0)),
            pl.BlockSpec((N_EXPERTS, LANES), lambda i: (0, 0)),
        ],
        out_specs=[
            pl.BlockSpec((tm, d), lambda i: (i, 0)),
            pl.BlockSpec((tm * d // (2 * LANES), LANES), lambda i: (i, 0)),
            pl.BlockSpec((TOP_K, tm), lambda i: (0, i)),
            pl.BlockSpec((TOP_K, tm), lambda i: (0, i)),
            pl.BlockSpec((TOP_K, tm), lambda i: (0, i)),
            pl.BlockSpec((N_EXPERTS, LANES), lambda i: (0, 0)),
        ],
        out_shape=[
            jax.ShapeDtypeStruct((t, d), BF16),
            jax.ShapeDtypeStruct((t * d // (2 * LANES), LANES), U32),
            jax.ShapeDtypeStruct((TOP_K, t), I32),
            jax.ShapeDtypeStruct((TOP_K, t), F32),
            jax.ShapeDtypeStruct((TOP_K, t), I32),
            jax.ShapeDtypeStruct((N_EXPERTS, LANES), I32),
        ],
        scratch_shapes=[pltpu.VMEM((N_EXPERTS, LANES), F32)],
        compiler_params=_cparams(("arbitrary",), est),
        name="norm_router",
    )(x2, g.reshape(1, d), sc, sh, router_wt, jnp.broadcast_to(router_bias.reshape(N_EXPERTS, 1), (N_EXPERTS, LANES)))


def _pos_kernel(start_ref, e_ref, rank_ref, pos_ref):
    e = e_ref[...]
    acc = rank_ref[...]
    for ex in range(N_EXPERTS):
        acc = acc + jnp.where(e == ex, start_ref[ex], 0)
    pos_ref[...] = acc


def _positions(pstart, eidx, rank):
    k, t = eidx.shape
    tl = min(4096, t)
    return pl.pallas_call(
        _pos_kernel,
        grid_spec=pltpu.PrefetchScalarGridSpec(
            num_scalar_prefetch=1,
            grid=(t // tl,),
            in_specs=[pl.BlockSpec((k, tl), lambda i, s: (0, i)), pl.BlockSpec((k, tl), lambda i, s: (0, i))],
            out_specs=pl.BlockSpec((k, tl), lambda i, s: (0, i)),
        ),
        out_shape=jax.ShapeDtypeStruct((k, t), I32),
        compiler_params=_cparams(("arbitrary",), 8 * k * tl * 4),
        name="moe_positions",
    )(pstart, eidx, rank)


def _dispatch_kernel(pos_ref, hp_ref, xs_ref, sem, *, tm, per):
    def copy(t, k):
        src = hp_ref.at[pl.ds(pl.multiple_of(t * per, per), per)]
        dst = xs_ref.at[pl.ds(pl.multiple_of(pos_ref[k, t] * per, per), per)]
        return pltpu.make_async_copy(src, dst, sem)

    def start(t, c):
        for k in range(TOP_K):
            copy(t, k).start(priority=k % 2)
        return c

    def wait(t, c):
        for k in range(TOP_K):
            copy(t, k).wait()
        return c

    lax.fori_loop(0, tm, start, 0, unroll=2)
    lax.fori_loop(0, tm, wait, 0, unroll=2)


def _dispatch(pos, hp, n_rows, per):
    t = hp.shape[0] // per
    tm = min(512, t)
    kern = functools.partial(_dispatch_kernel, tm=tm, per=per)
    return pl.pallas_call(
        kern,
        grid=(t // tm,),
        in_specs=[
            pl.BlockSpec((TOP_K, tm), lambda i: (0, i), memory_space=pltpu.SMEM),
            pl.BlockSpec((tm * per, LANES), lambda i: (i, 0)),
        ],
        out_specs=pl.BlockSpec(memory_space=pl.ANY),
        out_shape=jax.ShapeDtypeStruct((n_rows * per, LANES), U32),
        scratch_shapes=[pltpu.SemaphoreType.DMA(())],
        compiler_params=pltpu.CompilerParams(dimension_semantics=("arbitrary",), has_side_effects=True,
                                             vmem_limit_bytes=_vmem_limit(4 * tm * per * LANES * 4)),
        name="moe_dispatch",
    )(pos, hp)


def _expert_kernel(blk_exp_ref, n_used_ref, xs_ref, wg_ref, wu_ref, wd_ref, ys_ref, wg_scr, wu_scr, wd_scr, *, per):
    b = pl.program_id(0)
    new_expert = (b == 0) | (blk_exp_ref[b] != blk_exp_ref[jnp.maximum(b - 1, 0)])

    @pl.when(new_expert)
    def _():
        wg_scr[...] = wg_ref[0, 0].astype(BF16)
        wu_scr[...] = wu_ref[0, 0].astype(BF16)
        wd_scr[...] = wd_ref[0, 0].astype(BF16)

    @pl.when(b < n_used_ref[0])
    def _():
        lo, hi = _unpack_bf16_pairs(_load_token_tiles(xs_ref, 0, MOE_BLK, per))
        half = lo.shape[1]
        gate = (jnp.dot(lo, wg_scr[:half], preferred_element_type=F32)
                + jnp.dot(hi, wg_scr[half:], preferred_element_type=F32))
        up = (jnp.dot(lo, wu_scr[:half], preferred_element_type=F32)
              + jnp.dot(hi, wu_scr[half:], preferred_element_type=F32))
        hidden = (_silu(gate) * up).astype(BF16)
        _store_token_tiles(ys_ref, 0, _pack_bf16_pairs(jnp.dot(hidden, wd_scr[...], preferred_element_type=F32)))

    @pl.when(b >= n_used_ref[0])
    def _():
        ys_ref[...] = jnp.zeros_like(ys_ref)


def _experts(blk_exp, n_used, xs, wg, wu, wd, layer, per):
    half = per * LANES
    d = 2 * half
    de = wg.shape[3]
    nb = xs.shape[0] // (MOE_BLK * per)
    est = 4 * MOE_BLK * half * 4 + 2 * 3 * d * de * 4 + 3 * d * de * 2 + MOE_BLK * (2 * de + d) * 4
    return pl.pallas_call(
        functools.partial(_expert_kernel, per=per),
        grid_spec=pltpu.PrefetchScalarGridSpec(
            num_scalar_prefetch=2,
            grid=(nb,),
            in_specs=[
                pl.BlockSpec((MOE_BLK * per, LANES), lambda b, be, nu: (b, 0)),
                pl.BlockSpec((1, 1, d, de), lambda b, be, nu: (layer, be[b], 0, 0)),
                pl.BlockSpec((1, 1, d, de), lambda b, be, nu: (layer, be[b], 0, 0)),
                pl.BlockSpec((1, 1, de, d), lambda b, be, nu: (layer, be[b], 0, 0)),
            ],
            out_specs=pl.BlockSpec((MOE_BLK * per, LANES), lambda b, be, nu: (b, 0)),
            scratch_shapes=[pltpu.VMEM((d, de), BF16), pltpu.VMEM((d, de), BF16), pltpu.VMEM((de, d), BF16)],
        ),
        out_shape=jax.ShapeDtypeStruct(xs.shape, U32),
        compiler_params=_cparams(("arbitrary",), est),
        name="moe_experts",
    )(blk_exp, n_used, xs, wg, wu, wd)


def _combine_kernel(pos_ref, x_ref, gmod_ref, h_ref, wt_ref, sg_ref, su_ref, sd_ref, ys_ref, o_ref, yg_scr, sem,
                    *, tm, per):
    def copy(t, k):
        src = ys_ref.at[pl.ds(pl.multiple_of(pos_ref[k, t] * per, per), per)]
        dst = yg_scr.at[pl.ds(pl.multiple_of((k * tm + t) * per, per), per)]
        return pltpu.make_async_copy(src, dst, sem)

    def start(t, c):
        for k in range(TOP_K):
            copy(t, k).start(priority=k % 2)
        return c

    def wait(t, c):
        for k in range(TOP_K):
            copy(t, k).wait()
        return c

    lax.fori_loop(0, tm, start, 0, unroll=2)
    h = h_ref[...]
    hidden = (_silu(jnp.dot(h, sg_ref[...], preferred_element_type=F32))
              * jnp.dot(h, su_ref[...], preferred_element_type=F32)).astype(BF16)
    shared = jnp.dot(hidden, sd_ref[...], preferred_element_type=F32)
    lax.fori_loop(0, tm, wait, 0, unroll=2)
    half = per * LANES
    acc_lo = shared[:, :half]
    acc_hi = shared[:, half:]
    wt = wt_ref[...]
    for k in range(TOP_K):
        packed = _load_token_tiles(yg_scr, k * tm, tm, per)
        wk = wt[:, k:k + 1]
        acc_lo = acc_lo + wk * lax.bitcast_convert_type(packed << 16, F32)
        acc_hi = acc_hi + wk * lax.bitcast_convert_type(packed & jnp.uint32(0xFFFF0000), F32)
    g = gmod_ref[0]
    o_ref[:, :half] = x_ref[:, :half] + g[:, :half] * acc_lo
    o_ref[:, half:] = x_ref[:, half:] + g[:, half:] * acc_hi


def _combine(pos, x2, seq, gmod, h, wts_t, sg, su, sd, ys, per):
    t, d = x2.shape
    half = d // 2
    ds_ = sg.shape[1]
    tm = min(256, seq)
    tiles_per_batch = seq // tm
    kern = functools.partial(_combine_kernel, tm=tm, per=per)
    const = lambda shape: pl.BlockSpec(shape, lambda i: (0,) * len(shape))
    est = 4 * tm * d * 4 + 2 * tm * d * 2 + 2 * 3 * d * ds_ * 2 + TOP_K * tm * half * 4 + 8 * tm * d * 4
    return pl.pallas_call(
        kern,
        grid=(t // tm,),
        in_specs=[
            pl.BlockSpec((TOP_K, tm), lambda i: (0, i), memory_space=pltpu.SMEM),
            pl.BlockSpec((tm, d), lambda i: (i, 0)),
            pl.BlockSpec((1, 1, d), lambda i: (i // tiles_per_batch, 0, 0)),
            pl.BlockSpec((tm, d), lambda i: (i, 0)),
            pl.BlockSpec((tm, TOP_K), lambda i: (i, 0)),
            const((d, ds_)), const((d, ds_)), const((ds_, d)),
            pl.BlockSpec(memory_space=pl.ANY),
        ],
        out_specs=pl.BlockSpec((tm, d), lambda i: (i, 0)),
        out_shape=jax.ShapeDtypeStruct((t, d), F32),
        scratch_shapes=[pltpu.VMEM((TOP_K * tm * per, LANES), U32), pltpu.SemaphoreType.DMA(())],
        compiler_params=_cparams(("arbitrary",), est),
        name="moe_combine",
    )(pos, x2, gmod, h, wts_t, sg, su, sd, ys)


def _final_kernel(x_ref, g_ref, o_ref):
    x = x_ref[...]
    o_ref[...] = x * lax.rsqrt(jnp.mean(x * x, axis=-1, keepdims=True) + EPS) * g_ref[...]


def _final_norm(x2, g):
    t, d = x2.shape
    tm = min(1024, t)
    return pl.pallas_call(
        _final_kernel,
        grid=(t // tm,),
        in_specs=[pl.BlockSpec((tm, d), lambda i: (i, 0)), pl.BlockSpec((1, d), lambda i: (0, 0))],
        out_specs=pl.BlockSpec((tm, d), lambda i: (i, 0)),
        out_shape=jax.ShapeDtypeStruct((t, d), F32),
        compiler_params=_cparams(("arbitrary",), 4 * tm * d * 4),
        name="final_norm",
    )(x2, g.reshape(1, d))


def _moe_layout(counts, n_blocks):
    padded = (counts + MOE_BLK - 1) // MOE_BLK * MOE_BLK
    pends = jnp.cumsum(padded)
    pstart = (pends - padded).astype(I32)
    blk_row = jnp.arange(n_blocks, dtype=I32)[:, None] * MOE_BLK
    blk_exp = jnp.minimum(jnp.sum((pends[None, :] <= blk_row).astype(I32), axis=1), N_EXPERTS - 1)
    n_used = (pends[-1:] // MOE_BLK).astype(I32)
    return pstart, blk_exp, n_used


def _layer(x2, bsz, seq, layer, mod, norm1_g, w_in, lb, onorm_g, conv_w, rel_bias, w_a, w_b, w_c, w_out, norm2_g,
           router_w, router_bias, wg, wu, wd, sg, su, sd):
    t, d = x2.shape
    sh1, sc1, g1, sh2, sc2, g2 = [m.reshape(bsz, 1, d) for m in jnp.split(mod, N_MOD, axis=-1)]
    n_col_blocks = w_in.shape[2] // A_DIM
    n_mix_blocks = n_col_blocks - (3 * d) // A_DIM
    h1 = _norm_mod(x2, seq, norm1_g, sc1, sh1)
    pg = _project(h1, w_in, layer, tuple(range(n_mix_blocks, n_col_blocks)), "sigmoid", BF16, "proj_gates")
    af = _project(h1, w_in, layer, (1,), "none", F32, "proj_forget")
    p = _project(h1, w_in, layer, (4, 5, 0, 2, 3, 6, 7, 8), "none", BF16, "proj_mixers")
    b_blk = 0
    aq_blk = (3 * B_WIDTH) // A_DIM
    q_blk = aq_blk + 3
    a_feat = _hgrn(p, af, jnp.log(lb), jnp.log1p(-lb), onorm_g, bsz, seq, aq_blk)
    c_feat = _attention(p, rel_bias, bsz, seq, q_blk)
    x2 = _merge(x2, seq, g1, a_feat, c_feat, p, pg, conv_w, w_a.astype(BF16), w_b.astype(BF16), w_c.astype(BF16),
                w_out.astype(BF16), b_blk)

    h, hp, eidx, wts, rank, cnt = _router(x2, seq, norm2_g, sc2, sh2, router_w.T, router_bias)
    n_blocks = (t * TOP_K) // MOE_BLK + N_EXPERTS
    pstart, blk_exp, n_used = _moe_layout(cnt[:, 0], n_blocks)
    pos = _positions(pstart, eidx, rank)
    per = d // (2 * LANES)
    xs = _dispatch(pos, hp, n_blocks * MOE_BLK, per)
    ys = _experts(blk_exp, n_used, xs, wg, wu, wd, layer, per)
    return _combine(pos, x2, seq, g2, h, wts.T, sg.astype(BF16), su.astype(BF16), sd.astype(BF16), ys, per)


def kernel(x, c, ada_w, ada_b, norm1_g, w_in, hgrn_lb_logits, hgrn_onorm_g, conv_w, rel_bias, w_branch_a, w_branch_b, w_branch_c, w_out, norm2_g, router_w, router_bias, moe_w_gate, moe_w_up, moe_w_down, shared_w_gate, shared_w_up, shared_w_down, final_g):
    bsz, seq, d = x.shape
    depth = ada_w.shape[0]
    mod = _modulation(c, ada_w, ada_b)
    lbs = jnp.cumsum(jax.nn.softmax(hgrn_lb_logits.astype(F32), axis=0), axis=0)
    lbs = lbs - lbs[0]
    x2 = x.reshape(bsz * seq, d)
    for l in range(depth):
        x2 = _layer(x2, bsz, seq, l, mod[l], norm1_g[l], w_in, lbs[l], hgrn_onorm_g[l], conv_w[l], rel_bias[l],
                    w_branch_a[l], w_branch_b[l], w_branch_c[l], w_out[l], norm2_g[l], router_w[l], router_bias[l],
                    moe_w_gate, moe_w_up, moe_w_down, shared_w_gate[l], shared_w_up[l], shared_w_down[l])
    return _final_norm(x2, final_g).reshape(bsz, seq, d)
```

```python
import functools

import numpy as np
import jax
import jax.numpy as jnp
from jax import lax
from jax.experimental import pallas as pl
from jax.experimental.pallas import tpu as pltpu

F32 = jnp.float32
BF16 = jnp.bfloat16
I32 = jnp.int32
U32 = jnp.uint32

CHUNK = 64
EPS = 1e-6
A_HEADS = 6
A_HEAD_DIM = 128
A_DIM = A_HEADS * A_HEAD_DIM
B_WIDTH = 512
CONV_W = 3
C_HEADS = 6
C_HEAD_DIM = 128
C_DIM = C_HEADS * C_HEAD_DIM
LEFT_CHUNKS = 8
REL_CLIP = 256
N_EXPERTS = 64
TOP_K = 8
N_GROUPS = 8
GROUP_SIZE = N_EXPERTS // N_GROUPS
TOPK_GROUPS = 4
ROUTED_SCALE = 2.5
N_MOD = 6

V7X_VMEM_BYTES = 64 * 1024 * 1024
LANES = 128
NEG_BIG = -1e30

SUB = 16
FACTOR_SAFE_LOG = 80.0
MID = CHUNK // 2
MOE_BLK = 512
ATT_TQ = 256


def _vmem_limit(nbytes):
    return int(min(max(nbytes * 3 // 2, 16 * 1024 * 1024), V7X_VMEM_BYTES - 8 * 1024 * 1024))


def _cparams(sem, nbytes):
    return pltpu.CompilerParams(dimension_semantics=sem, vmem_limit_bytes=_vmem_limit(nbytes))


def _sigmoid(x):
    return 1.0 / (1.0 + jnp.exp(-x))


def _silu(x):
    return x * _sigmoid(x)


def _split3(x):
    hi = x.astype(BF16)
    r1 = x - hi.astype(F32)
    mid = r1.astype(BF16)
    lo = (r1 - mid.astype(F32)).astype(BF16)
    return hi, mid, lo


def _pack_bf16_pairs(x):
    n = x.shape[1] // 2
    xb = x.astype(BF16).astype(F32)
    lo = lax.bitcast_convert_type(xb[:, :n], U32) >> 16
    hi = lax.bitcast_convert_type(xb[:, n:], U32) & jnp.uint32(0xFFFF0000)
    return lo | hi


def _unpack_bf16_pairs(p):
    lo = lax.bitcast_convert_type(p << 16, F32).astype(BF16)
    hi = lax.bitcast_convert_type(p & jnp.uint32(0xFFFF0000), F32).astype(BF16)
    return lo, hi


def _store_token_tiles(ref, first_token, mat):
    n, w = mat.shape
    per = w // LANES
    for s in range(per):
        ref[pl.ds(first_token * per + s, n, stride=per), :] = mat[:, s * LANES:(s + 1) * LANES]


def _load_token_tiles(ref, first_token, n, per):
    return jnp.concatenate([ref[pl.ds(first_token * per + s, n, stride=per), :] for s in range(per)], axis=1)


def _mod_kernel(c_ref, w_ref, b_ref, o_ref):
    c = c_ref[...]
    o_ref[0] = jnp.dot(_silu(c), w_ref[0], preferred_element_type=F32) + b_ref[0]


def _modulation(c, ada_w, ada_b):
    depth, d, n = ada_w.shape
    bsz = c.shape[0]
    rows = 8
    c_pad = jnp.zeros((rows, d), F32).at[:bsz].set(c)
    tn = 1536 if n % 1536 == 0 else n
    out = pl.pallas_call(
        _mod_kernel,
        grid=(depth, n // tn),
        in_specs=[
            pl.BlockSpec((rows, d), lambda l, j: (0, 0)),
            pl.BlockSpec((1, d, tn), lambda l, j: (l, 0, j)),
            pl.BlockSpec((1, 1, tn), lambda l, j: (l, 0, j)),
        ],
        out_specs=pl.BlockSpec((1, rows, tn), lambda l, j: (l, 0, j)),
        out_shape=jax.ShapeDtypeStruct((depth, rows, n), F32),
        compiler_params=_cparams(("arbitrary", "arbitrary"), 2 * d * tn * 4 + 4 * rows * tn * 4),
        name="adaln_mod",
    )(c_pad, ada_w, ada_b.reshape(depth, 1, n))
    return out[:, :bsz]


def _norm_kernel(x_ref, g_ref, sc_ref, sh_ref, h_ref):
    x = x_ref[...]
    y = x * lax.rsqrt(jnp.mean(x * x, axis=-1, keepdims=True) + EPS)
    h_ref[...] = (y * g_ref[...] * (1.0 + sc_ref[0]) + sh_ref[0]).astype(BF16)


def _norm_mod(x2, seq, g, sc, sh):
    t, d = x2.shape
    tm = min(512, seq)
    tiles_per_batch = seq // tm
    return pl.pallas_call(
        _norm_kernel,
        grid=(t // tm,),
        in_specs=[
            pl.BlockSpec((tm, d), lambda i: (i, 0)),
            pl.BlockSpec((1, d), lambda i: (0, 0)),
            pl.BlockSpec((1, 1, d), lambda i: (i // tiles_per_batch, 0, 0)),
            pl.BlockSpec((1, 1, d), lambda i: (i // tiles_per_batch, 0, 0)),
        ],
        out_specs=pl.BlockSpec((tm, d), lambda i: (i, 0)),
        out_shape=jax.ShapeDtypeStruct((t, d), BF16),
        compiler_params=_cparams(("arbitrary",), 6 * tm * d * 4),
        name="norm_mod",
    )(x2, g.reshape(1, d), sc, sh)


def _proj_kernel(cmap_ref, h_ref, w_ref, o_ref, w_scr, *, act):
    @pl.when(pl.program_id(1) == 0)
    def _():
        w_scr[...] = w_ref[0].astype(BF16)

    acc = jnp.dot(h_ref[...], w_scr[...], preferred_element_type=F32)
    if act == "sigmoid":
        acc = _sigmoid(acc)
    o_ref[...] = acc.astype(o_ref.dtype)


def _project(h, w_all, layer, col_blocks, act, out_dtype, name):
    t, d = h.shape
    tn = A_DIM
    nj = len(col_blocks)
    tm = min(1024, t)
    osz = jnp.dtype(out_dtype).itemsize
    est = 2 * tm * d * 2 + 2 * d * tn * 4 + d * tn * 2 + 2 * tm * tn * osz + 2 * tm * tn * 4
    return pl.pallas_call(
        functools.partial(_proj_kernel, act=act),
        grid_spec=pltpu.PrefetchScalarGridSpec(
            num_scalar_prefetch=1,
            grid=(nj, t // tm),
            in_specs=[
                pl.BlockSpec((tm, d), lambda j, i, cm: (i, 0)),
                pl.BlockSpec((1, d, tn), lambda j, i, cm: (layer, 0, cm[j])),
            ],
            out_specs=pl.BlockSpec((tm, tn), lambda j, i, cm: (i, j)),
            scratch_shapes=[pltpu.VMEM((d, tn), BF16)],
        ),
        out_shape=jax.ShapeDtypeStruct((t, nj * tn), out_dtype),
        compiler_params=_cparams(("arbitrary", "arbitrary"), est),
        name=name,
    )(jnp.asarray(col_blocks, I32), h, w_all)


def _hgrn_kernel(aq_ref, af_ref, ai_ref, ag_ref, loglb_ref, log1mlb_ref, og_ref, sel_ref, o_ref,
                 st_scr, lf_scr, q_scr, k_scr, v_scr, b_scr, p_scr, *, n_chunks):
    @pl.when(pl.program_id(1) == 0)
    def _():
        st_scr[...] = jnp.zeros_like(st_scr)

    hd = A_HEAD_DIM
    n_sub = CHUNK // SUB
    nt = (((1,), (1,)), ((), ()))
    row_c = lax.broadcasted_iota(I32, (CHUNK, CHUNK), 0)
    col_c = lax.broadcasted_iota(I32, (CHUNK, CHUNK), 1)
    tri = (col_c <= row_c).astype(BF16)
    sub_row = lax.broadcasted_iota(I32, (SUB, 1), 0)
    chunk_row = lax.broadcasted_iota(I32, (CHUNK, 1), 0)
    log_lb = loglb_ref[...]
    log1m_lb = log1mlb_ref[...]
    og = og_ref[...]

    def log_f_body(ci, lowest):
        rows = pl.ds(pl.multiple_of(ci * CHUNK, CHUNK), CHUNK)
        af = af_ref[rows, :]
        ls = jnp.minimum(af, 0.0) - jnp.log(1.0 + jnp.exp(-jnp.abs(af)))
        t2 = log1m_lb + ls
        lf = jnp.maximum(log_lb, t2) + jnp.log(1.0 + jnp.exp(-jnp.abs(log_lb - t2)))
        lf_scr[rows, :] = lf
        first = jnp.sum(jnp.where(chunk_row <= MID, lf, 0.0), axis=0, keepdims=True)
        second = jnp.sum(jnp.where(chunk_row >= MID, lf, 0.0), axis=0, keepdims=True)
        return jnp.minimum(lowest, jnp.minimum(first, second))

    lowest_sum = lax.fori_loop(0, n_chunks, log_f_body, jnp.zeros((1, A_DIM), F32), unroll=2)
    safe = jnp.min(lowest_sum) > -FACTOR_SAFE_LOG

    def prep(ci):
        r0 = pl.multiple_of(ci * CHUNK, CHUNK)
        rows = pl.ds(r0, CHUNK)
        lf = lf_scr[rows, :]
        hi, mid, lo = _split3(lf)
        b = (jnp.dot(tri, hi, preferred_element_type=F32) + jnp.dot(tri, mid, preferred_element_type=F32)
             + jnp.dot(tri, lo, preferred_element_type=F32))
        q = aq_ref[rows, :].astype(F32) * (A_HEAD_DIM ** -0.5)
        return rows, 1.0 - jnp.exp(lf), b, q, ai_ref[rows, :]

    def head_tail(h, o, kh, vh, bh, bl):
        ktil = (kh * jnp.exp(bl - bh)).astype(BF16)
        upd = lax.dot_general(vh, ktil, (((0,), (0,)), ((), ())), preferred_element_type=F32)
        st_scr[h] = st_scr[h] * jnp.exp(bl) + upd
        return o * lax.rsqrt(jnp.mean(o * o, axis=-1, keepdims=True) + EPS)

    def finish(rows, outs):
        o_all = jnp.concatenate(outs, axis=1)
        gate = ag_ref[rows, :].astype(F32)
        o_ref[rows, :] = (o_all * og * _silu(gate)).astype(BF16)

    def fast_chunk(ci, carry):
        rows, k_all, b_all, q_all, v_all = prep(ci)
        qt_all = (q_all * jnp.exp(b_all)).astype(BF16)
        b_mid = b_all[MID:MID + 1]
        qm_all = (q_all * jnp.exp(b_all - b_mid)).astype(BF16)
        km_all = (k_all * jnp.exp(b_mid - b_all)).astype(BF16)
        b_last = b_all[CHUNK - 1:CHUNK]
        outs = []
        for h in range(A_HEADS):
            cols = slice(h * hd, (h + 1) * hd)
            vh = v_all[:, cols]
            qt = qt_all[:, cols]
            a = lax.dot_general(qm_all[:, cols], km_all[:, cols], nt, preferred_element_type=F32)
            a = jnp.where(col_c <= row_c, a, 0.0).astype(BF16)
            o = (jnp.dot(a, vh, preferred_element_type=F32)
                 + lax.dot_general(qt, st_scr[h].astype(BF16), nt, preferred_element_type=F32))
            outs.append(head_tail(h, o, k_all[:, cols], vh, b_all[:, cols], b_last[:, cols]))
        finish(rows, outs)
        return carry

    def exact_chunk(ci, carry):
        rows, k_new, b_new, q_new, v_new = prep(ci)
        k_scr[...] = k_new
        b_scr[...] = b_new
        q_scr[...] = q_new
        v_scr[...] = v_new.astype(F32)

        for i in range(n_sub):
            s0 = i * SUB
            b_blk = b_scr[pl.ds(s0, SUB), :]
            q_blk = q_scr[pl.ds(s0, SUB), :]
            for s in range(SUB):
                b_s = b_scr[pl.ds(s0 + s, 1), :]
                k_s = k_scr[pl.ds(s0 + s, 1), :]
                e = jnp.exp(jnp.where(sub_row >= s, b_blk - b_s, NEG_BIG))
                pv = (q_blk * k_s * e).astype(BF16)
                for h in range(A_HEADS):
                    p_scr[pl.ds((i * A_HEADS + h) * SUB, SUB), s * hd:(s + 1) * hd] = pv[:, h * hd:(h + 1) * hd]
        diag = jnp.dot(p_scr[...], sel_ref[...], preferred_element_type=F32)

        b_all = b_scr[...]
        q_all = q_scr[...]
        k_all = k_scr[...]
        qhat, khat = [], []
        for i in range(1, n_sub):
            s0 = i * SUB
            beta = b_scr[pl.ds(s0 - 1, 1), :]
            qhat.append((q_all[s0:s0 + SUB] * jnp.exp(b_all[s0:s0 + SUB] - beta)).astype(BF16))
            khat.append(jnp.where(chunk_row < s0, k_all * jnp.exp(jnp.minimum(beta - b_all, 0.0)), 0.0).astype(BF16))
        zero_blk = jnp.zeros((SUB, hd), BF16)

        b_last = b_scr[pl.ds(CHUNK - 1, 1), :]
        outs = []
        for h in range(A_HEADS):
            cols = slice(h * hd, (h + 1) * hd)
            qh = q_all[:, cols]
            kh = k_all[:, cols]
            vh = v_scr[:, cols].astype(BF16)
            bh = b_all[:, cols]
            st = st_scr[h]
            lhs_rows = [jnp.concatenate([zero_blk] * (n_sub - 1), axis=1)]
            for i in range(1, n_sub):
                parts = [zero_blk] * (n_sub - 1)
                parts[i - 1] = qhat[i - 1][:, cols]
                lhs_rows.append(jnp.concatenate(parts, axis=1))
            lhs = jnp.concatenate(lhs_rows, axis=0)
            rhs = jnp.concatenate([kk[:, cols] for kk in khat], axis=1)
            a_off = lax.dot_general(lhs, rhs, nt, preferred_element_type=F32)
            a_diag = jnp.concatenate(
                [pltpu.roll(diag[(i * A_HEADS + h) * SUB:(i * A_HEADS + h + 1) * SUB], i * SUB, 1) if i else
                 diag[h * SUB:(h + 1) * SUB] for i in range(n_sub)], axis=0)
            a = (a_diag[:, :CHUNK] + a_off).astype(BF16)
            qt = (qh * jnp.exp(bh)).astype(BF16)
            o = (jnp.dot(a, vh, preferred_element_type=F32)
                 + lax.dot_general(qt, st.astype(BF16), nt, preferred_element_type=F32))
            outs.append(head_tail(h, o, kh, vh, bh, b_last[:, cols]))
        finish(rows, outs)
        return carry

    @pl.when(safe)
    def _():
        lax.fori_loop(0, n_chunks, fast_chunk, 0, unroll=4)

    @pl.when(jnp.logical_not(safe))
    def _():
        lax.fori_loop(0, n_chunks, exact_chunk, 0)


def _hgrn(p, af, log_lb, log1m_lb, onorm_g, bsz, seq, aq_blk):
    t = p.shape[0]
    ts = min(512, seq)
    n_seq_tiles = seq // ts
    n_chunks = ts // CHUNK
    kern = functools.partial(_hgrn_kernel, n_chunks=n_chunks)
    row = lambda b, i: b * n_seq_tiles + i
    est = (2 * 3 * ts * A_DIM * 2 + 2 * ts * A_DIM * 4 + 2 * ts * A_DIM * 2 + A_HEADS * 128 * 128 * 4
           + 4 * CHUNK * A_DIM * 4 + CHUNK * SUB * A_DIM * 6 + 3 * ts * A_DIM * 4)
    sel_rows = np.arange(SUB * A_HEAD_DIM)
    selector = jnp.asarray(sel_rows[:, None] // A_HEAD_DIM == np.arange(A_HEAD_DIM)[None, :], BF16)
    return pl.pallas_call(
        kern,
        grid=(bsz, n_seq_tiles),
        in_specs=[
            pl.BlockSpec((ts, A_DIM), lambda b, i: (row(b, i), aq_blk)),
            pl.BlockSpec((ts, A_DIM), lambda b, i: (row(b, i), 0)),
            pl.BlockSpec((ts, A_DIM), lambda b, i: (row(b, i), aq_blk + 1)),
            pl.BlockSpec((ts, A_DIM), lambda b, i: (row(b, i), aq_blk + 2)),
            pl.BlockSpec((1, A_DIM), lambda b, i: (0, 0)),
            pl.BlockSpec((1, A_DIM), lambda b, i: (0, 0)),
            pl.BlockSpec((1, A_DIM), lambda b, i: (0, 0)),
            pl.BlockSpec((SUB * A_HEAD_DIM, A_HEAD_DIM), lambda b, i: (0, 0)),
        ],
        out_specs=pl.BlockSpec((ts, A_DIM), lambda b, i: (row(b, i), 0)),
        out_shape=jax.ShapeDtypeStruct((t, A_DIM), BF16),
        scratch_shapes=[
            pltpu.VMEM((A_HEADS, A_HEAD_DIM, A_HEAD_DIM), F32),
            pltpu.VMEM((ts, A_DIM), F32),
            pltpu.VMEM((CHUNK, A_DIM), F32),
            pltpu.VMEM((CHUNK, A_DIM), F32),
            pltpu.VMEM((CHUNK, A_DIM), F32),
            pltpu.VMEM((CHUNK, A_DIM), F32),
            pltpu.VMEM((CHUNK * A_HEADS, SUB * A_HEAD_DIM), BF16),
        ],
        compiler_params=_cparams(("arbitrary", "arbitrary"), est),
        name="hgrn2_scan",
    )(p, af, p, p, log_lb.reshape(1, A_DIM), log1m_lb.reshape(1, A_DIM), onorm_g.reshape(1, A_DIM), selector)


def _attn_kernel(q_ref, k0_ref, k1_ref, k2_ref, v0_ref, v1_ref, v2_ref, bias_ref, o_ref):
    i = pl.program_id(1)
    tq = q_ref.shape[0]
    hd = C_HEAD_DIM
    scale = C_HEAD_DIM ** -0.5
    col = lax.broadcasted_iota(I32, (1, 3 * tq), 1)
    valid = (col >= 2 * tq) | ((col >= tq) & (i >= 1)) | (i >= 2)
    for h in range(C_HEADS):
        cols = slice(h * hd, (h + 1) * hd)
        qh = q_ref[:, cols]
        kh = jnp.concatenate([k0_ref[:, cols], k1_ref[:, cols], k2_ref[:, cols]], axis=0)
        vh = jnp.concatenate([v0_ref[:, cols], v1_ref[:, cols], v2_ref[:, cols]], axis=0)
        s = lax.dot_general(qh, kh, (((1,), (1,)), ((), ())), preferred_element_type=F32) * scale + bias_ref[h]
        s = jnp.where(valid, s, NEG_BIG)
        m = jnp.max(s, axis=-1, keepdims=True)
        e = jnp.exp(s - m)
        pr = e / jnp.sum(e, axis=-1, keepdims=True)
        o_ref[:, cols] = jnp.dot(pr.astype(BF16), vh, preferred_element_type=F32).astype(BF16)


def _attention_bias_table(rel_bias, tq):
    band = (LEFT_CHUNKS + 1) * CHUNK
    assert tq * 2 == LEFT_CHUNKS * CHUNK, "three key slots of tq rows must cover the band"
    diff = np.arange(CHUNK + band - 1) - (band - 1)
    ext = rel_bias.astype(F32)[:, np.clip(diff + LEFT_CHUNKS * CHUNK, -REL_CLIP, REL_CLIP) + REL_CLIP]
    rev = ext[:, ::-1]
    chunk_tab = jnp.stack([rev[:, CHUNK - 1 - tt:CHUNK - 1 - tt + band] for tt in range(CHUNK)], axis=1)
    rows = [jnp.pad(chunk_tab, ((0, 0), (0, 0), (c * CHUNK, 3 * tq - band - c * CHUNK)), constant_values=NEG_BIG)
            for c in range(tq // CHUNK)]
    return jnp.concatenate(rows, axis=1)


def _attention(p, rel_bias, bsz, seq, q_blk):
    t = p.shape[0]
    tq = ATT_TQ
    nq = seq // tq
    bias = _attention_bias_table(rel_bias, tq)
    row = lambda b, i, back: b * nq + jnp.maximum(i - back, 0)
    kv_spec = lambda blk, back: pl.BlockSpec((tq, C_DIM), lambda b, i: (row(b, i, back), blk))
    est = 2 * 7 * tq * C_DIM * 2 + 2 * C_HEADS * tq * 3 * tq * 4 + 2 * tq * C_DIM * 2 + 6 * tq * 3 * tq * 4
    return pl.pallas_call(
        _attn_kernel,
        grid=(bsz, nq),
        in_specs=[
            pl.BlockSpec((tq, C_DIM), lambda b, i: (b * nq + i, q_blk)),
            kv_spec(q_blk + 1, 2), kv_spec(q_blk + 1, 1), kv_spec(q_blk + 1, 0),
            kv_spec(q_blk + 2, 2), kv_spec(q_blk + 2, 1), kv_spec(q_blk + 2, 0),
            pl.BlockSpec((C_HEADS, tq, 3 * tq), lambda b, i: (0, 0, 0)),
        ],
        out_specs=pl.BlockSpec((tq, C_DIM), lambda b, i: (b * nq + i, 0)),
        out_shape=jax.ShapeDtypeStruct((t, C_DIM), BF16),
        compiler_params=_cparams(("arbitrary", "arbitrary"), est),
        name="chunk_attention",
    )(p, p, p, p, p, p, p, bias)


def _merge_kernel(x_ref, gmod_ref, a_ref, c_ref, bb_ref, bc_ref, bu_ref, hbc_ref, hbu_ref, cw_ref,
                  ga_ref, gb_ref, gc_ref, wa_ref, wb_ref, wc_ref, wo_ref, o_ref, *, tiles_per_batch):
    i = pl.program_id(0)
    tm = x_ref.shape[0]
    u = bc_ref[...].astype(F32) * bu_ref[...].astype(F32)
    halo = hbc_ref[...].astype(F32) * hbu_ref[...].astype(F32)
    halo = jnp.where(i % tiles_per_batch == 0, 0.0, halo)
    hrows = halo.shape[0]
    hm1 = halo[hrows - 1:hrows]
    hm2 = halo[hrows - 2:hrows - 1]
    r = lax.broadcasted_iota(I32, (tm, 1), 0)
    u1 = jnp.where(r == 0, hm1, pltpu.roll(u, 1, 0))
    u2 = jnp.where(r == 0, hm2, jnp.where(r == 1, hm1, pltpu.roll(u, 2, 0)))
    cw = cw_ref[...]
    conv = u2 * cw[0:1] + u1 * cw[1:2] + u * cw[2:3]
    bfeat = (bb_ref[...].astype(F32) * conv).astype(BF16)
    ya = jnp.dot(a_ref[...], wa_ref[...], preferred_element_type=F32)
    yb = jnp.dot(bfeat, wb_ref[...], preferred_element_type=F32)
    yc = jnp.dot(c_ref[...], wc_ref[...], preferred_element_type=F32)
    merged = (ga_ref[...].astype(F32) * ya + gb_ref[...].astype(F32) * yb + gc_ref[...].astype(F32) * yc)
    y = jnp.dot(merged.astype(BF16), wo_ref[...], preferred_element_type=F32)
    o_ref[...] = x_ref[...] + gmod_ref[0] * y


def _merge(x2, seq, gmod, a_feat, c_feat, p, pg, conv_w, wa, wb, wc, wo, b_blk):
    t, d = x2.shape
    tm = min(256, seq)
    tiles_per_batch = seq // tm
    halo = 16
    hb = tm // halo
    kern = functools.partial(_merge_kernel, tiles_per_batch=tiles_per_batch)
    const = lambda shape: pl.BlockSpec(shape, lambda i: (0,) * len(shape))
    halo_spec = lambda blk: pl.BlockSpec((halo, B_WIDTH), lambda i: (jnp.maximum(i * hb - 1, 0), blk))
    est = (4 * tm * d * 4 + 4 * tm * A_DIM * 2 + 6 * tm * B_WIDTH * 2 + 6 * tm * d * 2
           + 2 * (A_DIM + B_WIDTH + C_DIM + d) * d * 2 + 6 * tm * d * 4)
    return pl.pallas_call(
        kern,
        grid=(t // tm,),
        in_specs=[
            pl.BlockSpec((tm, d), lambda i: (i, 0)),
            pl.BlockSpec((1, 1, d), lambda i: (i // tiles_per_batch, 0, 0)),
            pl.BlockSpec((tm, A_DIM), lambda i: (i, 0)),
            pl.BlockSpec((tm, C_DIM), lambda i: (i, 0)),
            pl.BlockSpec((tm, B_WIDTH), lambda i: (i, b_blk)),
            pl.BlockSpec((tm, B_WIDTH), lambda i: (i, b_blk + 1)),
            pl.BlockSpec((tm, B_WIDTH), lambda i: (i, b_blk + 2)),
            halo_spec(b_blk + 1),
            halo_spec(b_blk + 2),
            const((CONV_W, B_WIDTH)),
            pl.BlockSpec((tm, d), lambda i: (i, 0)),
            pl.BlockSpec((tm, d), lambda i: (i, 1)),
            pl.BlockSpec((tm, d), lambda i: (i, 2)),
            const((A_DIM, d)), const((B_WIDTH, d)), const((C_DIM, d)), const((d, d)),
        ],
        out_specs=pl.BlockSpec((tm, d), lambda i: (i, 0)),
        out_shape=jax.ShapeDtypeStruct((t, d), F32),
        compiler_params=_cparams(("arbitrary",), est),
        name="merge_outproj",
    )(x2, gmod, a_feat, c_feat, p, p, p, p, p, conv_w, pg, pg, pg, wa, wb, wc, wo)


def _first_max(vals, iota, n, axis):
    m = jnp.max(vals, axis=axis, keepdims=True)
    idx = jnp.min(jnp.where(vals == m, iota, n), axis=axis, keepdims=True)
    return m, idx


def _router_kernel(x_ref, g_ref, sc_ref, sh_ref, rwt_ref, rb_ref, h_ref, hp_ref, e_ref, w_ref, rank_ref, cnt_ref,
                   cnt_scr):
    step = pl.program_id(0)

    @pl.when(step == 0)
    def _():
        cnt_scr[...] = jnp.zeros_like(cnt_scr)

    tm = x_ref.shape[0]
    x = x_ref[...]
    y = x * lax.rsqrt(jnp.mean(x * x, axis=-1, keepdims=True) + EPS)
    h = y * g_ref[...] * (1.0 + sc_ref[0]) + sh_ref[0]
    h_ref[...] = h.astype(BF16)
    _store_token_tiles(hp_ref, 0, _pack_bf16_pairs(h))

    nt = (((1,), (1,)), ((), ()))
    h_hi, h_mid, _ = _split3(h)
    w_hi, w_mid, _ = _split3(rwt_ref[...])
    logits = (lax.dot_general(w_hi, h_hi, nt, preferred_element_type=F32)
              + lax.dot_general(w_hi, h_mid, nt, preferred_element_type=F32)
              + lax.dot_general(w_mid, h_hi, nt, preferred_element_type=F32))
    scores = _sigmoid(logits)
    sel = scores + rb_ref[:, 0:1]

    g3 = sel.reshape(N_GROUPS, GROUP_SIZE, tm)
    j3 = lax.broadcasted_iota(I32, g3.shape, 1)
    m1, i1 = _first_max(g3, j3, GROUP_SIZE, 1)
    m2 = jnp.max(jnp.where(j3 == i1, -jnp.inf, g3), axis=1, keepdims=True)
    gs = (m1 + m2).reshape(N_GROUPS, tm)
    gi = lax.broadcasted_iota(I32, gs.shape, 0)
    gmask = jnp.zeros(gs.shape, jnp.bool_)
    for _ in range(TOPK_GROUPS):
        _, idx = _first_max(gs, gi, N_GROUPS, 0)
        hit = gi == idx
        gmask = gmask | hit
        gs = jnp.where(hit, -jnp.inf, gs)
    emask = jnp.broadcast_to(gmask.reshape(N_GROUPS, 1, tm), (N_GROUPS, GROUP_SIZE, tm)).reshape(N_EXPERTS, tm)
    masked = jnp.where(emask, sel, -jnp.inf)

    ei = lax.broadcasted_iota(I32, (N_EXPERTS, tm), 0)
    chosen = jnp.zeros((N_EXPERTS, tm), F32)
    e_rows, w_rows = [], []
    for _ in range(TOP_K):
        _, idx = _first_max(masked, ei, N_EXPERTS, 0)
        hit = ei == idx
        e_rows.append(idx)
        w_rows.append(jnp.sum(jnp.where(hit, scores, 0.0), axis=0, keepdims=True))
        chosen = jnp.where(hit, 1.0, chosen)
        masked = jnp.where(hit, -jnp.inf, masked)
    eidx = jnp.concatenate(e_rows, axis=0)
    wts = jnp.concatenate(w_rows, axis=0)
    wts = wts / jnp.sum(wts, axis=0, keepdims=True) * ROUTED_SCALE
    e_ref[...] = eidx
    w_ref[...] = wts

    tr = lax.broadcasted_iota(I32, (tm, tm), 0)
    tc = lax.broadcasted_iota(I32, (tm, tm), 1)
    before = (tr < tc).astype(BF16)
    base = cnt_scr[:, 0:1]
    rank_all = jnp.dot(chosen.astype(BF16), before, preferred_element_type=F32) + base
    ranks = [jnp.sum(jnp.where(ei == e_rows[k], rank_all, 0.0), axis=0, keepdims=True) for k in range(TOP_K)]
    rank_ref[...] = jnp.concatenate(ranks, axis=0).astype(I32)
    total = base + jnp.sum(chosen, axis=1, keepdims=True)
    cnt_scr[...] = jnp.broadcast_to(total, cnt_scr.shape)
    cnt_ref[...] = jnp.broadcast_to(total, cnt_ref.shape).astype(I32)


def _router(x2, seq, g, sc, sh, router_wt, router_bias):
    t, d = x2.shape
    tm = min(512, seq)
    tiles_per_batch = seq // tm
    est = (2 * tm * d * 4 + 2 * tm * d * 2 + 2 * tm * d * 2 + 2 * N_EXPERTS * d * 4 + tm * tm * 2
           + 8 * tm * d * 4 + 16 * N_EXPERTS * tm * 4)
    return pl.pallas_call(
        _router_kernel,
        grid=(t // tm,),
        in_specs=[
            pl.BlockSpec((tm, d), lambda i: (i, 0)),
            pl.BlockSpec((1, d), lambda i: (0, 0)),
            pl.BlockSpec((1, 1, d), lambda i: (i // tiles_per_batch, 0, 0)),
            pl.BlockSpec((1, 1, d), lambda i: (i // tiles_per_batch, 0, 0)),
            pl.BlockSpec((N_EXPERTS, d), lambda i: (0, 0)),
            pl.BlockSpec((N_EXPERTS, LANES), lambda i: (0, 0)),
        ],
        out_specs=[
            pl.BlockSpec((tm, d), lambda i: (i, 0)),
            pl.BlockSpec((tm * d // (2 * LANES), LANES), lambda i: (i, 0)),
            pl.BlockSpec((TOP_K, tm), lambda i: (0, i)),
            pl.BlockSpec((TOP_K, tm), lambda i: (0, i)),
            pl.BlockSpec((TOP_K, tm), lambda i: (0, i)),
            pl.BlockSpec((N_EXPERTS, LANES), lambda i: (0, 0)),
        ],
        out_shape=[
            jax.ShapeDtypeStruct((t, d), BF16),
            jax.ShapeDtypeStruct((t * d // (2 * LANES), LANES), U32),
            jax.ShapeDtypeStruct((TOP_K, t), I32),
            jax.ShapeDtypeStruct((TOP_K, t), F32),
            jax.ShapeDtypeStruct((TOP_K, t), I32),
            jax.ShapeDtypeStruct((N_EXPERTS, LANES), I32),
        ],
        scratch_shapes=[pltpu.VMEM((N_EXPERTS, LANES), F32)],
        compiler_params=_cparams(("arbitrary",), est),
        name="norm_router",
    )(x2, g.reshape(1, d), sc, sh, router_wt, jnp.broadcast_to(router_bias.reshape(N_EXPERTS, 1), (N_EXPERTS, LANES)))


def _pos_kernel(start_ref, e_ref, rank_ref, pos_ref):
    e = e_ref[...]
    acc = rank_ref[...]
    for ex in range(N_EXPERTS):
        acc = acc + jnp.where(e == ex, start_ref[ex], 0)
    pos_ref[...] = acc


def _positions(pstart, eidx, rank):
    k, t = eidx.shape
    tl = min(4096, t)
    return pl.pallas_call(
        _pos_kernel,
        grid_spec=pltpu.PrefetchScalarGridSpec(
            num_scalar_prefetch=1,
            grid=(t // tl,),
            in_specs=[pl.BlockSpec((k, tl), lambda i, s: (0, i)), pl.BlockSpec((k, tl), lambda i, s: (0, i))],
            out_specs=pl.BlockSpec((k, tl), lambda i, s: (0, i)),
        ),
        out_shape=jax.ShapeDtypeStruct((k, t), I32),
        compiler_params=_cparams(("arbitrary",), 8 * k * tl * 4),
        name="moe_positions",
    )(pstart, eidx, rank)


def _dispatch_kernel(pos_ref, hp_ref, xs_ref, sem, *, tm, per):
    def copy(t, k):
        src = hp_ref.at[pl.ds(pl.multiple_of(t * per, per), per)]
        dst = xs_ref.at[pl.ds(pl.multiple_of(pos_ref[k, t] * per, per), per)]
        return pltpu.make_async_copy(src, dst, sem)

    def start(t, c):
        for k in range(TOP_K):
            copy(t, k).start(priority=k % 2)
        return c

    def wait(t, c):
        for k in range(TOP_K):
            copy(t, k).wait()
        return c

    lax.fori_loop(0, tm, start, 0, unroll=2)
    lax.fori_loop(0, tm, wait, 0, unroll=2)


def _dispatch(pos, hp, n_rows, per):
    t = hp.shape[0] // per
    tm = min(512, t)
    kern = functools.partial(_dispatch_kernel, tm=tm, per=per)
    return pl.pallas_call(
        kern,
        grid=(t // tm,),
        in_specs=[
            pl.BlockSpec((TOP_K, tm), lambda i: (0, i), memory_space=pltpu.SMEM),
            pl.BlockSpec((tm * per, LANES), lambda i: (i, 0)),
        ],
        out_specs=pl.BlockSpec(memory_space=pl.ANY),
        out_shape=jax.ShapeDtypeStruct((n_rows * per, LANES), U32),
        scratch_shapes=[pltpu.SemaphoreType.DMA(())],
        compiler_params=pltpu.CompilerParams(dimension_semantics=("arbitrary",), has_side_effects=True,
                                             vmem_limit_bytes=_vmem_limit(4 * tm * per * LANES * 4)),
        name="moe_dispatch",
    )(pos, hp)


def _expert_kernel(blk_exp_ref, n_used_ref, xs_ref, wg_ref, wu_ref, wd_ref, ys_ref, wg_scr, wu_scr, wd_scr, *, per):
    b = pl.program_id(0)
    new_expert = (b == 0) | (blk_exp_ref[b] != blk_exp_ref[jnp.maximum(b - 1, 0)])

    @pl.when(new_expert)
    def _():
        wg_scr[...] = wg_ref[0, 0].astype(BF16)
        wu_scr[...] = wu_ref[0, 0].astype(BF16)
        wd_scr[...] = wd_ref[0, 0].astype(BF16)

    @pl.when(b < n_used_ref[0])
    def _():
        lo, hi = _unpack_bf16_pairs(_load_token_tiles(xs_ref, 0, MOE_BLK, per))
        half = lo.shape[1]
        gate = (jnp.dot(lo, wg_scr[:half], preferred_element_type=F32)
                + jnp.dot(hi, wg_scr[half:], preferred_element_type=F32))
        up = (jnp.dot(lo, wu_scr[:half], preferred_element_type=F32)
              + jnp.dot(hi, wu_scr[half:], preferred_element_type=F32))
        hidden = (_silu(gate) * up).astype(BF16)
        _store_token_tiles(ys_ref, 0, _pack_bf16_pairs(jnp.dot(hidden, wd_scr[...], preferred_element_type=F32)))

    @pl.when(b >= n_used_ref[0])
    def _():
        ys_ref[...] = jnp.zeros_like(ys_ref)


def _experts(blk_exp, n_used, xs, wg, wu, wd, layer, per):
    half = per * LANES
    d = 2 * half
    de = wg.shape[3]
    nb = xs.shape[0] // (MOE_BLK * per)
    est = 4 * MOE_BLK * half * 4 + 2 * 3 * d * de * 4 + 3 * d * de * 2 + MOE_BLK * (2 * de + d) * 4
    return pl.pallas_call(
        functools.partial(_expert_kernel, per=per),
        grid_spec=pltpu.PrefetchScalarGridSpec(
            num_scalar_prefetch=2,
            grid=(nb,),
            in_specs=[
                pl.BlockSpec((MOE_BLK * per, LANES), lambda b, be, nu: (b, 0)),
                pl.BlockSpec((1, 1, d, de), lambda b, be, nu: (layer, be[b], 0, 0)),
                pl.BlockSpec((1, 1, d, de), lambda b, be, nu: (layer, be[b], 0, 0)),
                pl.BlockSpec((1, 1, de, d), lambda b, be, nu: (layer, be[b], 0, 0)),
            ],
            out_specs=pl.BlockSpec((MOE_BLK * per, LANES), lambda b, be, nu: (b, 0)),
            scratch_shapes=[pltpu.VMEM((d, de), BF16), pltpu.VMEM((d, de), BF16), pltpu.VMEM((de, d), BF16)],
        ),
        out_shape=jax.ShapeDtypeStruct(xs.shape, U32),
        compiler_params=_cparams(("arbitrary",), est),
        name="moe_experts",
    )(blk_exp, n_used, xs, wg, wu, wd)


def _combine_kernel(pos_ref, x_ref, gmod_ref, h_ref, wt_ref, sg_ref, su_ref, sd_ref, ys_ref, o_ref, yg_scr, sem,
                    *, tm, per):
    def copy(t, k):
        src = ys_ref.at[pl.ds(pl.multiple_of(pos_ref[k, t] * per, per), per)]
        dst = yg_scr.at[pl.ds(pl.multiple_of((k * tm + t) * per, per), per)]
        return pltpu.make_async_copy(src, dst, sem)

    def start(t, c):
        for k in range(TOP_K):
            copy(t, k).start(priority=k % 2)
        return c

    def wait(t, c):
        for k in range(TOP_K):
            copy(t, k).wait()
        return c

    lax.fori_loop(0, tm, start, 0, unroll=2)
    h = h_ref[...]
    hidden = (_silu(jnp.dot(h, sg_ref[...], preferred_element_type=F32))
              * jnp.dot(h, su_ref[...], preferred_element_type=F32)).astype(BF16)
    shared = jnp.dot(hidden, sd_ref[...], preferred_element_type=F32)
    lax.fori_loop(0, tm, wait, 0, unroll=2)
    half = per * LANES
    acc_lo = shared[:, :half]
    acc_hi = shared[:, half:]
    wt = wt_ref[...]
    for k in range(TOP_K):
        packed = _load_token_tiles(yg_scr, k * tm, tm, per)
        wk = wt[:, k:k + 1]
        acc_lo = acc_lo + wk * lax.bitcast_convert_type(packed << 16, F32)
        acc_hi = acc_hi + wk * lax.bitcast_convert_type(packed & jnp.uint32(0xFFFF0000), F32)
    g = gmod_ref[0]
    o_ref[:, :half] = x_ref[:, :half] + g[:, :half] * acc_lo
    o_ref[:, half:] = x_ref[:, half:] + g[:, half:] * acc_hi


def _combine(pos, x2, seq, gmod, h, wts_t, sg, su, sd, ys, per):
    t, d = x2.shape
    half = d // 2
    ds_ = sg.shape[1]
    tm = min(256, seq)
    tiles_per_batch = seq // tm
    kern = functools.partial(_combine_kernel, tm=tm, per=per)
    const = lambda shape: pl.BlockSpec(shape, lambda i: (0,) * len(shape))
    est = 4 * tm * d * 4 + 2 * tm * d * 2 + 2 * 3 * d * ds_ * 2 + TOP_K * tm * half * 4 + 8 * tm * d * 4
    return pl.pallas_call(
        kern,
        grid=(t // tm,),
        in_specs=[
            pl.BlockSpec((TOP_K, tm), lambda i: (0, i), memory_space=pltpu.SMEM),
            pl.BlockSpec((tm, d), lambda i: (i, 0)),
            pl.BlockSpec((1, 1, d), lambda i: (i // tiles_per_batch, 0, 0)),
            pl.BlockSpec((tm, d), lambda i: (i, 0)),
            pl.BlockSpec((tm, TOP_K), lambda i: (i, 0)),
            const((d, ds_)), const((d, ds_)), const((ds_, d)),
            pl.BlockSpec(memory_space=pl.ANY),
        ],
        out_specs=pl.BlockSpec((tm, d), lambda i: (i, 0)),
        out_shape=jax.ShapeDtypeStruct((t, d), F32),
        scratch_shapes=[pltpu.VMEM((TOP_K * tm * per, LANES), U32), pltpu.SemaphoreType.DMA(())],
        compiler_params=_cparams(("arbitrary",), est),
        name="moe_combine",
    )(pos, x2, gmod, h, wts_t, sg, su, sd, ys)


def _final_kernel(x_ref, g_ref, o_ref):
    x = x_ref[...]
    o_ref[...] = x * lax.rsqrt(jnp.mean(x * x, axis=-1, keepdims=True) + EPS) * g_ref[...]


def _final_norm(x2, g):
    t, d = x2.shape
    tm = min(1024, t)
    return pl.pallas_call(
        _final_kernel,
        grid=(t // tm,),
        in_specs=[pl.BlockSpec((tm, d), lambda i: (i, 0)), pl.BlockSpec((1, d), lambda i: (0, 0))],
        out_specs=pl.BlockSpec((tm, d), lambda i: (i, 0)),
        out_shape=jax.ShapeDtypeStruct((t, d), F32),
        compiler_params=_cparams(("arbitrary",), 4 * tm * d * 4),
        name="final_norm",
    )(x2, g.reshape(1, d))


def _moe_layout(counts, n_blocks):
    padded = (counts + MOE_BLK - 1) // MOE_BLK * MOE_BLK
    pends = jnp.cumsum(padded)
    pstart = (pends - padded).astype(I32)
    blk_row = jnp.arange(n_blocks, dtype=I32)[:, None] * MOE_BLK
    blk_exp = jnp.minimum(jnp.sum((pends[None, :] <= blk_row).astype(I32), axis=1), N_EXPERTS - 1)
    n_used = (pends[-1:] // MOE_BLK).astype(I32)
    return pstart, blk_exp, n_used


def _layer(x2, bsz, seq, layer, mod, norm1_g, w_in, lb, onorm_g, conv_w, rel_bias, w_a, w_b, w_c, w_out, norm2_g,
           router_w, router_bias, wg, wu, wd, sg, su, sd):
    t, d = x2.shape
    sh1, sc1, g1, sh2, sc2, g2 = [m.reshape(bsz, 1, d) for m in jnp.split(mod, N_MOD, axis=-1)]
    n_col_blocks = w_in.shape[2] // A_DIM
    n_mix_blocks = n_col_blocks - (3 * d) // A_DIM
    h1 = _norm_mod(x2, seq, norm1_g, sc1, sh1)
    pg = _project(h1, w_in, layer, tuple(range(n_mix_blocks, n_col_blocks)), "sigmoid", BF16, "proj_gates")
    af = _project(h1, w_in, layer, (1,), "none", F32, "proj_forget")
    p = _project(h1, w_in, layer, (4, 5, 0, 2, 3, 6, 7, 8), "none", BF16, "proj_mixers")
    b_blk = 0
    aq_blk = (3 * B_WIDTH) // A_DIM
    q_blk = aq_blk + 3
    a_feat = _hgrn(p, af, jnp.log(lb), jnp.log1p(-lb), onorm_g, bsz, seq, aq_blk)
    c_feat = _attention(p, rel_bias, bsz, seq, q_blk)
    x2 = _merge(x2, seq, g1, a_feat, c_feat, p, pg, conv_w, w_a.astype(BF16), w_b.astype(BF16), w_c.astype(BF16),
                w_out.astype(BF16), b_blk)

    h, hp, eidx, wts, rank, cnt = _router(x2, seq, norm2_g, sc2, sh2, router_w.T, router_bias)
    n_blocks = (t * TOP_K) // MOE_BLK + N_EXPERTS
    pstart, blk_exp, n_used = _moe_layout(cnt[:, 0], n_blocks)
    pos = _positions(pstart, eidx, rank)
    per = d // (2 * LANES)
    xs = _dispatch(pos, hp, n_blocks * MOE_BLK, per)
    ys = _experts(blk_exp, n_used, xs, wg, wu, wd, layer, per)
    return _combine(pos, x2, seq, g2, h, wts.T, sg.astype(BF16), su.astype(BF16), sd.astype(BF16), ys, per)


def kernel(x, c, ada_w, ada_b, norm1_g, w_in, hgrn_lb_logits, hgrn_onorm_g, conv_w, rel_bias, w_branch_a, w_branch_b, w_branch_c, w_out, norm2_g, router_w, router_bias, moe_w_gate, moe_w_up, moe_w_down, shared_w_gate, shared_w_up, shared_w_down, final_g):
    bsz, seq, d = x.shape
    depth = ada_w.shape[0]
    mod = _modulation(c, ada_w, ada_b)
    lbs = jnp.cumsum(jax.nn.softmax(hgrn_lb_logits.astype(F32), axis=0), axis=0)
    lbs = lbs - lbs[0]
    x2 = x.reshape(bsz * seq, d)
    for l in range(depth):
        x2 = _layer(x2, bsz, seq, l, mod[l], norm1_g[l], w_in, lbs[l], hgrn_onorm_g[l], conv_w[l], rel_bias[l],
                    w_branch_a[l], w_branch_b[l], w_branch_c[l], w_out[l], norm2_g[l], router_w[l], router_bias[l],
                    moe_w_gate, moe_w_up, moe_w_down, shared_w_gate[l], shared_w_up[l], shared_w_down[l])
    return _final_norm(x2, final_g).reshape(bsz, seq, d)
```

```python
import functools

import numpy as np
import jax
import jax.numpy as jnp
from jax import lax
from jax.experimental import pallas as pl
from jax.experimental.pallas import tpu as pltpu

F32 = jnp.float32
BF16 = jnp.bfloat16
I32 = jnp.int32
U32 = jnp.uint32

CHUNK = 64
EPS = 1e-6
A_HEADS = 6
A_HEAD_DIM = 128
A_DIM = A_HEADS * A_HEAD_DIM
B_WIDTH = 512
CONV_W = 3
C_HEADS = 6
C_HEAD_DIM = 128
C_DIM = C_HEADS * C_HEAD_DIM
LEFT_CHUNKS = 8
REL_CLIP = 256
N_EXPERTS = 64
TOP_K = 8
N_GROUPS = 8
GROUP_SIZE = N_EXPERTS // N_GROUPS
TOPK_GROUPS = 4
ROUTED_SCALE = 2.5
N_MOD = 6

V7X_VMEM_BYTES = 64 * 1024 * 1024
LANES = 128
NEG_BIG = -1e30

SUB = 16
FACTOR_SAFE_LOG = 80.0
MID = CHUNK // 2
MOE_BLK = 512
ATT_TQ = 256


def _vmem_limit(nbytes):
    return int(min(max(nbytes * 3 // 2, 16 * 1024 * 1024), V7X_VMEM_BYTES - 8 * 1024 * 1024))


def _cparams(sem, nbytes):
    return pltpu.CompilerParams(dimension_semantics=sem, vmem_limit_bytes=_vmem_limit(nbytes))


def _sigmoid(x):
    return 1.0 / (1.0 + jnp.exp(-x))


def _silu(x):
    return x * _sigmoid(x)


def _split3(x):
    hi = x.astype(BF16)
    r1 = x - hi.astype(F32)
    mid = r1.astype(BF16)
    lo = (r1 - mid.astype(F32)).astype(BF16)
    return hi, mid, lo


def _pack_bf16_pairs(x):
    n = x.shape[1] // 2
    xb = x.astype(BF16).astype(F32)
    lo = lax.bitcast_convert_type(xb[:, :n], U32) >> 16
    hi = lax.bitcast_convert_type(xb[:, n:], U32) & jnp.uint32(0xFFFF0000)
    return lo | hi


def _unpack_bf16_pairs(p):
    lo = lax.bitcast_convert_type(p << 16, F32).astype(BF16)
    hi = lax.bitcast_convert_type(p & jnp.uint32(0xFFFF0000), F32).astype(BF16)
    return lo, hi


def _store_token_tiles(ref, first_token, mat):
    n, w = mat.shape
    per = w // LANES
    for s in range(per):
        ref[pl.ds(first_token * per + s, n, stride=per), :] = mat[:, s * LANES:(s + 1) * LANES]


def _load_token_tiles(ref, first_token, n, per):
    return jnp.concatenate([ref[pl.ds(first_token * per + s, n, stride=per), :] for s in range(per)], axis=1)


def _mod_kernel(c_ref, w_ref, b_ref, o_ref):
    c = c_ref[...]
    o_ref[0] = jnp.dot(_silu(c), w_ref[0], preferred_element_type=F32) + b_ref[0]


def _modulation(c, ada_w, ada_b):
    depth, d, n = ada_w.shape
    bsz = c.shape[0]
    rows = 8
    c_pad = jnp.zeros((rows, d), F32).at[:bsz].set(c)
    tn = 1536 if n % 1536 == 0 else n
    out = pl.pallas_call(
        _mod_kernel,
        grid=(depth, n // tn),
        in_specs=[
            pl.BlockSpec((rows, d), lambda l, j: (0, 0)),
            pl.BlockSpec((1, d, tn), lambda l, j: (l, 0, j)),
            pl.BlockSpec((1, 1, tn), lambda l, j: (l, 0, j)),
        ],
        out_specs=pl.BlockSpec((1, rows, tn), lambda l, j: (l, 0, j)),
        out_shape=jax.ShapeDtypeStruct((depth, rows, n), F32),
        compiler_params=_cparams(("arbitrary", "arbitrary"), 2 * d * tn * 4 + 4 * rows * tn * 4),
        name="adaln_mod",
    )(c_pad, ada_w, ada_b.reshape(depth, 1, n))
    return out[:, :bsz]


def _norm_kernel(x_ref, g_ref, sc_ref, sh_ref, h_ref):
    x = x_ref[...]
    y = x * lax.rsqrt(jnp.mean(x * x, axis=-1, keepdims=True) + EPS)
    h_ref[...] = (y * g_ref[...] * (1.0 + sc_ref[0]) + sh_ref[0]).astype(BF16)


def _norm_mod(x2, seq, g, sc, sh):
    t, d = x2.shape
    tm = min(512, seq)
    tiles_per_batch = seq // tm
    return pl.pallas_call(
        _norm_kernel,
        grid=(t // tm,),
        in_specs=[
            pl.BlockSpec((tm, d), lambda i: (i, 0)),
            pl.BlockSpec((1, d), lambda i: (0, 0)),
            pl.BlockSpec((1, 1, d), lambda i: (i // tiles_per_batch, 0, 0)),
            pl.BlockSpec((1, 1, d), lambda i: (i // tiles_per_batch, 0, 0)),
        ],
        out_specs=pl.BlockSpec((tm, d), lambda i: (i, 0)),
        out_shape=jax.ShapeDtypeStruct((t, d), BF16),
        compiler_params=_cparams(("arbitrary",), 6 * tm * d * 4),
        name="norm_mod",
    )(x2, g.reshape(1, d), sc, sh)


def _proj_kernel(cmap_ref, h_ref, w_ref, o_ref, w_scr, *, act):
    @pl.when(pl.program_id(1) == 0)
    def _():
        w_scr[...] = w_ref[0].astype(BF16)

    acc = jnp.dot(h_ref[...], w_scr[...], preferred_element_type=F32)
    if act == "sigmoid":
        acc = _sigmoid(acc)
    o_ref[...] = acc.astype(o_ref.dtype)


def _project(h, w_all, layer, col_blocks, act, out_dtype, name):
    t, d = h.shape
    tn = A_DIM
    nj = len(col_blocks)
    tm = min(1024, t)
    osz = jnp.dtype(out_dtype).itemsize
    est = 2 * tm * d * 2 + 2 * d * tn * 4 + d * tn * 2 + 2 * tm * tn * osz + 2 * tm * tn * 4
    return pl.pallas_call(
        functools.partial(_proj_kernel, act=act),
        grid_spec=pltpu.PrefetchScalarGridSpec(
            num_scalar_prefetch=1,
            grid=(nj, t // tm),
            in_specs=[
                pl.BlockSpec((tm, d), lambda j, i, cm: (i, 0)),
                pl.BlockSpec((1, d, tn), lambda j, i, cm: (layer, 0, cm[j])),
            ],
            out_specs=pl.BlockSpec((tm, tn), lambda j, i, cm: (i, j)),
            scratch_shapes=[pltpu.VMEM((d, tn), BF16)],
        ),
        out_shape=jax.ShapeDtypeStruct((t, nj * tn), out_dtype),
        compiler_params=_cparams(("arbitrary", "arbitrary"), est),
        name=name,
    )(jnp.asarray(col_blocks, I32), h, w_all)


def _hgrn_kernel(aq_ref, af_ref, ai_ref, ag_ref, loglb_ref, log1mlb_ref, og_ref, sel_ref, o_ref,
                 st_scr, lf_scr, q_scr, k_scr, v_scr, b_scr, p_scr, *, n_chunks):
    @pl.when(pl.program_id(1) == 0)
    def _():
        st_scr[...] = jnp.zeros_like(st_scr)

    hd = A_HEAD_DIM
    n_sub = CHUNK // SUB
    nt = (((1,), (1,)), ((), ()))
    row_c = lax.broadcasted_iota(I32, (CHUNK, CHUNK), 0)
    col_c = lax.broadcasted_iota(I32, (CHUNK, CHUNK), 1)
    tri = (col_c <= row_c).astype(BF16)
    sub_row = lax.broadcasted_iota(I32, (SUB, 1), 0)
    chunk_row = lax.broadcasted_iota(I32, (CHUNK, 1), 0)
    log_lb = loglb_ref[...]
    log1m_lb = log1mlb_ref[...]
    og = og_ref[...]

    def log_f_body(ci, lowest):
        rows = pl.ds(pl.multiple_of(ci * CHUNK, CHUNK), CHUNK)
        af = af_ref[rows, :]
        ls = jnp.minimum(af, 0.0) - jnp.log(1.0 + jnp.exp(-jnp.abs(af)))
        t2 = log1m_lb + ls
        lf = jnp.maximum(log_lb, t2) + jnp.log(1.0 + jnp.exp(-jnp.abs(log_lb - t2)))
        lf_scr[rows, :] = lf
        first = jnp.sum(jnp.where(chunk_row <= MID, lf, 0.0), axis=0, keepdims=True)
        second = jnp.sum(jnp.where(chunk_row >= MID, lf, 0.0), axis=0, keepdims=True)
        return jnp.minimum(lowest, jnp.minimum(first, second))

    lowest_sum = lax.fori_loop(0, n_chunks, log_f_body, jnp.zeros((1, A_DIM), F32), unroll=2)
    safe = jnp.min(lowest_sum) > -FACTOR_SAFE_LOG

    def prep(ci):
        r0 = pl.multiple_of(ci * CHUNK, CHUNK)
        rows = pl.ds(r0, CHUNK)
        lf = lf_scr[rows, :]
        hi, mid, lo = _split3(lf)
        b = (jnp.dot(tri, hi, preferred_element_type=F32) + jnp.dot(tri, mid, preferred_element_type=F32)
             + jnp.dot(tri, lo, preferred_element_type=F32))
        q = aq_ref[rows, :].astype(F32) * (A_HEAD_DIM ** -0.5)
        return rows, 1.0 - jnp.exp(lf), b, q, ai_ref[rows, :]

    def head_tail(h, o, kh, vh, bh, bl):
        ktil = (kh * jnp.exp(bl - bh)).astype(BF16)
        upd = lax.dot_general(vh, ktil, (((0,), (0,)), ((), ())), preferred_element_type=F32)
        st_scr[h] = st_scr[h] * jnp.exp(bl) + upd
        return o * lax.rsqrt(jnp.mean(o * o, axis=-1, keepdims=True) + EPS)

    def finish(rows, outs):
        o_all = jnp.concatenate(outs, axis=1)
        gate = ag_ref[rows, :].astype(F32)
        o_ref[rows, :] = (o_all * og * _silu(gate)).astype(BF16)

    def fast_chunk(ci, carry):
        rows, k_all, b_all, q_all, v_all = prep(ci)
        qt_all = (q_all * jnp.exp(b_all)).astype(BF16)
        b_mid = b_all[MID:MID + 1]
        qm_all = (q_all * jnp.exp(b_all - b_mid)).astype(BF16)
        km_all = (k_all * jnp.exp(b_mid - b_all)).astype(BF16)
        b_last = b_all[CHUNK - 1:CHUNK]
        outs = []
        for h in range(A_HEADS):
            cols = slice(h * hd, (h + 1) * hd)
            vh = v_all[:, cols]
            qt = qt_all[:, cols]
            a = lax.dot_general(qm_all[:, cols], km_all[:, cols], nt, preferred_element_type=F32)
            a = jnp.where(col_c <= row_c, a, 0.0).astype(BF16)
            o = (jnp.dot(a, vh, preferred_element_type=F32)
                 + lax.dot_general(qt, st_scr[h].astype(BF16), nt, preferred_element_type=F32))
            outs.append(head_tail(h, o, k_all[:, cols], vh, b_all[:, cols], b_last[:, cols]))
        finish(rows, outs)
        return carry

    def exact_chunk(ci, carry):
        rows, k_new, b_new, q_new, v_new = prep(ci)
        k_scr[...] = k_new
        b_scr[...] = b_new
        q_scr[...] = q_new
        v_scr[...] = v_new.astype(F32)

        for i in range(n_sub):
            s0 = i * SUB
            b_blk = b_scr[pl.ds(s0, SUB), :]
            q_blk = q_scr[pl.ds(s0, SUB), :]
            for s in range(SUB):
                b_s = b_scr[pl.ds(s0 + s, 1), :]
                k_s = k_scr[pl.ds(s0 + s, 1), :]
                e = jnp.exp(jnp.where(sub_row >= s, b_blk - b_s, NEG_BIG))
                pv = (q_blk * k_s * e).astype(BF16)
                for h in range(A_HEADS):
                    p_scr[pl.ds((i * A_HEADS + h) * SUB, SUB), s * hd:(s + 1) * hd] = pv[:, h * hd:(h + 1) * hd]
        diag = jnp.dot(p_scr[...], sel_ref[...], preferred_element_type=F32)

        b_all = b_scr[...]
        q_all = q_scr[...]
        k_all = k_scr[...]
        qhat, khat = [], []
        for i in range(1, n_sub):
            s0 = i * SUB
            beta = b_scr[pl.ds(s0 - 1, 1), :]
            qhat.append((q_all[s0:s0 + SUB] * jnp.exp(b_all[s0:s0 + SUB] - beta)).astype(BF16))
            khat.append(jnp.where(chunk_row < s0, k_all * jnp.exp(jnp.minimum(beta - b_all, 0.0)), 0.0).astype(BF16))
        zero_blk = jnp.zeros((SUB, hd), BF16)

        b_last = b_scr[pl.ds(CHUNK - 1, 1), :]
        outs = []
        for h in range(A_HEADS):
            cols = slice(h * hd, (h + 1) * hd)
            qh = q_all[:, cols]
            kh = k_all[:, cols]
            vh = v_scr[:, cols].astype(BF16)
            bh = b_all[:, cols]
            st = st_scr[h]
            lhs_rows = [jnp.concatenate([zero_blk] * (n_sub - 1), axis=1)]
            for i in range(1, n_sub):
                parts = [zero_blk] * (n_sub - 1)
                parts[i - 1] = qhat[i - 1][:, cols]
                lhs_rows.append(jnp.concatenate(parts, axis=1))
            lhs = jnp.concatenate(lhs_rows, axis=0)
            rhs = jnp.concatenate([kk[:, cols] for kk in khat], axis=1)
            a_off = lax.dot_general(lhs, rhs, nt, preferred_element_type=F32)
            a_diag = jnp.concatenate(
                [pltpu.roll(diag[(i * A_HEADS + h) * SUB:(i * A_HEADS + h + 1) * SUB], i * SUB, 1) if i else
                 diag[h * SUB:(h + 1) * SUB] for i in range(n_sub)], axis=0)
            a = (a_diag[:, :CHUNK] + a_off).astype(BF16)
            qt = (qh * jnp.exp(bh)).astype(BF16)
            o = (jnp.dot(a, vh, preferred_element_type=F32)
                 + lax.dot_general(qt, st.astype(BF16), nt, preferred_element_type=F32))
            outs.append(head_tail(h, o, kh, vh, bh, b_last[:, cols]))
        finish(rows, outs)
        return carry

    @pl.when(safe)
    def _():
        lax.fori_loop(0, n_chunks, fast_chunk, 0, unroll=8)

    @pl.when(jnp.logical_not(safe))
    def _():
        lax.fori_loop(0, n_chunks, exact_chunk, 0)


def _hgrn(p, af, log_lb, log1m_lb, onorm_g, bsz, seq, aq_blk):
    t = p.shape[0]
    ts = min(512, seq)
    n_seq_tiles = seq // ts
    n_chunks = ts // CHUNK
    kern = functools.partial(_hgrn_kernel, n_chunks=n_chunks)
    row = lambda b, i: b * n_seq_tiles + i
    est = (2 * 3 * ts * A_DIM * 2 + 2 * ts * A_DIM * 4 + 2 * ts * A_DIM * 2 + A_HEADS * 128 * 128 * 4
           + 4 * CHUNK * A_DIM * 4 + CHUNK * SUB * A_DIM * 6 + 3 * ts * A_DIM * 4)
    sel_rows = np.arange(SUB * A_HEAD_DIM)
    selector = jnp.asarray(sel_rows[:, None] // A_HEAD_DIM == np.arange(A_HEAD_DIM)[None, :], BF16)
    return pl.pallas_call(
        kern,
        grid=(bsz, n_seq_tiles),
        in_specs=[
            pl.BlockSpec((ts, A_DIM), lambda b, i: (row(b, i), aq_blk)),
            pl.BlockSpec((ts, A_DIM), lambda b, i: (row(b, i), 0)),
            pl.BlockSpec((ts, A_DIM), lambda b, i: (row(b, i), aq_blk + 1)),
            pl.BlockSpec((ts, A_DIM), lambda b, i: (row(b, i), aq_blk + 2)),
            pl.BlockSpec((1, A_DIM), lambda b, i: (0, 0)),
            pl.BlockSpec((1, A_DIM), lambda b, i: (0, 0)),
            pl.BlockSpec((1, A_DIM), lambda b, i: (0, 0)),
            pl.BlockSpec((SUB * A_HEAD_DIM, A_HEAD_DIM), lambda b, i: (0, 0)),
        ],
        out_specs=pl.BlockSpec((ts, A_DIM), lambda b, i: (row(b, i), 0)),
        out_shape=jax.ShapeDtypeStruct((t, A_DIM), BF16),
        scratch_shapes=[
            pltpu.VMEM((A_HEADS, A_HEAD_DIM, A_HEAD_DIM), F32),
            pltpu.VMEM((ts, A_DIM), F32),
            pltpu.VMEM((CHUNK, A_DIM), F32),
            pltpu.VMEM((CHUNK, A_DIM), F32),
            pltpu.VMEM((CHUNK, A_DIM), F32),
            pltpu.VMEM((CHUNK, A_DIM), F32),
            pltpu.VMEM((CHUNK * A_HEADS, SUB * A_HEAD_DIM), BF16),
        ],
        compiler_params=_cparams(("arbitrary", "arbitrary"), est),
        name="hgrn2_scan",
    )(p, af, p, p, log_lb.reshape(1, A_DIM), log1m_lb.reshape(1, A_DIM), onorm_g.reshape(1, A_DIM), selector)


def _attn_kernel(q_ref, k0_ref, k1_ref, k2_ref, v0_ref, v1_ref, v2_ref, bias_ref, o_ref):
    i = pl.program_id(1)
    tq = q_ref.shape[0]
    hd = C_HEAD_DIM
    scale = C_HEAD_DIM ** -0.5
    col = lax.broadcasted_iota(I32, (1, 3 * tq), 1)
    valid = (col >= 2 * tq) | ((col >= tq) & (i >= 1)) | (i >= 2)
    for h in range(C_HEADS):
        cols = slice(h * hd, (h + 1) * hd)
        kh = jnp.concatenate([k0_ref[:, cols], k1_ref[:, cols], k2_ref[:, cols]], axis=0)
        vh = jnp.concatenate([v0_ref[:, cols], v1_ref[:, cols], v2_ref[:, cols]], axis=0)
        s = lax.dot_general(q_ref[:, cols], kh, (((1,), (1,)), ((), ())), preferred_element_type=F32)
        s = jnp.where(valid, s * scale + bias_ref[h], NEG_BIG)
        e = jnp.exp(s - jnp.max(s, axis=-1, keepdims=True))
        inv = 1.0 / jnp.sum(e, axis=-1, keepdims=True)
        o_ref[:, cols] = (jnp.dot(e.astype(BF16), vh, preferred_element_type=F32) * inv).astype(BF16)


def _attention_bias_table(rel_bias, tq):
    band = (LEFT_CHUNKS + 1) * CHUNK
    assert tq * 2 == LEFT_CHUNKS * CHUNK, "three key slots of tq rows must cover the band"
    diff = np.arange(CHUNK + band - 1) - (band - 1)
    ext = rel_bias.astype(F32)[:, np.clip(diff + LEFT_CHUNKS * CHUNK, -REL_CLIP, REL_CLIP) + REL_CLIP]
    rev = ext[:, ::-1]
    chunk_tab = jnp.stack([rev[:, CHUNK - 1 - tt:CHUNK - 1 - tt + band] for tt in range(CHUNK)], axis=1)
    rows = [jnp.pad(chunk_tab, ((0, 0), (0, 0), (c * CHUNK, 3 * tq - band - c * CHUNK)), constant_values=NEG_BIG)
            for c in range(tq // CHUNK)]
    return jnp.concatenate(rows, axis=1)


def _attention(p, rel_bias, bsz, seq, q_blk):
    t = p.shape[0]
    tq = ATT_TQ
    nq = seq // tq
    bias = _attention_bias_table(rel_bias, tq)
    row = lambda b, i, back: b * nq + jnp.maximum(i - back, 0)
    kv_spec = lambda blk, back: pl.BlockSpec((tq, C_DIM), lambda b, i: (row(b, i, back), blk))
    est = 2 * 7 * tq * C_DIM * 2 + 2 * C_HEADS * tq * 3 * tq * 4 + 2 * tq * C_DIM * 2 + 6 * tq * 3 * tq * 4
    return pl.pallas_call(
        _attn_kernel,
        grid=(bsz, nq),
        in_specs=[
            pl.BlockSpec((tq, C_DIM), lambda b, i: (b * nq + i, q_blk)),
            kv_spec(q_blk + 1, 2), kv_spec(q_blk + 1, 1), kv_spec(q_blk + 1, 0),
            kv_spec(q_blk + 2, 2), kv_spec(q_blk + 2, 1), kv_spec(q_blk + 2, 0),
            pl.BlockSpec((C_HEADS, tq, 3 * tq), lambda b, i: (0, 0, 0)),
        ],
        out_specs=pl.BlockSpec((tq, C_DIM), lambda b, i: (b * nq + i, 0)),
        out_shape=jax.ShapeDtypeStruct((t, C_DIM), BF16),
        compiler_params=_cparams(("arbitrary", "arbitrary"), est),
        name="chunk_attention",
    )(p, p, p, p, p, p, p, bias)


def _merge_kernel(x_ref, gmod_ref, a_ref, c_ref, bb_ref, bc_ref, bu_ref, hbc_ref, hbu_ref, cw_ref,
                  ga_ref, gb_ref, gc_ref, wa_ref, wb_ref, wc_ref, wo_ref, o_ref, *, tiles_per_batch):
    i = pl.program_id(0)
    tm = x_ref.shape[0]
    u = bc_ref[...].astype(F32) * bu_ref[...].astype(F32)
    halo = hbc_ref[...].astype(F32) * hbu_ref[...].astype(F32)
    halo = jnp.where(i % tiles_per_batch == 0, 0.0, halo)
    hrows = halo.shape[0]
    hm1 = halo[hrows - 1:hrows]
    hm2 = halo[hrows - 2:hrows - 1]
    r = lax.broadcasted_iota(I32, (tm, 1), 0)
    u1 = jnp.where(r == 0, hm1, pltpu.roll(u, 1, 0))
    u2 = jnp.where(r == 0, hm2, jnp.where(r == 1, hm1, pltpu.roll(u, 2, 0)))
    cw = cw_ref[...]
    conv = u2 * cw[0:1] + u1 * cw[1:2] + u * cw[2:3]
    bfeat = (bb_ref[...].astype(F32) * conv).astype(BF16)
    ya = jnp.dot(a_ref[...], wa_ref[...], preferred_element_type=F32)
    yb = jnp.dot(bfeat, wb_ref[...], preferred_element_type=F32)
    yc = jnp.dot(c_ref[...], wc_ref[...], preferred_element_type=F32)
    merged = (ga_ref[...].astype(F32) * ya + gb_ref[...].astype(F32) * yb + gc_ref[...].astype(F32) * yc)
    y = jnp.dot(merged.astype(BF16), wo_ref[...], preferred_element_type=F32)
    o_ref[...] = x_ref[...] + gmod_ref[0] * y


def _merge(x2, seq, gmod, a_feat, c_feat, p, pg, conv_w, wa, wb, wc, wo, b_blk):
    t, d = x2.shape
    tm = min(256, seq)
    tiles_per_batch = seq // tm
    halo = 16
    hb = tm // halo
    kern = functools.partial(_merge_kernel, tiles_per_batch=tiles_per_batch)
    const = lambda shape: pl.BlockSpec(shape, lambda i: (0,) * len(shape))
    halo_spec = lambda blk: pl.BlockSpec((halo, B_WIDTH), lambda i: (jnp.maximum(i * hb - 1, 0), blk))
    est = (4 * tm * d * 4 + 4 * tm * A_DIM * 2 + 6 * tm * B_WIDTH * 2 + 6 * tm * d * 2
           + 2 * (A_DIM + B_WIDTH + C_DIM + d) * d * 2 + 6 * tm * d * 4)
    return pl.pallas_call(
        kern,
        grid=(t // tm,),
        in_specs=[
            pl.BlockSpec((tm, d), lambda i: (i, 0)),
            pl.BlockSpec((1, 1, d), lambda i: (i // tiles_per_batch, 0, 0)),
            pl.BlockSpec((tm, A_DIM), lambda i: (i, 0)),
            pl.BlockSpec((tm, C_DIM), lambda i: (i, 0)),
            pl.BlockSpec((tm, B_WIDTH), lambda i: (i, b_blk)),
            pl.BlockSpec((tm, B_WIDTH), lambda i: (i, b_blk + 1)),
            pl.BlockSpec((tm, B_WIDTH), lambda i: (i, b_blk + 2)),
            halo_spec(b_blk + 1),
            halo_spec(b_blk + 2),
            const((CONV_W, B_WIDTH)),
            pl.BlockSpec((tm, d), lambda i: (i, 0)),
            pl.BlockSpec((tm, d), lambda i: (i, 1)),
            pl.BlockSpec((tm, d), lambda i: (i, 2)),
            const((A_DIM, d)), const((B_WIDTH, d)), const((C_DIM, d)), const((d, d)),
        ],
        out_specs=pl.BlockSpec((tm, d), lambda i: (i, 0)),
        out_shape=jax.ShapeDtypeStruct((t, d), F32),
        compiler_params=_cparams(("arbitrary",), est),
        name="merge_outproj",
    )(x2, gmod, a_feat, c_feat, p, p, p, p, p, conv_w, pg, pg, pg, wa, wb, wc, wo)


def _first_max(vals, iota, n, axis):
    m = jnp.max(vals, axis=axis, keepdims=True)
    idx = jnp.min(jnp.where(vals == m, iota, n), axis=axis, keepdims=True)
    return m, idx


def _router_kernel(x_ref, g_ref, sc_ref, sh_ref, rwt_ref, rb_ref, h_ref, hp_ref, e_ref, w_ref, rank_ref, cnt_ref,
                   cnt_scr):
    step = pl.program_id(0)

    @pl.when(step == 0)
    def _():
        cnt_scr[...] = jnp.zeros_like(cnt_scr)

    tm = x_ref.shape[0]
    x = x_ref[...]
    y = x * lax.rsqrt(jnp.mean(x * x, axis=-1, keepdims=True) + EPS)
    h = y * g_ref[...] * (1.0 + sc_ref[0]) + sh_ref[0]
    h_ref[...] = h.astype(BF16)
    _store_token_tiles(hp_ref, 0, _pack_bf16_pairs(h))

    nt = (((1,), (1,)), ((), ()))
    h_hi, h_mid, _ = _split3(h)
    w_hi, w_mid, _ = _split3(rwt_ref[...])
    logits = (lax.dot_general(w_hi, h_hi, nt, preferred_element_type=F32)
              + lax.dot_general(w_hi, h_mid, nt, preferred_element_type=F32)
              + lax.dot_general(w_mid, h_hi, nt, preferred_element_type=F32))
    scores = _sigmoid(logits)
    sel = scores + rb_ref[:, 0:1]

    g3 = sel.reshape(N_GROUPS, GROUP_SIZE, tm)
    j3 = lax.broadcasted_iota(I32, g3.shape, 1)
    m1, i1 = _first_max(g3, j3, GROUP_SIZE, 1)
    m2 = jnp.max(jnp.where(j3 == i1, -jnp.inf, g3), axis=1, keepdims=True)
    gs = (m1 + m2).reshape(N_GROUPS, tm)
    gi = lax.broadcasted_iota(I32, gs.shape, 0)
    gmask = jnp.zeros(gs.shape, jnp.bool_)
    for _ in range(TOPK_GROUPS):
        _, idx = _first_max(gs, gi, N_GROUPS, 0)
        hit = gi == idx
        gmask = gmask | hit
        gs = jnp.where(hit, -jnp.inf, gs)
    emask = jnp.broadcast_to(gmask.reshape(N_GROUPS, 1, tm), (N_GROUPS, GROUP_SIZE, tm)).reshape(N_EXPERTS, tm)
    masked = jnp.where(emask, sel, -jnp.inf)

    ei = lax.broadcasted_iota(I32, (N_EXPERTS, tm), 0)
    chosen = jnp.zeros((N_EXPERTS, tm), F32)
    e_rows, w_rows = [], []
    for _ in range(TOP_K):
        _, idx = _first_max(masked, ei, N_EXPERTS, 0)
        hit = ei == idx
        e_rows.append(idx)
        w_rows.append(jnp.sum(jnp.where(hit, scores, 0.0), axis=0, keepdims=True))
        chosen = jnp.where(hit, 1.0, chosen)
        masked = jnp.where(hit, -jnp.inf, masked)
    eidx = jnp.concatenate(e_rows, axis=0)
    wts = jnp.concatenate(w_rows, axis=0)
    wts = wts / jnp.sum(wts, axis=0, keepdims=True) * ROUTED_SCALE
    e_ref[...] = eidx
    w_ref[...] = wts

    tr = lax.broadcasted_iota(I32, (tm, tm), 0)
    tc = lax.broadcasted_iota(I32, (tm, tm), 1)
    before = (tr < tc).astype(BF16)
    base = cnt_scr[:, 0:1]
    rank_all = jnp.dot(chosen.astype(BF16), before, preferred_element_type=F32) + base
    ranks = [jnp.sum(jnp.where(ei == e_rows[k], rank_all, 0.0), axis=0, keepdims=True) for k in range(TOP_K)]
    rank_ref[...] = jnp.concatenate(ranks, axis=0).astype(I32)
    total = base + jnp.sum(chosen, axis=1, keepdims=True)
    cnt_scr[...] = jnp.broadcast_to(total, cnt_scr.shape)
    cnt_ref[...] = jnp.broadcast_to(total, cnt_ref.shape).astype(I32)


def _router(x2, seq, g, sc, sh, router_wt, router_bias):
    t, d = x2.shape
    tm = min(512, seq)
    tiles_per_batch = seq // tm
    est = (2 * tm * d * 4 + 2 * tm * d * 2 + 2 * tm * d * 2 + 2 * N_EXPERTS * d * 4 + tm * tm * 2
           + 8 * tm * d * 4 + 16 * N_EXPERTS * tm * 4)
    return pl.pallas_call(
        _router_kernel,
        grid=(t // tm,),
        in_specs=[
            pl.BlockSpec((tm, d), lambda i: (i, 0)),
            pl.BlockSpec((1, d), lambda i: (0, 0)),
            pl.BlockSpec((1, 1, d), lambda i: (i // tiles_per_batch, 0, 0)),
            pl.BlockSpec((1, 1, d), lambda i: (i // tiles_per_batch, 0, 0)),
            pl.BlockSpec((N_EXPERTS, d), lambda i: (0, 0)),
            pl.BlockSpec((N_EXPERTS, LANES), lambda i: (0, 0)),
        ],
        out_specs=[
            pl.BlockSpec((tm, d), lambda i: (i, 0)),
            pl.BlockSpec((tm * d // (2 * LANES), LANES), lambda i: (i, 0)),
            pl.BlockSpec((TOP_K, tm), lambda i: (0, i)),
            pl.BlockSpec((TOP_K, tm), lambda i: (0, i)),
            pl.BlockSpec((TOP_K, tm), lambda i: (0, i)),
            pl.BlockSpec((N_EXPERTS, LANES), lambda i: (0, 0)),
        ],
        out_shape=[
            jax.ShapeDtypeStruct((t, d), BF16),
            jax.ShapeDtypeStruct((t * d // (2 * LANES), LANES), U32),
            jax.ShapeDtypeStruct((TOP_K, t), I32),
            jax.ShapeDtypeStruct((TOP_K, t), F32),
            jax.ShapeDtypeStruct((TOP_K, t), I32),
            jax.ShapeDtypeStruct((N_EXPERTS, LANES), I32),
        ],
        scratch_shapes=[pltpu.VMEM((N_EXPERTS, LANES), F32)],
        compiler_params=_cparams(("arbitrary",), est),
        name="norm_router",
    )(x2, g.reshape(1, d), sc, sh, router_wt, jnp.broadcast_to(router_bias.reshape(N_EXPERTS, 1), (N_EXPERTS, LANES)))


def _pos_kernel(start_ref, e_ref, rank_ref, pos_ref):
    e = e_ref[...]
    acc = rank_ref[...]
    for ex in range(N_EXPERTS):
        acc = acc + jnp.where(e == ex, start_ref[ex], 0)
    pos_ref[...] = acc


def _positions(pstart, eidx, rank):
    k, t = eidx.shape
    tl = min(4096, t)
    return pl.pallas_call(
        _pos_kernel,
        grid_spec=pltpu.PrefetchScalarGridSpec(
            num_scalar_prefetch=1,
            grid=(t // tl,),
            in_specs=[pl.BlockSpec((k, tl), lambda i, s: (0, i)), pl.BlockSpec((k, tl), lambda i, s: (0, i))],
            out_specs=pl.BlockSpec((k, tl), lambda i, s: (0, i)),
        ),
        out_shape=jax.ShapeDtypeStruct((k, t), I32),
        compiler_params=_cparams(("arbitrary",), 8 * k * tl * 4),
        name="moe_positions",
    )(pstart, eidx, rank)


def _dispatch_kernel(pos_ref, hp_ref, xs_ref, sem, *, tm, per):
    def copy(t, k):
        src = hp_ref.at[pl.ds(pl.multiple_of(t * per, per), per)]
        dst = xs_ref.at[pl.ds(pl.multiple_of(pos_ref[k, t] * per, per), per)]
        return pltpu.make_async_copy(src, dst, sem)

    def start(t, c):
        for k in range(TOP_K):
            copy(t, k).start(priority=k % 2)
        return c

    def wait(t, c):
        for k in range(TOP_K):
            copy(t, k).wait()
        return c

    lax.fori_loop(0, tm, start, 0, unroll=2)
    lax.fori_loop(0, tm, wait, 0, unroll=2)


def _dispatch(pos, hp, n_rows, per):
    t = hp.shape[0] // per
    tm = min(512, t)
    kern = functools.partial(_dispatch_kernel, tm=tm, per=per)
    return pl.pallas_call(
        kern,
        grid=(t // tm,),
        in_specs=[
            pl.BlockSpec((TOP_K, tm), lambda i: (0, i), memory_space=pltpu.SMEM),
            pl.BlockSpec((tm * per, LANES), lambda i: (i, 0)),
        ],
        out_specs=pl.BlockSpec(memory_space=pl.ANY),
        out_shape=jax.ShapeDtypeStruct((n_rows * per, LANES), U32),
        scratch_shapes=[pltpu.SemaphoreType.DMA(())],
        compiler_params=pltpu.CompilerParams(dimension_semantics=("arbitrary",), has_side_effects=True,
                                             vmem_limit_bytes=_vmem_limit(4 * tm * per * LANES * 4)),
        name="moe_dispatch",
    )(pos, hp)


def _expert_kernel(blk_exp_ref, n_used_ref, xs_ref, wg_ref, wu_ref, wd_ref, ys_ref, wg_scr, wu_scr, wd_scr, *, per):
    b = pl.program_id(0)
    new_expert = (b == 0) | (blk_exp_ref[b] != blk_exp_ref[jnp.maximum(b - 1, 0)])

    @pl.when(new_expert)
    def _():
        wg_scr[...] = wg_ref[0, 0].astype(BF16)
        wu_scr[...] = wu_ref[0, 0].astype(BF16)
        wd_scr[...] = wd_ref[0, 0].astype(BF16)

    @pl.when(b < n_used_ref[0])
    def _():
        lo, hi = _unpack_bf16_pairs(_load_token_tiles(xs_ref, 0, MOE_BLK, per))
        half = lo.shape[1]
        gate = (jnp.dot(lo, wg_scr[:half], preferred_element_type=F32)
                + jnp.dot(hi, wg_scr[half:], preferred_element_type=F32))
        up = (jnp.dot(lo, wu_scr[:half], preferred_element_type=F32)
              + jnp.dot(hi, wu_scr[half:], preferred_element_type=F32))
        hidden = (_silu(gate) * up).astype(BF16)
        _store_token_tiles(ys_ref, 0, _pack_bf16_pairs(jnp.dot(hidden, wd_scr[...], preferred_element_type=F32)))

    @pl.when(b >= n_used_ref[0])
    def _():
        ys_ref[...] = jnp.zeros_like(ys_ref)


def _experts(blk_exp, n_used, xs, wg, wu, wd, layer, per):
    half = per * LANES
    d = 2 * half
    de = wg.shape[3]
    nb = xs.shape[0] // (MOE_BLK * per)
    est = 4 * MOE_BLK * half * 4 + 2 * 3 * d * de * 4 + 3 * d * de * 2 + MOE_BLK * (2 * de + d) * 4
    return pl.pallas_call(
        functools.partial(_expert_kernel, per=per),
        grid_spec=pltpu.PrefetchScalarGridSpec(
            num_scalar_prefetch=2,
            grid=(nb,),
            in_specs=[
                pl.BlockSpec((MOE_BLK * per, LANES), lambda b, be, nu: (b, 0)),
                pl.BlockSpec((1, 1, d, de), lambda b, be, nu: (layer, be[b], 0, 0)),
                pl.BlockSpec((1, 1, d, de), lambda b, be, nu: (layer, be[b], 0, 0)),
                pl.BlockSpec((1, 1, de, d), lambda b, be, nu: (layer, be[b], 0, 0)),
            ],
            out_specs=pl.BlockSpec((MOE_BLK * per, LANES), lambda b, be, nu: (b, 0)),
            scratch_shapes=[pltpu.VMEM((d, de), BF16), pltpu.VMEM((d, de), BF16), pltpu.VMEM((de, d), BF16)],
        ),
        out_shape=jax.ShapeDtypeStruct(xs.shape, U32),
        compiler_params=_cparams(("arbitrary",), est),
        name="moe_experts",
    )(blk_exp, n_used, xs, wg, wu, wd)


def _combine_kernel(pos_ref, x_ref, gmod_ref, h_ref, wt_ref, sg_ref, su_ref, sd_ref, ys_ref, o_ref, yg_scr, sem,
                    *, tm, per):
    def copy(t, k):
        src = ys_ref.at[pl.ds(pl.multiple_of(pos_ref[k, t] * per, per), per)]
        dst = yg_scr.at[pl.ds(pl.multiple_of((k * tm + t) * per, per), per)]
        return pltpu.make_async_copy(src, dst, sem)

    def start(t, c):
        for k in range(TOP_K):
            copy(t, k).start(priority=k % 2)
        return c

    def wait(t, c):
        for k in range(TOP_K):
            copy(t, k).wait()
        return c

    lax.fori_loop(0, tm, start, 0, unroll=2)
    h = h_ref[...]
    hidden = (_silu(jnp.dot(h, sg_ref[...], preferred_element_type=F32))
              * jnp.dot(h, su_ref[...], preferred_element_type=F32)).astype(BF16)
    shared = jnp.dot(hidden, sd_ref[...], preferred_element_type=F32)
    lax.fori_loop(0, tm, wait, 0, unroll=2)
    half = per * LANES
    acc_lo = shared[:, :half]
    acc_hi = shared[:, half:]
    wt = wt_ref[...]
    for k in range(TOP_K):
        packed = _load_token_tiles(yg_scr, k * tm, tm, per)
        wk = wt[:, k:k + 1]
        acc_lo = acc_lo + wk * lax.bitcast_convert_type(packed << 16, F32)
        acc_hi = acc_hi + wk * lax.bitcast_convert_type(packed & jnp.uint32(0xFFFF0000), F32)
    g = gmod_ref[0]
    o_ref[:, :half] = x_ref[:, :half] + g[:, :half] * acc_lo
    o_ref[:, half:] = x_ref[:, half:] + g[:, half:] * acc_hi


def _combine(pos, x2, seq, gmod, h, wts_t, sg, su, sd, ys, per):
    t, d = x2.shape
    half = d // 2
    ds_ = sg.shape[1]
    tm = min(256, seq)
    tiles_per_batch = seq // tm
    kern = functools.partial(_combine_kernel, tm=tm, per=per)
    const = lambda shape: pl.BlockSpec(shape, lambda i: (0,) * len(shape))
    est = 4 * tm * d * 4 + 2 * tm * d * 2 + 2 * 3 * d * ds_ * 2 + TOP_K * tm * half * 4 + 8 * tm * d * 4
    return pl.pallas_call(
        kern,
        grid=(t // tm,),
        in_specs=[
            pl.BlockSpec((TOP_K, tm), lambda i: (0, i), memory_space=pltpu.SMEM),
            pl.BlockSpec((tm, d), lambda i: (i, 0)),
            pl.BlockSpec((1, 1, d), lambda i: (i // tiles_per_batch, 0, 0)),
            pl.BlockSpec((tm, d), lambda i: (i, 0)),
            pl.BlockSpec((tm, TOP_K), lambda i: (i, 0)),
            const((d, ds_)), const((d, ds_)), const((ds_, d)),
            pl.BlockSpec(memory_space=pl.ANY),
        ],
        out_specs=pl.BlockSpec((tm, d), lambda i: (i, 0)),
        out_shape=jax.ShapeDtypeStruct((t, d), F32),
        scratch_shapes=[pltpu.VMEM((TOP_K * tm * per, LANES), U32), pltpu.SemaphoreType.DMA(())],
        compiler_params=_cparams(("arbitrary",), est),
        name="moe_combine",
    )(pos, x2, gmod, h, wts_t, sg, su, sd, ys)


def _final_kernel(x_ref, g_ref, o_ref):
    x = x_ref[...]
    o_ref[...] = x * lax.rsqrt(jnp.mean(x * x, axis=-1, keepdims=True) + EPS) * g_ref[...]


def _final_norm(x2, g):
    t, d = x2.shape
    tm = min(1024, t)
    return pl.pallas_call(
        _final_kernel,
        grid=(t // tm,),
        in_specs=[pl.BlockSpec((tm, d), lambda i: (i, 0)), pl.BlockSpec((1, d), lambda i: (0, 0))],
        out_specs=pl.BlockSpec((tm, d), lambda i: (i, 0)),
        out_shape=jax.ShapeDtypeStruct((t, d), F32),
        compiler_params=_cparams(("arbitrary",), 4 * tm * d * 4),
        name="final_norm",
    )(x2, g.reshape(1, d))


def _moe_layout(counts, n_blocks):
    padded = (counts + MOE_BLK - 1) // MOE_BLK * MOE_BLK
    pends = jnp.cumsum(padded)
    pstart = (pends - padded).astype(I32)
    blk_row = jnp.arange(n_blocks, dtype=I32)[:, None] * MOE_BLK
    blk_exp = jnp.minimum(jnp.sum((pends[None, :] <= blk_row).astype(I32), axis=1), N_EXPERTS - 1)
    n_used = (pends[-1:] // MOE_BLK).astype(I32)
    return pstart, blk_exp, n_used


def _layer(x2, bsz, seq, layer, mod, norm1_g, w_in, lb, onorm_g, conv_w, rel_bias, w_a, w_b, w_c, w_out, norm2_g,
           router_w, router_bias, wg, wu, wd, sg, su, sd):
    t, d = x2.shape
    sh1, sc1, g1, sh2, sc2, g2 = [m.reshape(bsz, 1, d) for m in jnp.split(mod, N_MOD, axis=-1)]
    n_col_blocks = w_in.shape[2] // A_DIM
    n_mix_blocks = n_col_blocks - (3 * d) // A_DIM
    h1 = _norm_mod(x2, seq, norm1_g, sc1, sh1)
    pg = _project(h1, w_in, layer, tuple(range(n_mix_blocks, n_col_blocks)), "sigmoid", BF16, "proj_gates")
    af = _project(h1, w_in, layer, (1,), "none", F32, "proj_forget")
    p = _project(h1, w_in, layer, (4, 5, 0, 2, 3, 6, 7, 8), "none", BF16, "proj_mixers")
    b_blk = 0
    aq_blk = (3 * B_WIDTH) // A_DIM
    q_blk = aq_blk + 3
    a_feat = _hgrn(p, af, jnp.log(lb), jnp.log1p(-lb), onorm_g, bsz, seq, aq_blk)
    c_feat = _attention(p, rel_bias, bsz, seq, q_blk)
    x2 = _merge(x2, seq, g1, a_feat, c_feat, p, pg, conv_w, w_a.astype(BF16), w_b.astype(BF16), w_c.astype(BF16),
                w_out.astype(BF16), b_blk)

    h, hp, eidx, wts, rank, cnt = _router(x2, seq, norm2_g, sc2, sh2, router_w.T, router_bias)
    n_blocks = (t * TOP_K) // MOE_BLK + N_EXPERTS
    pstart, blk_exp, n_used = _moe_layout(cnt[:, 0], n_blocks)
    pos = _positions(pstart, eidx, rank)
    per = d // (2 * LANES)
    xs = _dispatch(pos, hp, n_blocks * MOE_BLK, per)
    ys = _experts(blk_exp, n_used, xs, wg, wu, wd, layer, per)
    return _combine(pos, x2, seq, g2, h, wts.T, sg.astype(BF16), su.astype(BF16), sd.astype(BF16), ys, per)


def kernel(x, c, ada_w, ada_b, norm1_g, w_in, hgrn_lb_logits, hgrn_onorm_g, conv_w, rel_bias, w_branch_a, w_branch_b, w_branch_c, w_out, norm2_g, router_w, router_bias, moe_w_gate, moe_w_up, moe_w_down, shared_w_gate, shared_w_up, shared_w_down, final_g):
    bsz, seq, d = x.shape
    depth = ada_w.shape[0]
    mod = _modulation(c, ada_w, ada_b)
    lbs = jnp.cumsum(jax.nn.softmax(hgrn_lb_logits.astype(F32), axis=0), axis=0)
    lbs = lbs - lbs[0]
    x2 = x.reshape(bsz * seq, d)
    for l in range(depth):
        x2 = _layer(x2, bsz, seq, l, mod[l], norm1_g[l], w_in, lbs[l], hgrn_onorm_g[l], conv_w[l], rel_bias[l],
                    w_branch_a[l], w_branch_b[l], w_branch_c[l], w_out[l], norm2_g[l], router_w[l], router_bias[l],
                    moe_w_gate, moe_w_up, moe_w_down, shared_w_gate[l], shared_w_up[l], shared_w_down[l])
    return _final_norm(x2, final_g).reshape(bsz, seq, d)
```

```python
import functools

import numpy as np
import jax
import jax.numpy as jnp
from jax import lax
from jax.experimental import pallas as pl
from jax.experimental.pallas import tpu as pltpu

F32 = jnp.float32
BF16 = jnp.bfloat16
I32 = jnp.int32
U32 = jnp.uint32

CHUNK = 64
EPS = 1e-6
A_HEADS = 6
A_HEAD_DIM = 128
A_DIM = A_HEADS * A_HEAD_DIM
B_WIDTH = 512
CONV_W = 3
C_HEADS = 6
C_HEAD_DIM = 128
C_DIM = C_HEADS * C_HEAD_DIM
LEFT_CHUNKS = 8
REL_CLIP = 256
N_EXPERTS = 64
TOP_K = 8
N_GROUPS = 8
GROUP_SIZE = N_EXPERTS // N_GROUPS
TOPK_GROUPS = 4
ROUTED_SCALE = 2.5
N_MOD = 6

V7X_VMEM_BYTES = 64 * 1024 * 1024
LANES = 128
NEG_BIG = -1e30

SUB = 16
FACTOR_SAFE_LOG = 80.0
MID = CHUNK // 2
MOE_BLK = 512
ATT_TQ = 256


def _vmem_limit(nbytes):
    return int(min(max(nbytes * 3 // 2, 16 * 1024 * 1024), V7X_VMEM_BYTES - 8 * 1024 * 1024))


def _cparams(sem, nbytes):
    return pltpu.CompilerParams(dimension_semantics=sem, vmem_limit_bytes=_vmem_limit(nbytes))


def _sigmoid(x):
    return 1.0 / (1.0 + jnp.exp(-x))


def _silu(x):
    return x * _sigmoid(x)


def _split3(x):
    hi = x.astype(BF16)
    r1 = x - hi.astype(F32)
    mid = r1.astype(BF16)
    lo = (r1 - mid.astype(F32)).astype(BF16)
    return hi, mid, lo


def _pack_bf16_pairs(x):
    n = x.shape[1] // 2
    xb = x.astype(BF16).astype(F32)
    lo = lax.bitcast_convert_type(xb[:, :n], U32) >> 16
    hi = lax.bitcast_convert_type(xb[:, n:], U32) & jnp.uint32(0xFFFF0000)
    return lo | hi


def _unpack_bf16_pairs(p):
    lo = lax.bitcast_convert_type(p << 16, F32).astype(BF16)
    hi = lax.bitcast_convert_type(p & jnp.uint32(0xFFFF0000), F32).astype(BF16)
    return lo, hi


def _store_token_tiles(ref, first_token, mat):
    n, w = mat.shape
    per = w // LANES
    for s in range(per):
        ref[pl.ds(first_token * per + s, n, stride=per), :] = mat[:, s * LANES:(s + 1) * LANES]


def _load_token_tiles(ref, first_token, n, per):
    return jnp.concatenate([ref[pl.ds(first_token * per + s, n, stride=per), :] for s in range(per)], axis=1)


def _mod_kernel(c_ref, w_ref, b_ref, o_ref):
    c = c_ref[...]
    o_ref[0] = jnp.dot(_silu(c), w_ref[0], preferred_element_type=F32) + b_ref[0]


def _modulation(c, ada_w, ada_b):
    depth, d, n = ada_w.shape
    bsz = c.shape[0]
    rows = 8
    c_pad = jnp.zeros((rows, d), F32).at[:bsz].set(c)
    tn = 1536 if n % 1536 == 0 else n
    out = pl.pallas_call(
        _mod_kernel,
        grid=(depth, n // tn),
        in_specs=[
            pl.BlockSpec((rows, d), lambda l, j: (0, 0)),
            pl.BlockSpec((1, d, tn), lambda l, j: (l, 0, j)),
            pl.BlockSpec((1, 1, tn), lambda l, j: (l, 0, j)),
        ],
        out_specs=pl.BlockSpec((1, rows, tn), lambda l, j: (l, 0, j)),
        out_shape=jax.ShapeDtypeStruct((depth, rows, n), F32),
        compiler_params=_cparams(("arbitrary", "arbitrary"), 2 * d * tn * 4 + 4 * rows * tn * 4),
        name="adaln_mod",
    )(c_pad, ada_w, ada_b.reshape(depth, 1, n))
    return out[:, :bsz]


def _norm_kernel(x_ref, g_ref, sc_ref, sh_ref, h_ref):
    x = x_ref[...]
    y = x * lax.rsqrt(jnp.mean(x * x, axis=-1, keepdims=True) + EPS)
    h_ref[...] = (y * g_ref[...] * (1.0 + sc_ref[0]) + sh_ref[0]).astype(BF16)


def _norm_mod(x2, seq, g, sc, sh):
    t, d = x2.shape
    tm = min(512, seq)
    tiles_per_batch = seq // tm
    return pl.pallas_call(
        _norm_kernel,
        grid=(t // tm,),
        in_specs=[
            pl.BlockSpec((tm, d), lambda i: (i, 0)),
            pl.BlockSpec((1, d), lambda i: (0, 0)),
            pl.BlockSpec((1, 1, d), lambda i: (i // tiles_per_batch, 0, 0)),
            pl.BlockSpec((1, 1, d), lambda i: (i // tiles_per_batch, 0, 0)),
        ],
        out_specs=pl.BlockSpec((tm, d), lambda i: (i, 0)),
        out_shape=jax.ShapeDtypeStruct((t, d), BF16),
        compiler_params=_cparams(("arbitrary",), 6 * tm * d * 4),
        name="norm_mod",
    )(x2, g.reshape(1, d), sc, sh)


def _proj_kernel(cmap_ref, h_ref, w_ref, o_ref, w_scr, *, act):
    @pl.when(pl.program_id(1) == 0)
    def _():
        w_scr[...] = w_ref[0].astype(BF16)

    acc = jnp.dot(h_ref[...], w_scr[...], preferred_element_type=F32)
    if act == "sigmoid":
        acc = _sigmoid(acc)
    o_ref[...] = acc.astype(o_ref.dtype)


def _project(h, w_all, layer, col_blocks, act, out_dtype, name):
    t, d = h.shape
    tn = A_DIM
    nj = len(col_blocks)
    tm = min(1024, t)
    osz = jnp.dtype(out_dtype).itemsize
    est = 2 * tm * d * 2 + 2 * d * tn * 4 + d * tn * 2 + 2 * tm * tn * osz + 2 * tm * tn * 4
    return pl.pallas_call(
        functools.partial(_proj_kernel, act=act),
        grid_spec=pltpu.PrefetchScalarGridSpec(
            num_scalar_prefetch=1,
            grid=(nj, t // tm),
            in_specs=[
                pl.BlockSpec((tm, d), lambda j, i, cm: (i, 0)),
                pl.BlockSpec((1, d, tn), lambda j, i, cm: (layer, 0, cm[j])),
            ],
            out_specs=pl.BlockSpec((tm, tn), lambda j, i, cm: (i, j)),
            scratch_shapes=[pltpu.VMEM((d, tn), BF16)],
        ),
        out_shape=jax.ShapeDtypeStruct((t, nj * tn), out_dtype),
        compiler_params=_cparams(("arbitrary", "arbitrary"), est),
        name=name,
    )(jnp.asarray(col_blocks, I32), h, w_all)


def _hgrn_kernel(aq_ref, af_ref, ai_ref, ag_ref, loglb_ref, log1mlb_ref, og_ref, sel_ref, o_ref,
                 st_scr, lf_scr, q_scr, k_scr, v_scr, b_scr, p_scr, *, n_chunks):
    @pl.when(pl.program_id(1) == 0)
    def _():
        st_scr[...] = jnp.zeros_like(st_scr)

    hd = A_HEAD_DIM
    n_sub = CHUNK // SUB
    nt = (((1,), (1,)), ((), ()))
    row_c = lax.broadcasted_iota(I32, (CHUNK, CHUNK), 0)
    col_c = lax.broadcasted_iota(I32, (CHUNK, CHUNK), 1)
    tri = (col_c <= row_c).astype(BF16)
    sub_row = lax.broadcasted_iota(I32, (SUB, 1), 0)
    chunk_row = lax.broadcasted_iota(I32, (CHUNK, 1), 0)
    log_lb = loglb_ref[...]
    log1m_lb = log1mlb_ref[...]
    og = og_ref[...]

    def log_f_body(ci, lowest):
        rows = pl.ds(pl.multiple_of(ci * CHUNK, CHUNK), CHUNK)
        af = af_ref[rows, :]
        ls = jnp.minimum(af, 0.0) - jnp.log(1.0 + jnp.exp(-jnp.abs(af)))
        t2 = log1m_lb + ls
        lf = jnp.maximum(log_lb, t2) + jnp.log(1.0 + jnp.exp(-jnp.abs(log_lb - t2)))
        lf_scr[rows, :] = lf
        first = jnp.sum(jnp.where(chunk_row <= MID, lf, 0.0), axis=0, keepdims=True)
        second = jnp.sum(jnp.where(chunk_row >= MID, lf, 0.0), axis=0, keepdims=True)
        return jnp.minimum(lowest, jnp.minimum(first, second))

    lowest_sum = lax.fori_loop(0, n_chunks, log_f_body, jnp.zeros((1, A_DIM), F32), unroll=2)
    safe = jnp.min(lowest_sum) > -FACTOR_SAFE_LOG

    def prep(ci):
        r0 = pl.multiple_of(ci * CHUNK, CHUNK)
        rows = pl.ds(r0, CHUNK)
        lf = lf_scr[rows, :]
        hi, mid, lo = _split3(lf)
        b = (jnp.dot(tri, hi, preferred_element_type=F32) + jnp.dot(tri, mid, preferred_element_type=F32)
             + jnp.dot(tri, lo, preferred_element_type=F32))
        q = aq_ref[rows, :].astype(F32) * (A_HEAD_DIM ** -0.5)
        return rows, 1.0 - jnp.exp(lf), b, q, ai_ref[rows, :]

    def head_tail(h, o, kh, vh, bh, bl):
        ktil = (kh * jnp.exp(bl - bh)).astype(BF16)
        upd = lax.dot_general(vh, ktil, (((0,), (0,)), ((), ())), preferred_element_type=F32)
        st_scr[h] = st_scr[h] * jnp.exp(bl) + upd
        return o * lax.rsqrt(jnp.mean(o * o, axis=-1, keepdims=True) + EPS)

    def finish(rows, outs):
        o_all = jnp.concatenate(outs, axis=1)
        gate = ag_ref[rows, :].astype(F32)
        o_ref[rows, :] = (o_all * og * _silu(gate)).astype(BF16)

    def fast_chunk(ci, carry):
        rows, k_all, b_all, q_all, v_all = prep(ci)
        qt_all = (q_all * jnp.exp(b_all)).astype(BF16)
        b_mid = b_all[MID:MID + 1]
        qm_all = (q_all * jnp.exp(b_all - b_mid)).astype(BF16)
        km_all = (k_all * jnp.exp(b_mid - b_all)).astype(BF16)
        b_last = b_all[CHUNK - 1:CHUNK]
        outs = []
        for h in range(A_HEADS):
            cols = slice(h * hd, (h + 1) * hd)
            vh = v_all[:, cols]
            qt = qt_all[:, cols]
            a = lax.dot_general(qm_all[:, cols], km_all[:, cols], nt, preferred_element_type=F32)
            a = jnp.where(col_c <= row_c, a, 0.0).astype(BF16)
            o = (jnp.dot(a, vh, preferred_element_type=F32)
                 + lax.dot_general(qt, st_scr[h].astype(BF16), nt, preferred_element_type=F32))
            outs.append(head_tail(h, o, k_all[:, cols], vh, b_all[:, cols], b_last[:, cols]))
        finish(rows, outs)
        return carry

    def exact_chunk(ci, carry):
        rows, k_new, b_new, q_new, v_new = prep(ci)
        k_scr[...] = k_new
        b_scr[...] = b_new
        q_scr[...] = q_new
        v_scr[...] = v_new.astype(F32)

        for i in range(n_sub):
            s0 = i * SUB
            b_blk = b_scr[pl.ds(s0, SUB), :]
            q_blk = q_scr[pl.ds(s0, SUB), :]
            for s in range(SUB):
                b_s = b_scr[pl.ds(s0 + s, 1), :]
                k_s = k_scr[pl.ds(s0 + s, 1), :]
                e = jnp.exp(jnp.where(sub_row >= s, b_blk - b_s, NEG_BIG))
                pv = (q_blk * k_s * e).astype(BF16)
                for h in range(A_HEADS):
                    p_scr[pl.ds((i * A_HEADS + h) * SUB, SUB), s * hd:(s + 1) * hd] = pv[:, h * hd:(h + 1) * hd]
        diag = jnp.dot(p_scr[...], sel_ref[...], preferred_element_type=F32)

        b_all = b_scr[...]
        q_all = q_scr[...]
        k_all = k_scr[...]
        qhat, khat = [], []
        for i in range(1, n_sub):
            s0 = i * SUB
            beta = b_scr[pl.ds(s0 - 1, 1), :]
            qhat.append((q_all[s0:s0 + SUB] * jnp.exp(b_all[s0:s0 + SUB] - beta)).astype(BF16))
            khat.append(jnp.where(chunk_row < s0, k_all * jnp.exp(jnp.minimum(beta - b_all, 0.0)), 0.0).astype(BF16))
        zero_blk = jnp.zeros((SUB, hd), BF16)

        b_last = b_scr[pl.ds(CHUNK - 1, 1), :]
        outs = []
        for h in range(A_HEADS):
            cols = slice(h * hd, (h + 1) * hd)
            qh = q_all[:, cols]
            kh = k_all[:, cols]
            vh = v_scr[:, cols].astype(BF16)
            bh = b_all[:, cols]
            st = st_scr[h]
            lhs_rows = [jnp.concatenate([zero_blk] * (n_sub - 1), axis=1)]
            for i in range(1, n_sub):
                parts = [zero_blk] * (n_sub - 1)
                parts[i - 1] = qhat[i - 1][:, cols]
                lhs_rows.append(jnp.concatenate(parts, axis=1))
            lhs = jnp.concatenate(lhs_rows, axis=0)
            rhs = jnp.concatenate([kk[:, cols] for kk in khat], axis=1)
            a_off = lax.dot_general(lhs, rhs, nt, preferred_element_type=F32)
            a_diag = jnp.concatenate(
                [pltpu.roll(diag[(i * A_HEADS + h) * SUB:(i * A_HEADS + h + 1) * SUB], i * SUB, 1) if i else
                 diag[h * SUB:(h + 1) * SUB] for i in range(n_sub)], axis=0)
            a = (a_diag[:, :CHUNK] + a_off).astype(BF16)
            qt = (qh * jnp.exp(bh)).astype(BF16)
            o = (jnp.dot(a, vh, preferred_element_type=F32)
                 + lax.dot_general(qt, st.astype(BF16), nt, preferred_element_type=F32))
            outs.append(head_tail(h, o, kh, vh, bh, b_last[:, cols]))
        finish(rows, outs)
        return carry

    @pl.when(safe)
    def _():
        lax.fori_loop(0, n_chunks, fast_chunk, 0, unroll=8)

    @pl.when(jnp.logical_not(safe))
    def _():
        lax.fori_loop(0, n_chunks, exact_chunk, 0)


def _hgrn(p, af, log_lb, log1m_lb, onorm_g, bsz, seq, aq_blk):
    t = p.shape[0]
    ts = min(512, seq)
    n_seq_tiles = seq // ts
    n_chunks = ts // CHUNK
    kern = functools.partial(_hgrn_kernel, n_chunks=n_chunks)
    row = lambda b, i: b * n_seq_tiles + i
    est = (2 * 3 * ts * A_DIM * 2 + 2 * ts * A_DIM * 4 + 2 * ts * A_DIM * 2 + A_HEADS * 128 * 128 * 4
           + 4 * CHUNK * A_DIM * 4 + CHUNK * SUB * A_DIM * 6 + 3 * ts * A_DIM * 4)
    sel_rows = np.arange(SUB * A_HEAD_DIM)
    selector = jnp.asarray(sel_rows[:, None] // A_HEAD_DIM == np.arange(A_HEAD_DIM)[None, :], BF16)
    return pl.pallas_call(
        kern,
        grid=(bsz, n_seq_tiles),
        in_specs=[
            pl.BlockSpec((ts, A_DIM), lambda b, i: (row(b, i), aq_blk)),
            pl.BlockSpec((ts, A_DIM), lambda b, i: (row(b, i), 0)),
            pl.BlockSpec((ts, A_DIM), lambda b, i: (row(b, i), aq_blk + 1)),
            pl.BlockSpec((ts, A_DIM), lambda b, i: (row(b, i), aq_blk + 2)),
            pl.BlockSpec((1, A_DIM), lambda b, i: (0, 0)),
            pl.BlockSpec((1, A_DIM), lambda b, i: (0, 0)),
            pl.BlockSpec((1, A_DIM), lambda b, i: (0, 0)),
            pl.BlockSpec((SUB * A_HEAD_DIM, A_HEAD_DIM), lambda b, i: (0, 0)),
        ],
        out_specs=pl.BlockSpec((ts, A_DIM), lambda b, i: (row(b, i), 0)),
        out_shape=jax.ShapeDtypeStruct((t, A_DIM), BF16),
        scratch_shapes=[
            pltpu.VMEM((A_HEADS, A_HEAD_DIM, A_HEAD_DIM), F32),
            pltpu.VMEM((ts, A_DIM), F32),
            pltpu.VMEM((CHUNK, A_DIM), F32),
            pltpu.VMEM((CHUNK, A_DIM), F32),
            pltpu.VMEM((CHUNK, A_DIM), F32),
            pltpu.VMEM((CHUNK, A_DIM), F32),
            pltpu.VMEM((CHUNK * A_HEADS, SUB * A_HEAD_DIM), BF16),
        ],
        compiler_params=_cparams(("arbitrary", "arbitrary"), est),
        name="hgrn2_scan",
    )(p, af, p, p, log_lb.reshape(1, A_DIM), log1m_lb.reshape(1, A_DIM), onorm_g.reshape(1, A_DIM), selector)


def _attn_kernel(q_ref, k0_ref, k1_ref, k2_ref, v0_ref, v1_ref, v2_ref, bias_ref, o_ref):
    i = pl.program_id(1)
    tq = q_ref.shape[0]
    hd = C_HEAD_DIM
    scale = C_HEAD_DIM ** -0.5
    col = lax.broadcasted_iota(I32, (1, 3 * tq), 1)
    valid = (col >= 2 * tq) | ((col >= tq) & (i >= 1)) | (i >= 2)
    for h in range(C_HEADS):
        cols = slice(h * hd, (h + 1) * hd)
        kh = jnp.concatenate([k0_ref[:, cols], k1_ref[:, cols], k2_ref[:, cols]], axis=0)
        vh = jnp.concatenate([v0_ref[:, cols], v1_ref[:, cols], v2_ref[:, cols]], axis=0)
        s = lax.dot_general(q_ref[:, cols], kh, (((1,), (1,)), ((), ())), preferred_element_type=F32)
        s = jnp.where(valid, s * scale + bias_ref[h], NEG_BIG)
        e = jnp.exp(s - jnp.max(s, axis=-1, keepdims=True))
        inv = 1.0 / jnp.sum(e, axis=-1, keepdims=True)
        o_ref[:, cols] = (jnp.dot(e.astype(BF16), vh, preferred_element_type=F32) * inv).astype(BF16)


def _attention_bias_table(rel_bias, tq):
    band = (LEFT_CHUNKS + 1) * CHUNK
    assert tq * 2 == LEFT_CHUNKS * CHUNK, "three key slots of tq rows must cover the band"
    diff = np.arange(CHUNK + band - 1) - (band - 1)
    ext = rel_bias.astype(F32)[:, np.clip(diff + LEFT_CHUNKS * CHUNK, -REL_CLIP, REL_CLIP) + REL_CLIP]
    rev = ext[:, ::-1]
    chunk_tab = jnp.stack([rev[:, CHUNK - 1 - tt:CHUNK - 1 - tt + band] for tt in range(CHUNK)], axis=1)
    rows = [jnp.pad(chunk_tab, ((0, 0), (0, 0), (c * CHUNK, 3 * tq - band - c * CHUNK)), constant_values=NEG_BIG)
            for c in range(tq // CHUNK)]
    return jnp.concatenate(rows, axis=1)


def _attention(p, rel_bias, bsz, seq, q_blk):
    t = p.shape[0]
    tq = ATT_TQ
    nq = seq // tq
    bias = _attention_bias_table(rel_bias, tq)
    row = lambda b, i, back: b * nq + jnp.maximum(i - back, 0)
    kv_spec = lambda blk, back: pl.BlockSpec((tq, C_DIM), lambda b, i: (row(b, i, back), blk))
    est = 2 * 7 * tq * C_DIM * 2 + 2 * C_HEADS * tq * 3 * tq * 4 + 2 * tq * C_DIM * 2 + 6 * tq * 3 * tq * 4
    return pl.pallas_call(
        _attn_kernel,
        grid=(bsz, nq),
        in_specs=[
            pl.BlockSpec((tq, C_DIM), lambda b, i: (b * nq + i, q_blk)),
            kv_spec(q_blk + 1, 2), kv_spec(q_blk + 1, 1), kv_spec(q_blk + 1, 0),
            kv_spec(q_blk + 2, 2), kv_spec(q_blk + 2, 1), kv_spec(q_blk + 2, 0),
            pl.BlockSpec((C_HEADS, tq, 3 * tq), lambda b, i: (0, 0, 0)),
        ],
        out_specs=pl.BlockSpec((tq, C_DIM), lambda b, i: (b * nq + i, 0)),
        out_shape=jax.ShapeDtypeStruct((t, C_DIM), BF16),
        compiler_params=_cparams(("arbitrary", "arbitrary"), est),
        name="chunk_attention",
    )(p, p, p, p, p, p, p, bias)


def _merge_kernel(x_ref, gmod_ref, a_ref, c_ref, bb_ref, bc_ref, bu_ref, hbc_ref, hbu_ref, cw_ref,
                  ga_ref, gb_ref, gc_ref, wa_ref, wb_ref, wc_ref, wo_ref, o_ref, *, tiles_per_batch):
    i = pl.program_id(0)
    tm = x_ref.shape[0]
    u = bc_ref[...].astype(F32) * bu_ref[...].astype(F32)
    halo = hbc_ref[...].astype(F32) * hbu_ref[...].astype(F32)
    halo = jnp.where(i % tiles_per_batch == 0, 0.0, halo)
    hrows = halo.shape[0]
    hm1 = halo[hrows - 1:hrows]
    hm2 = halo[hrows - 2:hrows - 1]
    r = lax.broadcasted_iota(I32, (tm, 1), 0)
    u1 = jnp.where(r == 0, hm1, pltpu.roll(u, 1, 0))
    u2 = jnp.where(r == 0, hm2, jnp.where(r == 1, hm1, pltpu.roll(u, 2, 0)))
    cw = cw_ref[...]
    conv = u2 * cw[0:1] + u1 * cw[1:2] + u * cw[2:3]
    bfeat = (bb_ref[...].astype(F32) * conv).astype(BF16)
    ya = jnp.dot(a_ref[...], wa_ref[...], preferred_element_type=F32)
    yb = jnp.dot(bfeat, wb_ref[...], preferred_element_type=F32)
    yc = jnp.dot(c_ref[...], wc_ref[...], preferred_element_type=F32)
    merged = (ga_ref[...].astype(F32) * ya + gb_ref[...].astype(F32) * yb + gc_ref[...].astype(F32) * yc)
    y = jnp.dot(merged.astype(BF16), wo_ref[...], preferred_element_type=F32)
    o_ref[...] = x_ref[...] + gmod_ref[0] * y


def _merge(x2, seq, gmod, a_feat, c_feat, p, pg, conv_w, wa, wb, wc, wo, b_blk):
    t, d = x2.shape
    tm = min(256, seq)
    tiles_per_batch = seq // tm
    halo = 16
    hb = tm // halo
    kern = functools.partial(_merge_kernel, tiles_per_batch=tiles_per_batch)
    const = lambda shape: pl.BlockSpec(shape, lambda i: (0,) * len(shape))
    halo_spec = lambda blk: pl.BlockSpec((halo, B_WIDTH), lambda i: (jnp.maximum(i * hb - 1, 0), blk))
    est = (4 * tm * d * 4 + 4 * tm * A_DIM * 2 + 6 * tm * B_WIDTH * 2 + 6 * tm * d * 2
           + 2 * (A_DIM + B_WIDTH + C_DIM + d) * d * 2 + 6 * tm * d * 4)
    return pl.pallas_call(
        kern,
        grid=(t // tm,),
        in_specs=[
            pl.BlockSpec((tm, d), lambda i: (i, 0)),
            pl.BlockSpec((1, 1, d), lambda i: (i // tiles_per_batch, 0, 0)),
            pl.BlockSpec((tm, A_DIM), lambda i: (i, 0)),
            pl.BlockSpec((tm, C_DIM), lambda i: (i, 0)),
            pl.BlockSpec((tm, B_WIDTH), lambda i: (i, b_blk)),
            pl.BlockSpec((tm, B_WIDTH), lambda i: (i, b_blk + 1)),
            pl.BlockSpec((tm, B_WIDTH), lambda i: (i, b_blk + 2)),
            halo_spec(b_blk + 1),
            halo_spec(b_blk + 2),
            const((CONV_W, B_WIDTH)),
            pl.BlockSpec((tm, d), lambda i: (i, 0)),
            pl.BlockSpec((tm, d), lambda i: (i, 1)),
            pl.BlockSpec((tm, d), lambda i: (i, 2)),
            const((A_DIM, d)), const((B_WIDTH, d)), const((C_DIM, d)), const((d, d)),
        ],
        out_specs=pl.BlockSpec((tm, d), lambda i: (i, 0)),
        out_shape=jax.ShapeDtypeStruct((t, d), F32),
        compiler_params=_cparams(("arbitrary",), est),
        name="merge_outproj",
    )(x2, gmod, a_feat, c_feat, p, p, p, p, p, conv_w, pg, pg, pg, wa, wb, wc, wo)


def _first_max(vals, iota, n, axis):
    m = jnp.max(vals, axis=axis, keepdims=True)
    idx = jnp.min(jnp.where(vals == m, iota, n), axis=axis, keepdims=True)
    return m, idx


def _router_kernel(x_ref, g_ref, sc_ref, sh_ref, rwt_ref, rb_ref, h_ref, hp_ref, e_ref, w_ref, rank_ref, cnt_ref,
                   cnt_scr):
    step = pl.program_id(0)

    @pl.when(step == 0)
    def _():
        cnt_scr[...] = jnp.zeros_like(cnt_scr)

    tm = x_ref.shape[0]
    x = x_ref[...]
    y = x * lax.rsqrt(jnp.mean(x * x, axis=-1, keepdims=True) + EPS)
    h = y * g_ref[...] * (1.0 + sc_ref[0]) + sh_ref[0]
    h_ref[...] = h.astype(BF16)
    _store_token_tiles(hp_ref, 0, _pack_bf16_pairs(h))

    nt = (((1,), (1,)), ((), ()))
    h_hi, h_mid, _ = _split3(h)
    w_hi, w_mid, _ = _split3(rwt_ref[...])
    logits = (lax.dot_general(w_hi, h_hi, nt, preferred_element_type=F32)
              + lax.dot_general(w_hi, h_mid, nt, preferred_element_type=F32)
              + lax.dot_general(w_mid, h_hi, nt, preferred_element_type=F32))
    scores = _sigmoid(logits)
    sel = scores + rb_ref[:, 0:1]

    g3 = sel.reshape(N_GROUPS, GROUP_SIZE, tm)
    j3 = lax.broadcasted_iota(I32, g3.shape, 1)
    m1, i1 = _first_max(g3, j3, GROUP_SIZE, 1)
    m2 = jnp.max(jnp.where(j3 == i1, -jnp.inf, g3), axis=1, keepdims=True)
    gs = (m1 + m2).reshape(N_GROUPS, tm)
    gi = lax.broadcasted_iota(I32, gs.shape, 0)
    gmask = jnp.zeros(gs.shape, jnp.bool_)
    for _ in range(TOPK_GROUPS):
        _, idx = _first_max(gs, gi, N_GROUPS, 0)
        hit = gi == idx
        gmask = gmask | hit
        gs = jnp.where(hit, -jnp.inf, gs)
    emask = jnp.broadcast_to(gmask.reshape(N_GROUPS, 1, tm), (N_GROUPS, GROUP_SIZE, tm)).reshape(N_EXPERTS, tm)
    masked = jnp.where(emask, sel, -jnp.inf)

    ei = lax.broadcasted_iota(I32, (N_EXPERTS, tm), 0)
    chosen = jnp.zeros((N_EXPERTS, tm), F32)
    e_rows, w_rows = [], []
    for _ in range(TOP_K):
        _, idx = _first_max(masked, ei, N_EXPERTS, 0)
        hit = ei == idx
        e_rows.append(idx)
        w_rows.append(jnp.sum(jnp.where(hit, scores, 0.0), axis=0, keepdims=True))
        chosen = jnp.where(hit, 1.0, chosen)
        masked = jnp.where(hit, -jnp.inf, masked)
    eidx = jnp.concatenate(e_rows, axis=0)
    wts = jnp.concatenate(w_rows, axis=0)
    wts = wts / jnp.sum(wts, axis=0, keepdims=True) * ROUTED_SCALE
    e_ref[...] = eidx
    w_ref[...] = wts

    tr = lax.broadcasted_iota(I32, (tm, tm), 0)
    tc = lax.broadcasted_iota(I32, (tm, tm), 1)
    before = (tr < tc).astype(BF16)
    base = cnt_scr[:, 0:1]
    rank_all = jnp.dot(chosen.astype(BF16), before, preferred_element_type=F32) + base
    ranks = [jnp.sum(jnp.where(ei == e_rows[k], rank_all, 0.0), axis=0, keepdims=True) for k in range(TOP_K)]
    rank_ref[...] = jnp.concatenate(ranks, axis=0).astype(I32)
    total = base + jnp.sum(chosen, axis=1, keepdims=True)
    cnt_scr[...] = jnp.broadcast_to(total, cnt_scr.shape)
    cnt_ref[...] = jnp.broadcast_to(total, cnt_ref.shape).astype(I32)


def _router(x2, seq, g, sc, sh, router_wt, router_bias):
    t, d = x2.shape
    tm = min(512, seq)
    tiles_per_batch = seq // tm
    est = (2 * tm * d * 4 + 2 * tm * d * 2 + 2 * tm * d * 2 + 2 * N_EXPERTS * d * 4 + tm * tm * 2
           + 8 * tm * d * 4 + 16 * N_EXPERTS * tm * 4)
    return pl.pallas_call(
        _router_kernel,
        grid=(t // tm,),
        in_specs=[
            pl.BlockSpec((tm, d), lambda i: (i, 0)),
            pl.BlockSpec((1, d), lambda i: (0, 0)),
            pl.BlockSpec((1, 1, d), lambda i: (i // tiles_per_batch, 0, 0)),
            pl.BlockSpec((1, 1, d), lambda i: (i // tiles_per_batch, 0, 0)),
            pl.BlockSpec((N_EXPERTS, d), lambda i: (0, 0)),
            pl.BlockSpec((N_EXPERTS, LANES), lambda i: (0, 0)),
        ],
        out_specs=[
            pl.BlockSpec((tm, d), lambda i: (i, 0)),
            pl.BlockSpec((tm * d // (2 * LANES), LANES), lambda i: (i, 0)),
            pl.BlockSpec((TOP_K, tm), lambda i: (0, i)),
            pl.BlockSpec((TOP_K, tm), lambda i: (0, i)),
            pl.BlockSpec((TOP_K, tm), lambda i: (0, i)),
            pl.BlockSpec((N_EXPERTS, LANES), lambda i: (0, 0)),
        ],
        out_shape=[
            jax.ShapeDtypeStruct((t, d), BF16),
            jax.ShapeDtypeStruct((t * d // (2 * LANES), LANES), U32),
            jax.ShapeDtypeStruct((TOP_K, t), I32),
            jax.ShapeDtypeStruct((TOP_K, t), F32),
            jax.ShapeDtypeStruct((TOP_K, t), I32),
            jax.ShapeDtypeStruct((N_EXPERTS, LANES), I32),
        ],
        scratch_shapes=[pltpu.VMEM((N_EXPERTS, LANES), F32)],
        compiler_params=_cparams(("arbitrary",), est),
        name="norm_router",
    )(x2, g.reshape(1, d), sc, sh, router_wt, jnp.broadcast_to(router_bias.reshape(N_EXPERTS, 1), (N_EXPERTS, LANES)))


def _pos_kernel(start_ref, e_ref, rank_ref, pos_ref):
    e = e_ref[...]
    acc = rank_ref[...]
    for ex in range(N_EXPERTS):
        acc = acc + jnp.where(e == ex, start_ref[ex], 0)
    pos_ref[...] = acc


def _positions(pstart, eidx, rank):
    k, t = eidx.shape
    tl = min(4096, t)
    return pl.pallas_call(
        _pos_kernel,
        grid_spec=pltpu.PrefetchScalarGridSpec(
            num_scalar_prefetch=1,
            grid=(t // tl,),
            in_specs=[pl.BlockSpec((k, tl), lambda i, s: (0, i)), pl.BlockSpec((k, tl), lambda i, s: (0, i))],
            out_specs=pl.BlockSpec((k, tl), lambda i, s: (0, i)),
        ),
        out_shape=jax.ShapeDtypeStruct((k, t), I32),
        compiler_params=_cparams(("arbitrary",), 8 * k * tl * 4),
        name="moe_positions",
    )(pstart, eidx, rank)


def _dispatch_kernel(pos_ref, hp_ref, h_ref, sg_ref, su_ref, sd_ref, xs_ref, shared_ref, sem, *, tm, per):
    def copy(t, k):
        src = hp_ref.at[pl.ds(pl.multiple_of(t * per, per), per)]
        dst = xs_ref.at[pl.ds(pl.multiple_of(pos_ref[k, t] * per, per), per)]
        return pltpu.make_async_copy(src, dst, sem)

    def start(t, c):
        for k in range(TOP_K):
            copy(t, k).start(priority=k % 2)
        return c

    def wait(t, c):
        for k in range(TOP_K):
            copy(t, k).wait()
        return c

    lax.fori_loop(0, tm, start, 0, unroll=2)
    h = h_ref[...]
    hidden = (_silu(jnp.dot(h, sg_ref[...], preferred_element_type=F32))
              * jnp.dot(h, su_ref[...], preferred_element_type=F32)).astype(BF16)
    shared_ref[...] = jnp.dot(hidden, sd_ref[...], preferred_element_type=F32)
    lax.fori_loop(0, tm, wait, 0, unroll=2)


def _dispatch(pos, hp, h, sg, su, sd, n_rows, per):
    t, d = h.shape
    ds_ = sg.shape[1]
    tm = min(512, t)
    kern = functools.partial(_dispatch_kernel, tm=tm, per=per)
    const = lambda shape: pl.BlockSpec(shape, lambda i: (0,) * len(shape))
    est = 2 * tm * per * LANES * 4 + 2 * tm * d * 2 + 2 * 3 * d * ds_ * 2 + 2 * tm * d * 4 + tm * (4 * ds_ + d) * 4
    return pl.pallas_call(
        kern,
        grid=(t // tm,),
        in_specs=[
            pl.BlockSpec((TOP_K, tm), lambda i: (0, i), memory_space=pltpu.SMEM),
            pl.BlockSpec((tm * per, LANES), lambda i: (i, 0)),
            pl.BlockSpec((tm, d), lambda i: (i, 0)),
            const((d, ds_)), const((d, ds_)), const((ds_, d)),
        ],
        out_specs=[pl.BlockSpec(memory_space=pl.ANY), pl.BlockSpec((tm, d), lambda i: (i, 0))],
        out_shape=[jax.ShapeDtypeStruct((n_rows * per, LANES), U32), jax.ShapeDtypeStruct((t, d), F32)],
        scratch_shapes=[pltpu.SemaphoreType.DMA(())],
        compiler_params=pltpu.CompilerParams(dimension_semantics=("arbitrary",), has_side_effects=True,
                                             vmem_limit_bytes=_vmem_limit(est)),
        name="moe_dispatch",
    )(pos, hp, h, sg, su, sd)


def _expert_kernel(blk_exp_ref, n_used_ref, xs_ref, wg_ref, wu_ref, wd_ref, ys_ref, wg_scr, wu_scr, wd_scr, *, per):
    b = pl.program_id(0)
    new_expert = (b == 0) | (blk_exp_ref[b] != blk_exp_ref[jnp.maximum(b - 1, 0)])

    @pl.when(new_expert)
    def _():
        wg_scr[...] = wg_ref[0, 0].astype(BF16)
        wu_scr[...] = wu_ref[0, 0].astype(BF16)
        wd_scr[...] = wd_ref[0, 0].astype(BF16)

    @pl.when(b < n_used_ref[0])
    def _():
        lo, hi = _unpack_bf16_pairs(_load_token_tiles(xs_ref, 0, MOE_BLK, per))
        half = lo.shape[1]
        gate = (jnp.dot(lo, wg_scr[:half], preferred_element_type=F32)
                + jnp.dot(hi, wg_scr[half:], preferred_element_type=F32))
        up = (jnp.dot(lo, wu_scr[:half], preferred_element_type=F32)
              + jnp.dot(hi, wu_scr[half:], preferred_element_type=F32))
        hidden = (_silu(gate) * up).astype(BF16)
        _store_token_tiles(ys_ref, 0, _pack_bf16_pairs(jnp.dot(hidden, wd_scr[...], preferred_element_type=F32)))

    @pl.when(b >= n_used_ref[0])
    def _():
        ys_ref[...] = jnp.zeros_like(ys_ref)


def _experts(blk_exp, n_used, xs, wg, wu, wd, layer, per):
    half = per * LANES
    d = 2 * half
    de = wg.shape[3]
    nb = xs.shape[0] // (MOE_BLK * per)
    est = 4 * MOE_BLK * half * 4 + 2 * 3 * d * de * 4 + 3 * d * de * 2 + MOE_BLK * (2 * de + d) * 4
    return pl.pallas_call(
        functools.partial(_expert_kernel, per=per),
        grid_spec=pltpu.PrefetchScalarGridSpec(
            num_scalar_prefetch=2,
            grid=(nb,),
            in_specs=[
                pl.BlockSpec((MOE_BLK * per, LANES), lambda b, be, nu: (b, 0)),
                pl.BlockSpec((1, 1, d, de), lambda b, be, nu: (layer, be[b], 0, 0)),
                pl.BlockSpec((1, 1, d, de), lambda b, be, nu: (layer, be[b], 0, 0)),
                pl.BlockSpec((1, 1, de, d), lambda b, be, nu: (layer, be[b], 0, 0)),
            ],
            out_specs=pl.BlockSpec((MOE_BLK * per, LANES), lambda b, be, nu: (b, 0)),
            scratch_shapes=[pltpu.VMEM((d, de), BF16), pltpu.VMEM((d, de), BF16), pltpu.VMEM((de, d), BF16)],
        ),
        out_shape=jax.ShapeDtypeStruct(xs.shape, U32),
        compiler_params=_cparams(("arbitrary",), est),
        name="moe_experts",
    )(blk_exp, n_used, xs, wg, wu, wd)


def _combine_kernel(pos_ref, x_ref, gmod_ref, shared_ref, wt_ref, ng_ref, nsc_ref, nsh_ref, ys_ref, *rest,
                    tm, per, last):
    if last:
        hn_ref, yg_scr, sem = rest
    else:
        o_ref, hn_ref, yg_scr, sem = rest

    def copy(t, k):
        src = ys_ref.at[pl.ds(pl.multiple_of(pos_ref[k, t] * per, per), per)]
        dst = yg_scr.at[pl.ds(pl.multiple_of((k * tm + t) * per, per), per)]
        return pltpu.make_async_copy(src, dst, sem)

    def start(t, c):
        for k in range(TOP_K):
            copy(t, k).start(priority=k % 2)
        return c

    def wait(t, c):
        for k in range(TOP_K):
            copy(t, k).wait()
        return c

    lax.fori_loop(0, tm, start, 0, unroll=2)
    lax.fori_loop(0, tm, wait, 0, unroll=2)
    half = per * LANES
    acc_lo = shared_ref[:, :half]
    acc_hi = shared_ref[:, half:]
    wt = wt_ref[...]
    for k in range(TOP_K):
        packed = _load_token_tiles(yg_scr, k * tm, tm, per)
        wk = wt[:, k:k + 1]
        acc_lo = acc_lo + wk * lax.bitcast_convert_type(packed << 16, F32)
        acc_hi = acc_hi + wk * lax.bitcast_convert_type(packed & jnp.uint32(0xFFFF0000), F32)
    g = gmod_ref[0]
    x_lo = x_ref[:, :half] + g[:, :half] * acc_lo
    x_hi = x_ref[:, half:] + g[:, half:] * acc_hi
    if not last:
        o_ref[:, :half] = x_lo
        o_ref[:, half:] = x_hi
    ms = (jnp.sum(x_lo * x_lo, axis=-1, keepdims=True) + jnp.sum(x_hi * x_hi, axis=-1, keepdims=True)) / (2 * half)
    r = lax.rsqrt(ms + EPS)
    ng = ng_ref[...]
    nsc = nsc_ref[0]
    nsh = nsh_ref[0]
    hn_ref[:, :half] = (x_lo * r * ng[:, :half] * (1.0 + nsc[:, :half]) + nsh[:, :half]).astype(hn_ref.dtype)
    hn_ref[:, half:] = (x_hi * r * ng[:, half:] * (1.0 + nsc[:, half:]) + nsh[:, half:]).astype(hn_ref.dtype)


def _combine(pos, x2, seq, gmod, shared, wts_t, ys, per, next_g, next_sc, next_sh, last):
    t, d = x2.shape
    half = d // 2
    tm = min(256, seq)
    tiles_per_batch = seq // tm
    kern = functools.partial(_combine_kernel, tm=tm, per=per, last=last)
    const = lambda shape: pl.BlockSpec(shape, lambda i: (0,) * len(shape))
    est = 8 * tm * d * 4 + TOP_K * tm * half * 4 + 8 * tm * d * 4
    tile = pl.BlockSpec((tm, d), lambda i: (i, 0))
    per_batch = pl.BlockSpec((1, 1, d), lambda i: (i // tiles_per_batch, 0, 0))
    if last:
        out_specs, out_shape = [tile], [jax.ShapeDtypeStruct((t, d), F32)]
    else:
        out_specs = [tile, tile]
        out_shape = [jax.ShapeDtypeStruct((t, d), F32), jax.ShapeDtypeStruct((t, d), BF16)]
    outs = pl.pallas_call(
        kern,
        grid=(t // tm,),
        in_specs=[
            pl.BlockSpec((TOP_K, tm), lambda i: (0, i), memory_space=pltpu.SMEM),
            tile,
            per_batch,
            tile,
            pl.BlockSpec((tm, TOP_K), lambda i: (i, 0)),
            const((1, d)), per_batch, per_batch,
            pl.BlockSpec(memory_space=pl.ANY),
        ],
        out_specs=out_specs,
        out_shape=out_shape,
        scratch_shapes=[pltpu.VMEM((TOP_K * tm * per, LANES), U32), pltpu.SemaphoreType.DMA(())],
        compiler_params=_cparams(("arbitrary",), est),
        name="moe_combine",
    )(pos, x2, gmod, shared, wts_t, next_g.reshape(1, d), next_sc, next_sh, ys)
    return (None, outs[0]) if last else (outs[0], outs[1])


def _moe_layout(counts, n_blocks):
    padded = (counts + MOE_BLK - 1) // MOE_BLK * MOE_BLK
    pends = jnp.cumsum(padded)
    pstart = (pends - padded).astype(I32)
    blk_row = jnp.arange(n_blocks, dtype=I32)[:, None] * MOE_BLK
    blk_exp = jnp.minimum(jnp.sum((pends[None, :] <= blk_row).astype(I32), axis=1), N_EXPERTS - 1)
    n_used = (pends[-1:] // MOE_BLK).astype(I32)
    return pstart, blk_exp, n_used


def _layer(x2, h1, bsz, seq, layer, mods, next_norm, last, w_in, lb, onorm_g, conv_w, rel_bias, w_a, w_b, w_c, w_out,
           norm2_g, router_w, router_bias, wg, wu, wd, sg, su, sd):
    t, d = x2.shape
    sh1, sc1, g1, sh2, sc2, g2 = mods
    n_col_blocks = w_in.shape[2] // A_DIM
    n_mix_blocks = n_col_blocks - (3 * d) // A_DIM
    pg = _project(h1, w_in, layer, tuple(range(n_mix_blocks, n_col_blocks)), "sigmoid", BF16, "proj_gates")
    af = _project(h1, w_in, layer, (1,), "none", F32, "proj_forget")
    p = _project(h1, w_in, layer, (4, 5, 0, 2, 3, 6, 7, 8), "none", BF16, "proj_mixers")
    b_blk = 0
    aq_blk = (3 * B_WIDTH) // A_DIM
    q_blk = aq_blk + 3
    a_feat = _hgrn(p, af, jnp.log(lb), jnp.log1p(-lb), onorm_g, bsz, seq, aq_blk)
    c_feat = _attention(p, rel_bias, bsz, seq, q_blk)
    x2 = _merge(x2, seq, g1, a_feat, c_feat, p, pg, conv_w, w_a.astype(BF16), w_b.astype(BF16), w_c.astype(BF16),
                w_out.astype(BF16), b_blk)

    h, hp, eidx, wts, rank, cnt = _router(x2, seq, norm2_g, sc2, sh2, router_w.T, router_bias)
    n_blocks = (t * TOP_K) // MOE_BLK + N_EXPERTS
    pstart, blk_exp, n_used = _moe_layout(cnt[:, 0], n_blocks)
    pos = _positions(pstart, eidx, rank)
    per = d // (2 * LANES)
    xs, shared = _dispatch(pos, hp, h, sg.astype(BF16), su.astype(BF16), sd.astype(BF16), n_blocks * MOE_BLK, per)
    ys = _experts(blk_exp, n_used, xs, wg, wu, wd, layer, per)
    return _combine(pos, x2, seq, g2, shared, wts.T, ys, per, *next_norm, last)


def kernel(x, c, ada_w, ada_b, norm1_g, w_in, hgrn_lb_logits, hgrn_onorm_g, conv_w, rel_bias, w_branch_a, w_branch_b, w_branch_c, w_out, norm2_g, router_w, router_bias, moe_w_gate, moe_w_up, moe_w_down, shared_w_gate, shared_w_up, shared_w_down, final_g):
    bsz, seq, d = x.shape
    depth = ada_w.shape[0]
    mod = _modulation(c, ada_w, ada_b)
    lbs = jnp.cumsum(jax.nn.softmax(hgrn_lb_logits.astype(F32), axis=0), axis=0)
    lbs = lbs - lbs[0]
    mods = [[m.reshape(bsz, 1, d) for m in jnp.split(mod[l], N_MOD, axis=-1)] for l in range(depth)]
    x2 = x.reshape(bsz * seq, d)
    h1 = _norm_mod(x2, seq, norm1_g[0], mods[0][1], mods[0][0])
    no_mod = jnp.zeros((bsz, 1, d), F32)
    for l in range(depth):
        last = l == depth - 1
        next_norm = (final_g, no_mod, no_mod) if last else (norm1_g[l + 1], mods[l + 1][1], mods[l + 1][0])
        x2, h1 = _layer(x2, h1, bsz, seq, l, mods[l], next_norm, last, w_in, lbs[l], hgrn_onorm_g[l], conv_w[l],
                        rel_bias[l], w_branch_a[l], w_branch_b[l], w_branch_c[l], w_out[l], norm2_g[l], router_w[l],
                        router_bias[l], moe_w_gate, moe_w_up, moe_w_down, shared_w_gate[l], shared_w_up[l],
                        shared_w_down[l])
    return h1.reshape(bsz, seq, d)
```

```python
import functools

import numpy as np
import jax
import jax.numpy as jnp
from jax import lax
from jax.experimental import pallas as pl
from jax.experimental.pallas import tpu as pltpu

F32 = jnp.float32
BF16 = jnp.bfloat16
I32 = jnp.int32
U32 = jnp.uint32

CHUNK = 64
EPS = 1e-6
A_HEADS = 6
A_HEAD_DIM = 128
A_DIM = A_HEADS * A_HEAD_DIM
B_WIDTH = 512
CONV_W = 3
C_HEADS = 6
C_HEAD_DIM = 128
C_DIM = C_HEADS * C_HEAD_DIM
LEFT_CHUNKS = 8
REL_CLIP = 256
N_EXPERTS = 64
TOP_K = 8
N_GROUPS = 8
GROUP_SIZE = N_EXPERTS // N_GROUPS
TOPK_GROUPS = 4
ROUTED_SCALE = 2.5
N_MOD = 6

V7X_VMEM_BYTES = 64 * 1024 * 1024
LANES = 128
NEG_BIG = -1e30

SUB = 16
FACTOR_SAFE_LOG = 80.0
MID = CHUNK // 2
MOE_BLK = 512
ATT_TQ = 256


def _vmem_limit(nbytes):
    return int(min(max(nbytes * 3 // 2, 16 * 1024 * 1024), V7X_VMEM_BYTES - 8 * 1024 * 1024))


def _cparams(sem, nbytes):
    return pltpu.CompilerParams(dimension_semantics=sem, vmem_limit_bytes=_vmem_limit(nbytes))


def _sigmoid(x):
    return 1.0 / (1.0 + jnp.exp(-x))


def _silu(x):
    return x * _sigmoid(x)


def _split3(x):
    hi = x.astype(BF16)
    r1 = x - hi.astype(F32)
    mid = r1.astype(BF16)
    lo = (r1 - mid.astype(F32)).astype(BF16)
    return hi, mid, lo


def _pack_bf16_pairs(x):
    n = x.shape[1] // 2
    xb = x.astype(BF16).astype(F32)
    lo = lax.bitcast_convert_type(xb[:, :n], U32) >> 16
    hi = lax.bitcast_convert_type(xb[:, n:], U32) & jnp.uint32(0xFFFF0000)
    return lo | hi


def _unpack_bf16_pairs(p):
    lo = lax.bitcast_convert_type(p << 16, F32).astype(BF16)
    hi = lax.bitcast_convert_type(p & jnp.uint32(0xFFFF0000), F32).astype(BF16)
    return lo, hi


def _store_token_tiles(ref, first_token, mat):
    n, w = mat.shape
    per = w // LANES
    for s in range(per):
        ref[pl.ds(first_token * per + s, n, stride=per), :] = mat[:, s * LANES:(s + 1) * LANES]


def _load_token_tiles(ref, first_token, n, per):
    return jnp.concatenate([ref[pl.ds(first_token * per + s, n, stride=per), :] for s in range(per)], axis=1)


def _mod_kernel(c_ref, w_ref, b_ref, o_ref):
    c = c_ref[...]
    o_ref[0] = jnp.dot(_silu(c), w_ref[0], preferred_element_type=F32) + b_ref[0]


def _modulation(c, ada_w, ada_b):
    depth, d, n = ada_w.shape
    bsz = c.shape[0]
    rows = 8
    c_pad = jnp.zeros((rows, d), F32).at[:bsz].set(c)
    tn = 1536 if n % 1536 == 0 else n
    out = pl.pallas_call(
        _mod_kernel,
        grid=(depth, n // tn),
        in_specs=[
            pl.BlockSpec((rows, d), lambda l, j: (0, 0)),
            pl.BlockSpec((1, d, tn), lambda l, j: (l, 0, j)),
            pl.BlockSpec((1, 1, tn), lambda l, j: (l, 0, j)),
        ],
        out_specs=pl.BlockSpec((1, rows, tn), lambda l, j: (l, 0, j)),
        out_shape=jax.ShapeDtypeStruct((depth, rows, n), F32),
        compiler_params=_cparams(("arbitrary", "arbitrary"), 2 * d * tn * 4 + 4 * rows * tn * 4),
        name="adaln_mod",
    )(c_pad, ada_w, ada_b.reshape(depth, 1, n))
    return out[:, :bsz]


def _norm_kernel(x_ref, g_ref, sc_ref, sh_ref, h_ref):
    x = x_ref[...]
    y = x * lax.rsqrt(jnp.mean(x * x, axis=-1, keepdims=True) + EPS)
    h_ref[...] = (y * g_ref[...] * (1.0 + sc_ref[0]) + sh_ref[0]).astype(BF16)


def _norm_mod(x2, seq, g, sc, sh):
    t, d = x2.shape
    tm = min(512, seq)
    tiles_per_batch = seq // tm
    return pl.pallas_call(
        _norm_kernel,
        grid=(t // tm,),
        in_specs=[
            pl.BlockSpec((tm, d), lambda i: (i, 0)),
            pl.BlockSpec((1, d), lambda i: (0, 0)),
            pl.BlockSpec((1, 1, d), lambda i: (i // tiles_per_batch, 0, 0)),
            pl.BlockSpec((1, 1, d), lambda i: (i // tiles_per_batch, 0, 0)),
        ],
        out_specs=pl.BlockSpec((tm, d), lambda i: (i, 0)),
        out_shape=jax.ShapeDtypeStruct((t, d), BF16),
        compiler_params=_cparams(("arbitrary",), 6 * tm * d * 4),
        name="norm_mod",
    )(x2, g.reshape(1, d), sc, sh)


def _proj_kernel(cmap_ref, h_ref, w_ref, o_ref, w_scr, *, act):
    @pl.when(pl.program_id(1) == 0)
    def _():
        w_scr[...] = w_ref[0].astype(BF16)

    acc = jnp.dot(h_ref[...], w_scr[...], preferred_element_type=F32)
    if act == "sigmoid":
        acc = _sigmoid(acc)
    o_ref[...] = acc.astype(o_ref.dtype)


def _project(h, w_all, layer, col_blocks, act, out_dtype, name):
    t, d = h.shape
    tn = A_DIM
    nj = len(col_blocks)
    tm = min(1024, t)
    osz = jnp.dtype(out_dtype).itemsize
    est = 2 * tm * d * 2 + 2 * d * tn * 4 + d * tn * 2 + 2 * tm * tn * osz + 2 * tm * tn * 4
    return pl.pallas_call(
        functools.partial(_proj_kernel, act=act),
        grid_spec=pltpu.PrefetchScalarGridSpec(
            num_scalar_prefetch=1,
            grid=(nj, t // tm),
            in_specs=[
                pl.BlockSpec((tm, d), lambda j, i, cm: (i, 0)),
                pl.BlockSpec((1, d, tn), lambda j, i, cm: (layer, 0, cm[j])),
            ],
            out_specs=pl.BlockSpec((tm, tn), lambda j, i, cm: (i, j)),
            scratch_shapes=[pltpu.VMEM((d, tn), BF16)],
        ),
        out_shape=jax.ShapeDtypeStruct((t, nj * tn), out_dtype),
        compiler_params=_cparams(("arbitrary", "arbitrary"), est),
        name=name,
    )(jnp.asarray(col_blocks, I32), h, w_all)


def _hgrn_kernel(aq_ref, af_ref, ai_ref, ag_ref, loglb_ref, log1mlb_ref, og_ref, sel_ref, o_ref,
                 st_scr, lf_scr, q_scr, k_scr, v_scr, b_scr, p_scr, *, n_chunks):
    @pl.when(pl.program_id(1) == 0)
    def _():
        st_scr[...] = jnp.zeros_like(st_scr)

    hd = A_HEAD_DIM
    n_sub = CHUNK // SUB
    nt = (((1,), (1,)), ((), ()))
    row_c = lax.broadcasted_iota(I32, (CHUNK, CHUNK), 0)
    col_c = lax.broadcasted_iota(I32, (CHUNK, CHUNK), 1)
    tri = (col_c <= row_c).astype(BF16)
    sub_row = lax.broadcasted_iota(I32, (SUB, 1), 0)
    chunk_row = lax.broadcasted_iota(I32, (CHUNK, 1), 0)
    log_lb = loglb_ref[...]
    log1m_lb = log1mlb_ref[...]
    og = og_ref[...]

    def log_f_body(ci, lowest):
        rows = pl.ds(pl.multiple_of(ci * CHUNK, CHUNK), CHUNK)
        af = af_ref[rows, :]
        ls = jnp.minimum(af, 0.0) - jnp.log(1.0 + jnp.exp(-jnp.abs(af)))
        t2 = log1m_lb + ls
        lf = jnp.maximum(log_lb, t2) + jnp.log(1.0 + jnp.exp(-jnp.abs(log_lb - t2)))
        lf_scr[rows, :] = lf
        first = jnp.sum(jnp.where(chunk_row <= MID, lf, 0.0), axis=0, keepdims=True)
        second = jnp.sum(jnp.where(chunk_row >= MID, lf, 0.0), axis=0, keepdims=True)
        return jnp.minimum(lowest, jnp.minimum(first, second))

    lowest_sum = lax.fori_loop(0, n_chunks, log_f_body, jnp.zeros((1, A_DIM), F32), unroll=2)
    safe = jnp.min(lowest_sum) > -FACTOR_SAFE_LOG

    def prep(ci):
        r0 = pl.multiple_of(ci * CHUNK, CHUNK)
        rows = pl.ds(r0, CHUNK)
        lf = lf_scr[rows, :]
        hi, mid, lo = _split3(lf)
        b = (jnp.dot(tri, hi, preferred_element_type=F32) + jnp.dot(tri, mid, preferred_element_type=F32)
             + jnp.dot(tri, lo, preferred_element_type=F32))
        q = aq_ref[rows, :].astype(F32) * (A_HEAD_DIM ** -0.5)
        return rows, 1.0 - jnp.exp(lf), b, q, ai_ref[rows, :]

    def head_tail(h, o, kh, vh, bh, bl):
        ktil = (kh * jnp.exp(bl - bh)).astype(BF16)
        upd = lax.dot_general(vh, ktil, (((0,), (0,)), ((), ())), preferred_element_type=F32)
        st_scr[h] = st_scr[h] * jnp.exp(bl) + upd
        return o * lax.rsqrt(jnp.mean(o * o, axis=-1, keepdims=True) + EPS)

    def finish(rows, outs):
        o_all = jnp.concatenate(outs, axis=1)
        gate = ag_ref[rows, :].astype(F32)
        o_ref[rows, :] = (o_all * og * _silu(gate)).astype(BF16)

    def fast_chunk(ci, carry):
        rows, k_all, b_all, q_all, v_all = prep(ci)
        qt_all = (q_all * jnp.exp(b_all)).astype(BF16)
        b_mid = b_all[MID:MID + 1]
        qm_all = (q_all * jnp.exp(b_all - b_mid)).astype(BF16)
        km_all = (k_all * jnp.exp(b_mid - b_all)).astype(BF16)
        b_last = b_all[CHUNK - 1:CHUNK]
        outs = []
        for h in range(A_HEADS):
            cols = slice(h * hd, (h + 1) * hd)
            vh = v_all[:, cols]
            qt = qt_all[:, cols]
            a = lax.dot_general(qm_all[:, cols], km_all[:, cols], nt, preferred_element_type=F32)
            a = jnp.where(col_c <= row_c, a, 0.0).astype(BF16)
            o = (jnp.dot(a, vh, preferred_element_type=F32)
                 + lax.dot_general(qt, st_scr[h].astype(BF16), nt, preferred_element_type=F32))
            outs.append(head_tail(h, o, k_all[:, cols], vh, b_all[:, cols], b_last[:, cols]))
        finish(rows, outs)
        return carry

    def exact_chunk(ci, carry):
        rows, k_new, b_new, q_new, v_new = prep(ci)
        k_scr[...] = k_new
        b_scr[...] = b_new
        q_scr[...] = q_new
        v_scr[...] = v_new.astype(F32)

        for i in range(n_sub):
            s0 = i * SUB
            b_blk = b_scr[pl.ds(s0, SUB), :]
            q_blk = q_scr[pl.ds(s0, SUB), :]
            for s in range(SUB):
                b_s = b_scr[pl.ds(s0 + s, 1), :]
                k_s = k_scr[pl.ds(s0 + s, 1), :]
                e = jnp.exp(jnp.where(sub_row >= s, b_blk - b_s, NEG_BIG))
                pv = (q_blk * k_s * e).astype(BF16)
                for h in range(A_HEADS):
                    p_scr[pl.ds((i * A_HEADS + h) * SUB, SUB), s * hd:(s + 1) * hd] = pv[:, h * hd:(h + 1) * hd]
        diag = jnp.dot(p_scr[...], sel_ref[...], preferred_element_type=F32)

        b_all = b_scr[...]
        q_all = q_scr[...]
        k_all = k_scr[...]
        qhat, khat = [], []
        for i in range(1, n_sub):
            s0 = i * SUB
            beta = b_scr[pl.ds(s0 - 1, 1), :]
            qhat.append((q_all[s0:s0 + SUB] * jnp.exp(b_all[s0:s0 + SUB] - beta)).astype(BF16))
            khat.append(jnp.where(chunk_row < s0, k_all * jnp.exp(jnp.minimum(beta - b_all, 0.0)), 0.0).astype(BF16))
        zero_blk = jnp.zeros((SUB, hd), BF16)

        b_last = b_scr[pl.ds(CHUNK - 1, 1), :]
        outs = []
        for h in range(A_HEADS):
            cols = slice(h * hd, (h + 1) * hd)
            qh = q_all[:, cols]
            kh = k_all[:, cols]
            vh = v_scr[:, cols].astype(BF16)
            bh = b_all[:, cols]
            st = st_scr[h]
            lhs_rows = [jnp.concatenate([zero_blk] * (n_sub - 1), axis=1)]
            for i in range(1, n_sub):
                parts = [zero_blk] * (n_sub - 1)
                parts[i - 1] = qhat[i - 1][:, cols]
                lhs_rows.append(jnp.concatenate(parts, axis=1))
            lhs = jnp.concatenate(lhs_rows, axis=0)
            rhs = jnp.concatenate([kk[:, cols] for kk in khat], axis=1)
            a_off = lax.dot_general(lhs, rhs, nt, preferred_element_type=F32)
            a_diag = jnp.concatenate(
                [pltpu.roll(diag[(i * A_HEADS + h) * SUB:(i * A_HEADS + h + 1) * SUB], i * SUB, 1) if i else
                 diag[h * SUB:(h + 1) * SUB] for i in range(n_sub)], axis=0)
            a = (a_diag[:, :CHUNK] + a_off).astype(BF16)
            qt = (qh * jnp.exp(bh)).astype(BF16)
            o = (jnp.dot(a, vh, preferred_element_type=F32)
                 + lax.dot_general(qt, st.astype(BF16), nt, preferred_element_type=F32))
            outs.append(head_tail(h, o, kh, vh, bh, b_last[:, cols]))
        finish(rows, outs)
        return carry

    @pl.when(safe)
    def _():
        lax.fori_loop(0, n_chunks, fast_chunk, 0, unroll=8)

    @pl.when(jnp.logical_not(safe))
    def _():
        lax.fori_loop(0, n_chunks, exact_chunk, 0)


def _hgrn(p, af, log_lb, log1m_lb, onorm_g, bsz, seq, aq_blk):
    t = p.shape[0]
    ts = min(512, seq)
    n_seq_tiles = seq // ts
    n_chunks = ts // CHUNK
    kern = functools.partial(_hgrn_kernel, n_chunks=n_chunks)
    row = lambda b, i: b * n_seq_tiles + i
    est = (2 * 3 * ts * A_DIM * 2 + 2 * ts * A_DIM * 4 + 2 * ts * A_DIM * 2 + A_HEADS * 128 * 128 * 4
           + 4 * CHUNK * A_DIM * 4 + CHUNK * SUB * A_DIM * 6 + 3 * ts * A_DIM * 4)
    sel_rows = np.arange(SUB * A_HEAD_DIM)
    selector = jnp.asarray(sel_rows[:, None] // A_HEAD_DIM == np.arange(A_HEAD_DIM)[None, :], BF16)
    return pl.pallas_call(
        kern,
        grid=(bsz, n_seq_tiles),
        in_specs=[
            pl.BlockSpec((ts, A_DIM), lambda b, i: (row(b, i), aq_blk)),
            pl.BlockSpec((ts, A_DIM), lambda b, i: (row(b, i), 0)),
            pl.BlockSpec((ts, A_DIM), lambda b, i: (row(b, i), aq_blk + 1)),
            pl.BlockSpec((ts, A_DIM), lambda b, i: (row(b, i), aq_blk + 2)),
            pl.BlockSpec((1, A_DIM), lambda b, i: (0, 0)),
            pl.BlockSpec((1, A_DIM), lambda b, i: (0, 0)),
            pl.BlockSpec((1, A_DIM), lambda b, i: (0, 0)),
            pl.BlockSpec((SUB * A_HEAD_DIM, A_HEAD_DIM), lambda b, i: (0, 0)),
        ],
        out_specs=pl.BlockSpec((ts, A_DIM), lambda b, i: (row(b, i), 0)),
        out_shape=jax.ShapeDtypeStruct((t, A_DIM), BF16),
        scratch_shapes=[
            pltpu.VMEM((A_HEADS, A_HEAD_DIM, A_HEAD_DIM), F32),
            pltpu.VMEM((ts, A_DIM), F32),
            pltpu.VMEM((CHUNK, A_DIM), F32),
            pltpu.VMEM((CHUNK, A_DIM), F32),
            pltpu.VMEM((CHUNK, A_DIM), F32),
            pltpu.VMEM((CHUNK, A_DIM), F32),
            pltpu.VMEM((CHUNK * A_HEADS, SUB * A_HEAD_DIM), BF16),
        ],
        compiler_params=_cparams(("arbitrary", "arbitrary"), est),
        name="hgrn2_scan",
    )(p, af, p, p, log_lb.reshape(1, A_DIM), log1m_lb.reshape(1, A_DIM), onorm_g.reshape(1, A_DIM), selector)


def _attn_kernel(q_ref, k0_ref, k1_ref, k2_ref, v0_ref, v1_ref, v2_ref, bias_ref, o_ref):
    i = pl.program_id(1)
    tq = q_ref.shape[0]
    hd = C_HEAD_DIM
    scale = C_HEAD_DIM ** -0.5
    col = lax.broadcasted_iota(I32, (1, 3 * tq), 1)
    valid = (col >= 2 * tq) | ((col >= tq) & (i >= 1)) | (i >= 2)
    for h in range(C_HEADS):
        cols = slice(h * hd, (h + 1) * hd)
        kh = jnp.concatenate([k0_ref[:, cols], k1_ref[:, cols], k2_ref[:, cols]], axis=0)
        vh = jnp.concatenate([v0_ref[:, cols], v1_ref[:, cols], v2_ref[:, cols]], axis=0)
        s = lax.dot_general(q_ref[:, cols], kh, (((1,), (1,)), ((), ())), preferred_element_type=F32)
        s = jnp.where(valid, s * scale + bias_ref[h], NEG_BIG)
        e = jnp.exp(s - jnp.max(s, axis=-1, keepdims=True))
        inv = 1.0 / jnp.sum(e, axis=-1, keepdims=True)
        o_ref[:, cols] = (jnp.dot(e.astype(BF16), vh, preferred_element_type=F32) * inv).astype(BF16)


def _attention_bias_table(rel_bias, tq):
    band = (LEFT_CHUNKS + 1) * CHUNK
    assert tq * 2 == LEFT_CHUNKS * CHUNK, "three key slots of tq rows must cover the band"
    diff = np.arange(CHUNK + band - 1) - (band - 1)
    ext = rel_bias.astype(F32)[:, np.clip(diff + LEFT_CHUNKS * CHUNK, -REL_CLIP, REL_CLIP) + REL_CLIP]
    rev = ext[:, ::-1]
    chunk_tab = jnp.stack([rev[:, CHUNK - 1 - tt:CHUNK - 1 - tt + band] for tt in range(CHUNK)], axis=1)
    rows = [jnp.pad(chunk_tab, ((0, 0), (0, 0), (c * CHUNK, 3 * tq - band - c * CHUNK)), constant_values=NEG_BIG)
            for c in range(tq // CHUNK)]
    return jnp.concatenate(rows, axis=1)


def _attention(p, rel_bias, bsz, seq, q_blk):
    t = p.shape[0]
    tq = ATT_TQ
    nq = seq // tq
    bias = _attention_bias_table(rel_bias, tq)
    row = lambda b, i, back: b * nq + jnp.maximum(i - back, 0)
    kv_spec = lambda blk, back: pl.BlockSpec((tq, C_DIM), lambda b, i: (row(b, i, back), blk))
    est = 2 * 7 * tq * C_DIM * 2 + 2 * C_HEADS * tq * 3 * tq * 4 + 2 * tq * C_DIM * 2 + 6 * tq * 3 * tq * 4
    return pl.pallas_call(
        _attn_kernel,
        grid=(bsz, nq),
        in_specs=[
            pl.BlockSpec((tq, C_DIM), lambda b, i: (b * nq + i, q_blk)),
            kv_spec(q_blk + 1, 2), kv_spec(q_blk + 1, 1), kv_spec(q_blk + 1, 0),
            kv_spec(q_blk + 2, 2), kv_spec(q_blk + 2, 1), kv_spec(q_blk + 2, 0),
            pl.BlockSpec((C_HEADS, tq, 3 * tq), lambda b, i: (0, 0, 0)),
        ],
        out_specs=pl.BlockSpec((tq, C_DIM), lambda b, i: (b * nq + i, 0)),
        out_shape=jax.ShapeDtypeStruct((t, C_DIM), BF16),
        compiler_params=_cparams(("arbitrary", "arbitrary"), est),
        name="chunk_attention",
    )(p, p, p, p, p, p, p, bias)


def _merge_kernel(x_ref, gmod_ref, a_ref, c_ref, bb_ref, bc_ref, bu_ref, hbc_ref, hbu_ref, cw_ref,
                  ga_ref, gb_ref, gc_ref, wa_ref, wb_ref, wc_ref, wo_ref, o_ref, *, tiles_per_batch):
    i = pl.program_id(0)
    tm = x_ref.shape[0]
    u = bc_ref[...].astype(F32) * bu_ref[...].astype(F32)
    halo = hbc_ref[...].astype(F32) * hbu_ref[...].astype(F32)
    halo = jnp.where(i % tiles_per_batch == 0, 0.0, halo)
    hrows = halo.shape[0]
    hm1 = halo[hrows - 1:hrows]
    hm2 = halo[hrows - 2:hrows - 1]
    r = lax.broadcasted_iota(I32, (tm, 1), 0)
    u1 = jnp.where(r == 0, hm1, pltpu.roll(u, 1, 0))
    u2 = jnp.where(r == 0, hm2, jnp.where(r == 1, hm1, pltpu.roll(u, 2, 0)))
    cw = cw_ref[...]
    conv = u2 * cw[0:1] + u1 * cw[1:2] + u * cw[2:3]
    bfeat = (bb_ref[...].astype(F32) * conv).astype(BF16)
    ya = jnp.dot(a_ref[...], wa_ref[...], preferred_element_type=F32)
    yb = jnp.dot(bfeat, wb_ref[...], preferred_element_type=F32)
    yc = jnp.dot(c_ref[...], wc_ref[...], preferred_element_type=F32)
    merged = (ga_ref[...].astype(F32) * ya + gb_ref[...].astype(F32) * yb + gc_ref[...].astype(F32) * yc)
    y = jnp.dot(merged.astype(BF16), wo_ref[...], preferred_element_type=F32)
    o_ref[...] = x_ref[...] + gmod_ref[0] * y


def _merge(x2, seq, gmod, a_feat, c_feat, p, pg, conv_w, wa, wb, wc, wo, b_blk):
    t, d = x2.shape
    tm = min(256, seq)
    tiles_per_batch = seq // tm
    halo = 16
    hb = tm // halo
    kern = functools.partial(_merge_kernel, tiles_per_batch=tiles_per_batch)
    const = lambda shape: pl.BlockSpec(shape, lambda i: (0,) * len(shape))
    halo_spec = lambda blk: pl.BlockSpec((halo, B_WIDTH), lambda i: (jnp.maximum(i * hb - 1, 0), blk))
    est = (4 * tm * d * 4 + 4 * tm * A_DIM * 2 + 6 * tm * B_WIDTH * 2 + 6 * tm * d * 2
           + 2 * (A_DIM + B_WIDTH + C_DIM + d) * d * 2 + 6 * tm * d * 4)
    return pl.pallas_call(
        kern,
        grid=(t // tm,),
        in_specs=[
            pl.BlockSpec((tm, d), lambda i: (i, 0)),
            pl.BlockSpec((1, 1, d), lambda i: (i // tiles_per_batch, 0, 0)),
            pl.BlockSpec((tm, A_DIM), lambda i: (i, 0)),
            pl.BlockSpec((tm, C_DIM), lambda i: (i, 0)),
            pl.BlockSpec((tm, B_WIDTH), lambda i: (i, b_blk)),
            pl.BlockSpec((tm, B_WIDTH), lambda i: (i, b_blk + 1)),
            pl.BlockSpec((tm, B_WIDTH), lambda i: (i, b_blk + 2)),
            halo_spec(b_blk + 1),
            halo_spec(b_blk + 2),
            const((CONV_W, B_WIDTH)),
            pl.BlockSpec((tm, d), lambda i: (i, 0)),
            pl.BlockSpec((tm, d), lambda i: (i, 1)),
            pl.BlockSpec((tm, d), lambda i: (i, 2)),
            const((A_DIM, d)), const((B_WIDTH, d)), const((C_DIM, d)), const((d, d)),
        ],
        out_specs=pl.BlockSpec((tm, d), lambda i: (i, 0)),
        out_shape=jax.ShapeDtypeStruct((t, d), F32),
        compiler_params=_cparams(("arbitrary",), est),
        name="merge_outproj",
    )(x2, gmod, a_feat, c_feat, p, p, p, p, p, conv_w, pg, pg, pg, wa, wb, wc, wo)


def _first_max(vals, iota, n, axis):
    m = jnp.max(vals, axis=axis, keepdims=True)
    idx = jnp.min(jnp.where(vals == m, iota, n), axis=axis, keepdims=True)
    return m, idx


def _router_kernel(x_ref, g_ref, sc_ref, sh_ref, rwt_ref, rb_ref, h_ref, hp_ref, e_ref, w_ref, rank_ref, cnt_ref,
                   cnt_scr):
    step = pl.program_id(0)

    @pl.when(step == 0)
    def _():
        cnt_scr[...] = jnp.zeros_like(cnt_scr)

    tm = x_ref.shape[0]
    x = x_ref[...]
    y = x * lax.rsqrt(jnp.mean(x * x, axis=-1, keepdims=True) + EPS)
    h = y * g_ref[...] * (1.0 + sc_ref[0]) + sh_ref[0]
    h_ref[...] = h.astype(BF16)
    _store_token_tiles(hp_ref, 0, _pack_bf16_pairs(h))

    nt = (((1,), (1,)), ((), ()))
    h_hi, h_mid, _ = _split3(h)
    w_hi, w_mid, _ = _split3(rwt_ref[...])
    logits = (lax.dot_general(w_hi, h_hi, nt, preferred_element_type=F32)
              + lax.dot_general(w_hi, h_mid, nt, preferred_element_type=F32)
              + lax.dot_general(w_mid, h_hi, nt, preferred_element_type=F32))
    scores = _sigmoid(logits)
    sel = scores + rb_ref[:, 0:1]

    g3 = sel.reshape(N_GROUPS, GROUP_SIZE, tm)
    j3 = lax.broadcasted_iota(I32, g3.shape, 1)
    m1, i1 = _first_max(g3, j3, GROUP_SIZE, 1)
    m2 = jnp.max(jnp.where(j3 == i1, -jnp.inf, g3), axis=1, keepdims=True)
    gs = (m1 + m2).reshape(N_GROUPS, tm)
    gi = lax.broadcasted_iota(I32, gs.shape, 0)
    gmask = jnp.zeros(gs.shape, jnp.bool_)
    for _ in range(TOPK_GROUPS):
        _, idx = _first_max(gs, gi, N_GROUPS, 0)
        hit = gi == idx
        gmask = gmask | hit
        gs = jnp.where(hit, -jnp.inf, gs)
    emask = jnp.broadcast_to(gmask.reshape(N_GROUPS, 1, tm), (N_GROUPS, GROUP_SIZE, tm)).reshape(N_EXPERTS, tm)
    masked = jnp.where(emask, sel, -jnp.inf)

    ei = lax.broadcasted_iota(I32, (N_EXPERTS, tm), 0)
    chosen = jnp.zeros((N_EXPERTS, tm), F32)
    e_rows, w_rows = [], []
    for _ in range(TOP_K):
        _, idx = _first_max(masked, ei, N_EXPERTS, 0)
        hit = ei == idx
        e_rows.append(idx)
        w_rows.append(jnp.sum(jnp.where(hit, scores, 0.0), axis=0, keepdims=True))
        chosen = jnp.where(hit, 1.0, chosen)
        masked = jnp.where(hit, -jnp.inf, masked)
    eidx = jnp.concatenate(e_rows, axis=0)
    wts = jnp.concatenate(w_rows, axis=0)
    wts = wts / jnp.sum(wts, axis=0, keepdims=True) * ROUTED_SCALE
    e_ref[...] = eidx
    w_ref[...] = wts

    tr = lax.broadcasted_iota(I32, (tm, tm), 0)
    tc = lax.broadcasted_iota(I32, (tm, tm), 1)
    before = (tr < tc).astype(BF16)
    base = cnt_scr[:, 0:1]
    rank_all = jnp.dot(chosen.astype(BF16), before, preferred_element_type=F32) + base
    ranks = [jnp.sum(jnp.where(ei == e_rows[k], rank_all, 0.0), axis=0, keepdims=True) for k in range(TOP_K)]
    rank_ref[...] = jnp.concatenate(ranks, axis=0).astype(I32)
    total = base + jnp.sum(chosen, axis=1, keepdims=True)
    cnt_scr[...] = jnp.broadcast_to(total, cnt_scr.shape)
    cnt_ref[...] = jnp.broadcast_to(total, cnt_ref.shape).astype(I32)


def _router(x2, seq, g, sc, sh, router_wt, router_bias):
    t, d = x2.shape
    tm = min(512, seq)
    tiles_per_batch = seq // tm
    est = (2 * tm * d * 4 + 2 * tm * d * 2 + 2 * tm * d * 2 + 2 * N_EXPERTS * d * 4 + tm * tm * 2
           + 8 * tm * d * 4 + 16 * N_EXPERTS * tm * 4)
    return pl.pallas_call(
        _router_kernel,
        grid=(t // tm,),
        in_specs=[
            pl.BlockSpec((tm, d), lambda i: (i, 0)),
            pl.BlockSpec((1, d), lambda i: (0, 0)),
            pl.BlockSpec((1, 1, d), lambda i: (i // tiles_per_batch, 0, 0)),
            pl.BlockSpec((1, 1, d), lambda i: (i // tiles_per_batch, 0, 0)),
            pl.BlockSpec((N_EXPERTS, d), lambda i: (0, 0)),
            pl.BlockSpec((N_EXPERTS, LANES), lambda i: (0, 0)),
        ],
        out_specs=[
            pl.BlockSpec((tm, d), lambda i: (i, 0)),
            pl.BlockSpec((tm * d // (2 * LANES), LANES), lambda i: (i, 0)),
            pl.BlockSpec((TOP_K, tm), lambda i: (0, i)),
            pl.BlockSpec((TOP_K, tm), lambda i: (0, i)),
            pl.BlockSpec((TOP_K, tm), lambda i: (0, i)),
            pl.BlockSpec((N_EXPERTS, LANES), lambda i: (0, 0)),
        ],
        out_shape=[
            jax.ShapeDtypeStruct((t, d), BF16),
            jax.ShapeDtypeStruct((t * d // (2 * LANES), LANES), U32),
            jax.ShapeDtypeStruct((TOP_K, t), I32),
            jax.ShapeDtypeStruct((TOP_K, t), F32),
            jax.ShapeDtypeStruct((TOP_K, t), I32),
            jax.ShapeDtypeStruct((N_EXPERTS, LANES), I32),
        ],
        scratch_shapes=[pltpu.VMEM((N_EXPERTS, LANES), F32)],
        compiler_params=_cparams(("arbitrary",), est),
        name="norm_router",
    )(x2, g.reshape(1, d), sc, sh, router_wt, jnp.broadcast_to(router_bias.reshape(N_EXPERTS, 1), (N_EXPERTS, LANES)))


def _pos_kernel(start_ref, e_ref, rank_ref, pos_ref):
    e = e_ref[...]
    acc = rank_ref[...]
    for ex in range(N_EXPERTS):
        acc = acc + jnp.where(e == ex, start_ref[ex], 0)
    pos_ref[...] = acc


def _positions(pstart, eidx, rank):
    k, t = eidx.shape
    tl = min(4096, t)
    return pl.pallas_call(
        _pos_kernel,
        grid_spec=pltpu.PrefetchScalarGridSpec(
            num_scalar_prefetch=1,
            grid=(t // tl,),
            in_specs=[pl.BlockSpec((k, tl), lambda i, s: (0, i)), pl.BlockSpec((k, tl), lambda i, s: (0, i))],
            out_specs=pl.BlockSpec((k, tl), lambda i, s: (0, i)),
        ),
        out_shape=jax.ShapeDtypeStruct((k, t), I32),
        compiler_params=_cparams(("arbitrary",), 8 * k * tl * 4),
        name="moe_positions",
    )(pstart, eidx, rank)


def _dispatch_kernel(pos_ref, hp_ref, xs_ref, sem, *, tm, per):
    def copy(t, k):
        src = hp_ref.at[pl.ds(pl.multiple_of(t * per, per), per)]
        dst = xs_ref.at[pl.ds(pl.multiple_of(pos_ref[k, t] * per, per), per)]
        return pltpu.make_async_copy(src, dst, sem)

    def start(t, c):
        for k in range(TOP_K):
            copy(t, k).start(priority=k % 2)
        return c

    def wait(t, c):
        for k in range(TOP_K):
            copy(t, k).wait()
        return c

    lax.fori_loop(0, tm, start, 0, unroll=2)
    lax.fori_loop(0, tm, wait, 0, unroll=2)


def _dispatch(pos, hp, n_rows, per):
    t = hp.shape[0] // per
    tm = min(512, t)
    kern = functools.partial(_dispatch_kernel, tm=tm, per=per)
    return pl.pallas_call(
        kern,
        grid=(t // tm,),
        in_specs=[
            pl.BlockSpec((TOP_K, tm), lambda i: (0, i), memory_space=pltpu.SMEM),
            pl.BlockSpec((tm * per, LANES), lambda i: (i, 0)),
        ],
        out_specs=pl.BlockSpec(memory_space=pl.ANY),
        out_shape=jax.ShapeDtypeStruct((n_rows * per, LANES), U32),
        scratch_shapes=[pltpu.SemaphoreType.DMA(())],
        compiler_params=pltpu.CompilerParams(dimension_semantics=("arbitrary",), has_side_effects=True,
                                             vmem_limit_bytes=_vmem_limit(4 * tm * per * LANES * 4)),
        name="moe_dispatch",
    )(pos, hp)


def _expert_kernel(blk_exp_ref, n_used_ref, xs_ref, wg_ref, wu_ref, wd_ref, ys_ref, wg_scr, wu_scr, wd_scr, *, per):
    b = pl.program_id(0)
    new_expert = (b == 0) | (blk_exp_ref[b] != blk_exp_ref[jnp.maximum(b - 1, 0)])

    @pl.when(new_expert)
    def _():
        wg_scr[...] = wg_ref[0, 0].astype(BF16)
        wu_scr[...] = wu_ref[0, 0].astype(BF16)
        wd_scr[...] = wd_ref[0, 0].astype(BF16)

    @pl.when(b < n_used_ref[0])
    def _():
        lo, hi = _unpack_bf16_pairs(_load_token_tiles(xs_ref, 0, MOE_BLK, per))
        half = lo.shape[1]
        gate = (jnp.dot(lo, wg_scr[:half], preferred_element_type=F32)
                + jnp.dot(hi, wg_scr[half:], preferred_element_type=F32))
        up = (jnp.dot(lo, wu_scr[:half], preferred_element_type=F32)
              + jnp.dot(hi, wu_scr[half:], preferred_element_type=F32))
        hidden = (_silu(gate) * up).astype(BF16)
        _store_token_tiles(ys_ref, 0, _pack_bf16_pairs(jnp.dot(hidden, wd_scr[...], preferred_element_type=F32)))

    @pl.when(b >= n_used_ref[0])
    def _():
        ys_ref[...] = jnp.zeros_like(ys_ref)


def _experts(blk_exp, n_used, xs, wg, wu, wd, layer, per):
    half = per * LANES
    d = 2 * half
    de = wg.shape[3]
    nb = xs.shape[0] // (MOE_BLK * per)
    est = 4 * MOE_BLK * half * 4 + 2 * 3 * d * de * 4 + 3 * d * de * 2 + MOE_BLK * (2 * de + d) * 4
    return pl.pallas_call(
        functools.partial(_expert_kernel, per=per),
        grid_spec=pltpu.PrefetchScalarGridSpec(
            num_scalar_prefetch=2,
            grid=(nb,),
            in_specs=[
                pl.BlockSpec((MOE_BLK * per, LANES), lambda b, be, nu: (b, 0)),
                pl.BlockSpec((1, 1, d, de), lambda b, be, nu: (layer, be[b], 0, 0)),
                pl.BlockSpec((1, 1, d, de), lambda b, be, nu: (layer, be[b], 0, 0)),
                pl.BlockSpec((1, 1, de, d), lambda b, be, nu: (layer, be[b], 0, 0)),
            ],
            out_specs=pl.BlockSpec((MOE_BLK * per, LANES), lambda b, be, nu: (b, 0)),
            scratch_shapes=[pltpu.VMEM((d, de), BF16), pltpu.VMEM((d, de), BF16), pltpu.VMEM((de, d), BF16)],
        ),
        out_shape=jax.ShapeDtypeStruct(xs.shape, U32),
        compiler_params=_cparams(("arbitrary",), est),
        name="moe_experts",
    )(blk_exp, n_used, xs, wg, wu, wd)


def _combine_kernel(pos_ref, x_ref, gmod_ref, h_ref, wt_ref, sg_ref, su_ref, sd_ref, ng_ref, nsc_ref, nsh_ref, ys_ref,
                    *rest, tm, per, last):
    if last:
        hn_ref, yg_scr, sem = rest
    else:
        o_ref, hn_ref, yg_scr, sem = rest

    def copy(t, k):
        src = ys_ref.at[pl.ds(pl.multiple_of(pos_ref[k, t] * per, per), per)]
        dst = yg_scr.at[pl.ds(pl.multiple_of((k * tm + t) * per, per), per)]
        return pltpu.make_async_copy(src, dst, sem)

    def start(t, c):
        for k in range(TOP_K):
            copy(t, k).start(priority=k % 2)
        return c

    def wait(t, c):
        for k in range(TOP_K):
            copy(t, k).wait()
        return c

    lax.fori_loop(0, tm, start, 0, unroll=2)
    h = h_ref[...]
    hidden = (_silu(jnp.dot(h, sg_ref[...], preferred_element_type=F32))
              * jnp.dot(h, su_ref[...], preferred_element_type=F32)).astype(BF16)
    shared = jnp.dot(hidden, sd_ref[...], preferred_element_type=F32)
    lax.fori_loop(0, tm, wait, 0, unroll=2)
    half = per * LANES
    acc_lo = shared[:, :half]
    acc_hi = shared[:, half:]
    wt = wt_ref[...]
    for k in range(TOP_K):
        packed = _load_token_tiles(yg_scr, k * tm, tm, per)
        wk = wt[:, k:k + 1]
        acc_lo = acc_lo + wk * lax.bitcast_convert_type(packed << 16, F32)
        acc_hi = acc_hi + wk * lax.bitcast_convert_type(packed & jnp.uint32(0xFFFF0000), F32)
    g = gmod_ref[0]
    x_lo = x_ref[:, :half] + g[:, :half] * acc_lo
    x_hi = x_ref[:, half:] + g[:, half:] * acc_hi
    if not last:
        o_ref[:, :half] = x_lo
        o_ref[:, half:] = x_hi
    ms = (jnp.sum(x_lo * x_lo, axis=-1, keepdims=True) + jnp.sum(x_hi * x_hi, axis=-1, keepdims=True)) / (2 * half)
    r = lax.rsqrt(ms + EPS)
    ng = ng_ref[...]
    nsc = nsc_ref[0]
    nsh = nsh_ref[0]
    hn_ref[:, :half] = (x_lo * r * ng[:, :half] * (1.0 + nsc[:, :half]) + nsh[:, :half]).astype(hn_ref.dtype)
    hn_ref[:, half:] = (x_hi * r * ng[:, half:] * (1.0 + nsc[:, half:]) + nsh[:, half:]).astype(hn_ref.dtype)


def _combine(pos, x2, seq, gmod, h, wts_t, sg, su, sd, ys, per, next_g, next_sc, next_sh, last):
    t, d = x2.shape
    half = d // 2
    ds_ = sg.shape[1]
    tm = min(256, seq)
    tiles_per_batch = seq // tm
    kern = functools.partial(_combine_kernel, tm=tm, per=per, last=last)
    const = lambda shape: pl.BlockSpec(shape, lambda i: (0,) * len(shape))
    est = 6 * tm * d * 4 + 2 * tm * d * 2 + 2 * 3 * d * ds_ * 2 + TOP_K * tm * half * 4 + 8 * tm * d * 4
    tile = pl.BlockSpec((tm, d), lambda i: (i, 0))
    per_batch = pl.BlockSpec((1, 1, d), lambda i: (i // tiles_per_batch, 0, 0))
    if last:
        out_specs, out_shape = [tile], [jax.ShapeDtypeStruct((t, d), F32)]
    else:
        out_specs = [tile, tile]
        out_shape = [jax.ShapeDtypeStruct((t, d), F32), jax.ShapeDtypeStruct((t, d), BF16)]
    outs = pl.pallas_call(
        kern,
        grid=(t // tm,),
        in_specs=[
            pl.BlockSpec((TOP_K, tm), lambda i: (0, i), memory_space=pltpu.SMEM),
            tile,
            per_batch,
            tile,
            pl.BlockSpec((tm, TOP_K), lambda i: (i, 0)),
            const((d, ds_)), const((d, ds_)), const((ds_, d)),
            const((1, d)), per_batch, per_batch,
            pl.BlockSpec(memory_space=pl.ANY),
        ],
        out_specs=out_specs,
        out_shape=out_shape,
        scratch_shapes=[pltpu.VMEM((TOP_K * tm * per, LANES), U32), pltpu.SemaphoreType.DMA(())],
        compiler_params=_cparams(("arbitrary",), est),
        name="moe_combine",
    )(pos, x2, gmod, h, wts_t, sg, su, sd, next_g.reshape(1, d), next_sc, next_sh, ys)
    return (None, outs[0]) if last else (outs[0], outs[1])


def _moe_layout(counts, n_blocks):
    padded = (counts + MOE_BLK - 1) // MOE_BLK * MOE_BLK
    pends = jnp.cumsum(padded)
    pstart = (pends - padded).astype(I32)
    blk_row = jnp.arange(n_blocks, dtype=I32)[:, None] * MOE_BLK
    blk_exp = jnp.minimum(jnp.sum((pends[None, :] <= blk_row).astype(I32), axis=1), N_EXPERTS - 1)
    n_used = (pends[-1:] // MOE_BLK).astype(I32)
    return pstart, blk_exp, n_used


def _layer(x2, h1, bsz, seq, layer, mods, next_norm, last, w_in, lb, onorm_g, conv_w, rel_bias, w_a, w_b, w_c, w_out,
           norm2_g, router_w, router_bias, wg, wu, wd, sg, su, sd):
    t, d = x2.shape
    sh1, sc1, g1, sh2, sc2, g2 = mods
    n_col_blocks = w_in.shape[2] // A_DIM
    n_mix_blocks = n_col_blocks - (3 * d) // A_DIM
    pg = _project(h1, w_in, layer, tuple(range(n_mix_blocks, n_col_blocks)), "sigmoid", BF16, "proj_gates")
    af = _project(h1, w_in, layer, (1,), "none", F32, "proj_forget")
    p = _project(h1, w_in, layer, (4, 5, 0, 2, 3, 6, 7, 8), "none", BF16, "proj_mixers")
    b_blk = 0
    aq_blk = (3 * B_WIDTH) // A_DIM
    q_blk = aq_blk + 3
    a_feat = _hgrn(p, af, jnp.log(lb), jnp.log1p(-lb), onorm_g, bsz, seq, aq_blk)
    c_feat = _attention(p, rel_bias, bsz, seq, q_blk)
    x2 = _merge(x2, seq, g1, a_feat, c_feat, p, pg, conv_w, w_a.astype(BF16), w_b.astype(BF16), w_c.astype(BF16),
                w_out.astype(BF16), b_blk)

    h, hp, eidx, wts, rank, cnt = _router(x2, seq, norm2_g, sc2, sh2, router_w.T, router_bias)
    n_blocks = (t * TOP_K) // MOE_BLK + N_EXPERTS
    pstart, blk_exp, n_used = _moe_layout(cnt[:, 0], n_blocks)
    pos = _positions(pstart, eidx, rank)
    per = d // (2 * LANES)
    xs = _dispatch(pos, hp, n_blocks * MOE_BLK, per)
    ys = _experts(blk_exp, n_used, xs, wg, wu, wd, layer, per)
    return _combine(pos, x2, seq, g2, h, wts.T, sg.astype(BF16), su.astype(BF16), sd.astype(BF16), ys, per,
                    *next_norm, last)


def kernel(x, c, ada_w, ada_b, norm1_g, w_in, hgrn_lb_logits, hgrn_onorm_g, conv_w, rel_bias, w_branch_a, w_branch_b, w_branch_c, w_out, norm2_g, router_w, router_bias, moe_w_gate, moe_w_up, moe_w_down, shared_w_gate, shared_w_up, shared_w_down, final_g):
    bsz, seq, d = x.shape
    depth = ada_w.shape[0]
    mod = _modulation(c, ada_w, ada_b)
    lbs = jnp.cumsum(jax.nn.softmax(hgrn_lb_logits.astype(F32), axis=0), axis=0)
    lbs = lbs - lbs[0]
    mods = [[m.reshape(bsz, 1, d) for m in jnp.split(mod[l], N_MOD, axis=-1)] for l in range(depth)]
    x2 = x.reshape(bsz * seq, d)
    h1 = _norm_mod(x2, seq, norm1_g[0], mods[0][1], mods[0][0])
    no_mod = jnp.zeros((bsz, 1, d), F32)
    for l in range(depth):
        last = l == depth - 1
        next_norm = (final_g, no_mod, no_mod) if last else (norm1_g[l + 1], mods[l + 1][1], mods[l + 1][0])
        x2, h1 = _layer(x2, h1, bsz, seq, l, mods[l], next_norm, last, w_in, lbs[l], hgrn_onorm_g[l], conv_w[l],
                        rel_bias[l], w_branch_a[l], w_branch_b[l], w_branch_c[l], w_out[l], norm2_g[l], router_w[l],
                        router_bias[l], moe_w_gate, moe_w_up, moe_w_down, shared_w_gate[l], shared_w_up[l],
                        shared_w_down[l])
    return h1.reshape(bsz, seq, d)
```

```python
import functools

import numpy as np
import jax
import jax.numpy as jnp
from jax import lax
from jax.experimental import pallas as pl
from jax.experimental.pallas import tpu as pltpu

F32 = jnp.float32
BF16 = jnp.bfloat16
I32 = jnp.int32
U32 = jnp.uint32

CHUNK = 64
EPS = 1e-6
A_HEADS = 6
A_HEAD_DIM = 128
A_DIM = A_HEADS * A_HEAD_DIM
B_WIDTH = 512
CONV_W = 3
C_HEADS = 6
C_HEAD_DIM = 128
C_DIM = C_HEADS * C_HEAD_DIM
LEFT_CHUNKS = 8
REL_CLIP = 256
N_EXPERTS = 64
TOP_K = 8
N_GROUPS = 8
GROUP_SIZE = N_EXPERTS // N_GROUPS
TOPK_GROUPS = 4
ROUTED_SCALE = 2.5
N_MOD = 6

V7X_VMEM_BYTES = 64 * 1024 * 1024
LANES = 128
NEG_BIG = -1e30

SUB = 16
FACTOR_SAFE_LOG = 80.0
MID = CHUNK // 2
MOE_BLK = 512
ATT_TQ = 256


def _vmem_limit(nbytes):
    return int(min(max(nbytes * 3 // 2, 16 * 1024 * 1024), V7X_VMEM_BYTES - 8 * 1024 * 1024))


def _cparams(sem, nbytes):
    return pltpu.CompilerParams(dimension_semantics=sem, vmem_limit_bytes=_vmem_limit(nbytes))


def _sigmoid(x):
    return 1.0 / (1.0 + jnp.exp(-x))


def _silu(x):
    return x * _sigmoid(x)


def _split3(x):
    hi = x.astype(BF16)
    r1 = x - hi.astype(F32)
    mid = r1.astype(BF16)
    lo = (r1 - mid.astype(F32)).astype(BF16)
    return hi, mid, lo


def _pack_bf16_pairs(x):
    n = x.shape[1] // 2
    xb = x.astype(BF16).astype(F32)
    lo = lax.bitcast_convert_type(xb[:, :n], U32) >> 16
    hi = lax.bitcast_convert_type(xb[:, n:], U32) & jnp.uint32(0xFFFF0000)
    return lo | hi


def _unpack_bf16_pairs(p):
    lo = lax.bitcast_convert_type(p << 16, F32).astype(BF16)
    hi = lax.bitcast_convert_type(p & jnp.uint32(0xFFFF0000), F32).astype(BF16)
    return lo, hi


def _store_token_tiles(ref, first_token, mat):
    n, w = mat.shape
    per = w // LANES
    for s in range(per):
        ref[pl.ds(first_token * per + s, n, stride=per), :] = mat[:, s * LANES:(s + 1) * LANES]


def _load_token_tiles(ref, first_token, n, per):
    return jnp.concatenate([ref[pl.ds(first_token * per + s, n, stride=per), :] for s in range(per)], axis=1)


def _mod_kernel(c_ref, w_ref, b_ref, o_ref):
    c = c_ref[...]
    o_ref[0] = jnp.dot(_silu(c), w_ref[0], preferred_element_type=F32) + b_ref[0]


def _modulation(c, ada_w, ada_b):
    depth, d, n = ada_w.shape
    bsz = c.shape[0]
    rows = 8
    c_pad = jnp.zeros((rows, d), F32).at[:bsz].set(c)
    tn = 1536 if n % 1536 == 0 else n
    out = pl.pallas_call(
        _mod_kernel,
        grid=(depth, n // tn),
        in_specs=[
            pl.BlockSpec((rows, d), lambda l, j: (0, 0)),
            pl.BlockSpec((1, d, tn), lambda l, j: (l, 0, j)),
            pl.BlockSpec((1, 1, tn), lambda l, j: (l, 0, j)),
        ],
        out_specs=pl.BlockSpec((1, rows, tn), lambda l, j: (l, 0, j)),
        out_shape=jax.ShapeDtypeStruct((depth, rows, n), F32),
        compiler_params=_cparams(("arbitrary", "arbitrary"), 2 * d * tn * 4 + 4 * rows * tn * 4),
        name="adaln_mod",
    )(c_pad, ada_w, ada_b.reshape(depth, 1, n))
    return out[:, :bsz]


def _norm_kernel(x_ref, g_ref, sc_ref, sh_ref, h_ref):
    x = x_ref[...]
    y = x * lax.rsqrt(jnp.mean(x * x, axis=-1, keepdims=True) + EPS)
    h_ref[...] = (y * g_ref[...] * (1.0 + sc_ref[0]) + sh_ref[0]).astype(BF16)


def _norm_mod(x2, seq, g, sc, sh):
    t, d = x2.shape
    tm = min(512, seq)
    tiles_per_batch = seq // tm
    return pl.pallas_call(
        _norm_kernel,
        grid=(t // tm,),
        in_specs=[
            pl.BlockSpec((tm, d), lambda i: (i, 0)),
            pl.BlockSpec((1, d), lambda i: (0, 0)),
            pl.BlockSpec((1, 1, d), lambda i: (i // tiles_per_batch, 0, 0)),
            pl.BlockSpec((1, 1, d), lambda i: (i // tiles_per_batch, 0, 0)),
        ],
        out_specs=pl.BlockSpec((tm, d), lambda i: (i, 0)),
        out_shape=jax.ShapeDtypeStruct((t, d), BF16),
        compiler_params=_cparams(("arbitrary",), 6 * tm * d * 4),
        name="norm_mod",
    )(x2, g.reshape(1, d), sc, sh)


def _proj_kernel(cmap_ref, h_ref, w_ref, o_ref, w_scr, *, act):
    @pl.when(pl.program_id(1) == 0)
    def _():
        w_scr[...] = w_ref[0].astype(BF16)

    acc = jnp.dot(h_ref[...], w_scr[...], preferred_element_type=F32)
    if act == "sigmoid":
        acc = _sigmoid(acc)
    o_ref[...] = acc.astype(o_ref.dtype)


def _project(h, w_all, layer, col_blocks, act, out_dtype, name):
    t, d = h.shape
    tn = A_DIM
    nj = len(col_blocks)
    tm = min(2048, t)
    osz = jnp.dtype(out_dtype).itemsize
    est = 2 * tm * d * 2 + 2 * d * tn * 4 + d * tn * 2 + 2 * tm * tn * osz + 2 * tm * tn * 4
    return pl.pallas_call(
        functools.partial(_proj_kernel, act=act),
        grid_spec=pltpu.PrefetchScalarGridSpec(
            num_scalar_prefetch=1,
            grid=(nj, t // tm),
            in_specs=[
                pl.BlockSpec((tm, d), lambda j, i, cm: (i, 0)),
                pl.BlockSpec((1, d, tn), lambda j, i, cm: (layer, 0, cm[j])),
            ],
            out_specs=pl.BlockSpec((tm, tn), lambda j, i, cm: (i, j)),
            scratch_shapes=[pltpu.VMEM((d, tn), BF16)],
        ),
        out_shape=jax.ShapeDtypeStruct((t, nj * tn), out_dtype),
        compiler_params=_cparams(("arbitrary", "arbitrary"), est),
        name=name,
    )(jnp.asarray(col_blocks, I32), h, w_all)


def _hgrn_kernel(aq_ref, af_ref, ai_ref, ag_ref, loglb_ref, log1mlb_ref, og_ref, sel_ref, o_ref,
                 st_scr, lf_scr, q_scr, k_scr, v_scr, b_scr, p_scr, *, n_chunks):
    @pl.when(pl.program_id(1) == 0)
    def _():
        st_scr[...] = jnp.zeros_like(st_scr)

    hd = A_HEAD_DIM
    n_sub = CHUNK // SUB
    nt = (((1,), (1,)), ((), ()))
    row_c = lax.broadcasted_iota(I32, (CHUNK, CHUNK), 0)
    col_c = lax.broadcasted_iota(I32, (CHUNK, CHUNK), 1)
    tri = (col_c <= row_c).astype(BF16)
    sub_row = lax.broadcasted_iota(I32, (SUB, 1), 0)
    chunk_row = lax.broadcasted_iota(I32, (CHUNK, 1), 0)
    log_lb = loglb_ref[...]
    log1m_lb = log1mlb_ref[...]
    og = og_ref[...]

    def log_f_body(ci, lowest):
        rows = pl.ds(pl.multiple_of(ci * CHUNK, CHUNK), CHUNK)
        af = af_ref[rows, :]
        ls = jnp.minimum(af, 0.0) - jnp.log(1.0 + jnp.exp(-jnp.abs(af)))
        t2 = log1m_lb + ls
        lf = jnp.maximum(log_lb, t2) + jnp.log(1.0 + jnp.exp(-jnp.abs(log_lb - t2)))
        lf_scr[rows, :] = lf
        first = jnp.sum(jnp.where(chunk_row <= MID, lf, 0.0), axis=0, keepdims=True)
        second = jnp.sum(jnp.where(chunk_row >= MID, lf, 0.0), axis=0, keepdims=True)
        return jnp.minimum(lowest, jnp.minimum(first, second))

    lowest_sum = lax.fori_loop(0, n_chunks, log_f_body, jnp.zeros((1, A_DIM), F32), unroll=2)
    safe = jnp.min(lowest_sum) > -FACTOR_SAFE_LOG

    def prep(ci):
        r0 = pl.multiple_of(ci * CHUNK, CHUNK)
        rows = pl.ds(r0, CHUNK)
        lf = lf_scr[rows, :]
        hi, mid, lo = _split3(lf)
        b = (jnp.dot(tri, hi, preferred_element_type=F32) + jnp.dot(tri, mid, preferred_element_type=F32)
             + jnp.dot(tri, lo, preferred_element_type=F32))
        q = aq_ref[rows, :].astype(F32) * (A_HEAD_DIM ** -0.5)
        return rows, 1.0 - jnp.exp(lf), b, q, ai_ref[rows, :]

    def head_tail(h, o, kh, vh, bh, bl):
        ktil = (kh * jnp.exp(bl - bh)).astype(BF16)
        upd = lax.dot_general(vh, ktil, (((0,), (0,)), ((), ())), preferred_element_type=F32)
        st_scr[h] = st_scr[h] * jnp.exp(bl) + upd
        return o * lax.rsqrt(jnp.mean(o * o, axis=-1, keepdims=True) + EPS)

    def finish(rows, outs):
        o_all = jnp.concatenate(outs, axis=1)
        gate = ag_ref[rows, :].astype(F32)
        o_ref[rows, :] = (o_all * og * _silu(gate)).astype(BF16)

    def fast_chunk(ci, carry):
        rows, k_all, b_all, q_all, v_all = prep(ci)
        qt_all = (q_all * jnp.exp(b_all)).astype(BF16)
        b_mid = b_all[MID:MID + 1]
        qm_all = (q_all * jnp.exp(b_all - b_mid)).astype(BF16)
        km_all = (k_all * jnp.exp(b_mid - b_all)).astype(BF16)
        b_last = b_all[CHUNK - 1:CHUNK]
        outs = []
        for h in range(A_HEADS):
            cols = slice(h * hd, (h + 1) * hd)
            vh = v_all[:, cols]
            qt = qt_all[:, cols]
            a = lax.dot_general(qm_all[:, cols], km_all[:, cols], nt, preferred_element_type=F32)
            a = jnp.where(col_c <= row_c, a, 0.0).astype(BF16)
            o = (jnp.dot(a, vh, preferred_element_type=F32)
                 + lax.dot_general(qt, st_scr[h].astype(BF16), nt, preferred_element_type=F32))
            outs.append(head_tail(h, o, k_all[:, cols], vh, b_all[:, cols], b_last[:, cols]))
        finish(rows, outs)
        return carry

    def exact_chunk(ci, carry):
        rows, k_new, b_new, q_new, v_new = prep(ci)
        k_scr[...] = k_new
        b_scr[...] = b_new
        q_scr[...] = q_new
        v_scr[...] = v_new.astype(F32)

        for i in range(n_sub):
            s0 = i * SUB
            b_blk = b_scr[pl.ds(s0, SUB), :]
            q_blk = q_scr[pl.ds(s0, SUB), :]
            for s in range(SUB):
                b_s = b_scr[pl.ds(s0 + s, 1), :]
                k_s = k_scr[pl.ds(s0 + s, 1), :]
                e = jnp.exp(jnp.where(sub_row >= s, b_blk - b_s, NEG_BIG))
                pv = (q_blk * k_s * e).astype(BF16)
                for h in range(A_HEADS):
                    p_scr[pl.ds((i * A_HEADS + h) * SUB, SUB), s * hd:(s + 1) * hd] = pv[:, h * hd:(h + 1) * hd]
        diag = jnp.dot(p_scr[...], sel_ref[...], preferred_element_type=F32)

        b_all = b_scr[...]
        q_all = q_scr[...]
        k_all = k_scr[...]
        qhat, khat = [], []
        for i in range(1, n_sub):
            s0 = i * SUB
            beta = b_scr[pl.ds(s0 - 1, 1), :]
            qhat.append((q_all[s0:s0 + SUB] * jnp.exp(b_all[s0:s0 + SUB] - beta)).astype(BF16))
            khat.append(jnp.where(chunk_row < s0, k_all * jnp.exp(jnp.minimum(beta - b_all, 0.0)), 0.0).astype(BF16))
        zero_blk = jnp.zeros((SUB, hd), BF16)

        b_last = b_scr[pl.ds(CHUNK - 1, 1), :]
        outs = []
        for h in range(A_HEADS):
            cols = slice(h * hd, (h + 1) * hd)
            qh = q_all[:, cols]
            kh = k_all[:, cols]
            vh = v_scr[:, cols].astype(BF16)
            bh = b_all[:, cols]
            st = st_scr[h]
            lhs_rows = [jnp.concatenate([zero_blk] * (n_sub - 1), axis=1)]
            for i in range(1, n_sub):
                parts = [zero_blk] * (n_sub - 1)
                parts[i - 1] = qhat[i - 1][:, cols]
                lhs_rows.append(jnp.concatenate(parts, axis=1))
            lhs = jnp.concatenate(lhs_rows, axis=0)
            rhs = jnp.concatenate([kk[:, cols] for kk in khat], axis=1)
            a_off = lax.dot_general(lhs, rhs, nt, preferred_element_type=F32)
            a_diag = jnp.concatenate(
                [pltpu.roll(diag[(i * A_HEADS + h) * SUB:(i * A_HEADS + h + 1) * SUB], i * SUB, 1) if i else
                 diag[h * SUB:(h + 1) * SUB] for i in range(n_sub)], axis=0)
            a = (a_diag[:, :CHUNK] + a_off).astype(BF16)
            qt = (qh * jnp.exp(bh)).astype(BF16)
            o = (jnp.dot(a, vh, preferred_element_type=F32)
                 + lax.dot_general(qt, st.astype(BF16), nt, preferred_element_type=F32))
            outs.append(head_tail(h, o, kh, vh, bh, b_last[:, cols]))
        finish(rows, outs)
        return carry

    @pl.when(safe)
    def _():
        lax.fori_loop(0, n_chunks, fast_chunk, 0, unroll=8)

    @pl.when(jnp.logical_not(safe))
    def _():
        lax.fori_loop(0, n_chunks, exact_chunk, 0)


def _hgrn(p, af, log_lb, log1m_lb, onorm_g, bsz, seq, aq_blk):
    t = p.shape[0]
    ts = min(512, seq)
    n_seq_tiles = seq // ts
    n_chunks = ts // CHUNK
    kern = functools.partial(_hgrn_kernel, n_chunks=n_chunks)
    row = lambda b, i: b * n_seq_tiles + i
    est = (2 * 3 * ts * A_DIM * 2 + 2 * ts * A_DIM * 4 + 2 * ts * A_DIM * 2 + A_HEADS * 128 * 128 * 4
           + 4 * CHUNK * A_DIM * 4 + CHUNK * SUB * A_DIM * 6 + 3 * ts * A_DIM * 4)
    sel_rows = np.arange(SUB * A_HEAD_DIM)
    selector = jnp.asarray(sel_rows[:, None] // A_HEAD_DIM == np.arange(A_HEAD_DIM)[None, :], BF16)
    return pl.pallas_call(
        kern,
        grid=(bsz, n_seq_tiles),
        in_specs=[
            pl.BlockSpec((ts, A_DIM), lambda b, i: (row(b, i), aq_blk)),
            pl.BlockSpec((ts, A_DIM), lambda b, i: (row(b, i), 0)),
            pl.BlockSpec((ts, A_DIM), lambda b, i: (row(b, i), aq_blk + 1)),
            pl.BlockSpec((ts, A_DIM), lambda b, i: (row(b, i), aq_blk + 2)),
            pl.BlockSpec((1, A_DIM), lambda b, i: (0, 0)),
            pl.BlockSpec((1, A_DIM), lambda b, i: (0, 0)),
            pl.BlockSpec((1, A_DIM), lambda b, i: (0, 0)),
            pl.BlockSpec((SUB * A_HEAD_DIM, A_HEAD_DIM), lambda b, i: (0, 0)),
        ],
        out_specs=pl.BlockSpec((ts, A_DIM), lambda b, i: (row(b, i), 0)),
        out_shape=jax.ShapeDtypeStruct((t, A_DIM), BF16),
        scratch_shapes=[
            pltpu.VMEM((A_HEADS, A_HEAD_DIM, A_HEAD_DIM), F32),
            pltpu.VMEM((ts, A_DIM), F32),
            pltpu.VMEM((CHUNK, A_DIM), F32),
            pltpu.VMEM((CHUNK, A_DIM), F32),
            pltpu.VMEM((CHUNK, A_DIM), F32),
            pltpu.VMEM((CHUNK, A_DIM), F32),
            pltpu.VMEM((CHUNK * A_HEADS, SUB * A_HEAD_DIM), BF16),
        ],
        compiler_params=_cparams(("arbitrary", "arbitrary"), est),
        name="hgrn2_scan",
    )(p, af, p, p, log_lb.reshape(1, A_DIM), log1m_lb.reshape(1, A_DIM), onorm_g.reshape(1, A_DIM), selector)


def _attn_kernel(q_ref, k0_ref, k1_ref, k2_ref, v0_ref, v1_ref, v2_ref, bias_ref, o_ref):
    i = pl.program_id(1)
    tq = q_ref.shape[0]
    hd = C_HEAD_DIM
    scale = C_HEAD_DIM ** -0.5
    col = lax.broadcasted_iota(I32, (1, 3 * tq), 1)
    valid = (col >= 2 * tq) | ((col >= tq) & (i >= 1)) | (i >= 2)
    for h in range(C_HEADS):
        cols = slice(h * hd, (h + 1) * hd)
        kh = jnp.concatenate([k0_ref[:, cols], k1_ref[:, cols], k2_ref[:, cols]], axis=0)
        vh = jnp.concatenate([v0_ref[:, cols], v1_ref[:, cols], v2_ref[:, cols]], axis=0)
        s = lax.dot_general(q_ref[:, cols], kh, (((1,), (1,)), ((), ())), preferred_element_type=F32)
        s = jnp.where(valid, s * scale + bias_ref[h], NEG_BIG)
        e = jnp.exp(s - jnp.max(s, axis=-1, keepdims=True))
        inv = 1.0 / jnp.sum(e, axis=-1, keepdims=True)
        o_ref[:, cols] = (jnp.dot(e.astype(BF16), vh, preferred_element_type=F32) * inv).astype(BF16)


def _attention_bias_table(rel_bias, tq):
    band = (LEFT_CHUNKS + 1) * CHUNK
    assert tq * 2 == LEFT_CHUNKS * CHUNK, "three key slots of tq rows must cover the band"
    diff = np.arange(CHUNK + band - 1) - (band - 1)
    ext = rel_bias.astype(F32)[:, np.clip(diff + LEFT_CHUNKS * CHUNK, -REL_CLIP, REL_CLIP) + REL_CLIP]
    rev = ext[:, ::-1]
    chunk_tab = jnp.stack([rev[:, CHUNK - 1 - tt:CHUNK - 1 - tt + band] for tt in range(CHUNK)], axis=1)
    rows = [jnp.pad(chunk_tab, ((0, 0), (0, 0), (c * CHUNK, 3 * tq - band - c * CHUNK)), constant_values=NEG_BIG)
            for c in range(tq // CHUNK)]
    return jnp.concatenate(rows, axis=1)


def _attention(p, rel_bias, bsz, seq, q_blk):
    t = p.shape[0]
    tq = ATT_TQ
    nq = seq // tq
    bias = _attention_bias_table(rel_bias, tq)
    row = lambda b, i, back: b * nq + jnp.maximum(i - back, 0)
    kv_spec = lambda blk, back: pl.BlockSpec((tq, C_DIM), lambda b, i: (row(b, i, back), blk))
    est = 2 * 7 * tq * C_DIM * 2 + 2 * C_HEADS * tq * 3 * tq * 4 + 2 * tq * C_DIM * 2 + 6 * tq * 3 * tq * 4
    return pl.pallas_call(
        _attn_kernel,
        grid=(bsz, nq),
        in_specs=[
            pl.BlockSpec((tq, C_DIM), lambda b, i: (b * nq + i, q_blk)),
            kv_spec(q_blk + 1, 2), kv_spec(q_blk + 1, 1), kv_spec(q_blk + 1, 0),
            kv_spec(q_blk + 2, 2), kv_spec(q_blk + 2, 1), kv_spec(q_blk + 2, 0),
            pl.BlockSpec((C_HEADS, tq, 3 * tq), lambda b, i: (0, 0, 0)),
        ],
        out_specs=pl.BlockSpec((tq, C_DIM), lambda b, i: (b * nq + i, 0)),
        out_shape=jax.ShapeDtypeStruct((t, C_DIM), BF16),
        compiler_params=_cparams(("arbitrary", "arbitrary"), est),
        name="chunk_attention",
    )(p, p, p, p, p, p, p, bias)


def _merge_kernel(x_ref, gmod_ref, a_ref, c_ref, bb_ref, bc_ref, bu_ref, hbc_ref, hbu_ref, cw_ref,
                  ga_ref, gb_ref, gc_ref, wa_ref, wb_ref, wc_ref, wo_ref, o_ref, *, tiles_per_batch):
    i = pl.program_id(0)
    tm = x_ref.shape[0]
    u = bc_ref[...].astype(F32) * bu_ref[...].astype(F32)
    halo = hbc_ref[...].astype(F32) * hbu_ref[...].astype(F32)
    halo = jnp.where(i % tiles_per_batch == 0, 0.0, halo)
    hrows = halo.shape[0]
    hm1 = halo[hrows - 1:hrows]
    hm2 = halo[hrows - 2:hrows - 1]
    r = lax.broadcasted_iota(I32, (tm, 1), 0)
    u1 = jnp.where(r == 0, hm1, pltpu.roll(u, 1, 0))
    u2 = jnp.where(r == 0, hm2, jnp.where(r == 1, hm1, pltpu.roll(u, 2, 0)))
    cw = cw_ref[...]
    conv = u2 * cw[0:1] + u1 * cw[1:2] + u * cw[2:3]
    bfeat = (bb_ref[...].astype(F32) * conv).astype(BF16)
    ya = jnp.dot(a_ref[...], wa_ref[...], preferred_element_type=F32)
    yb = jnp.dot(bfeat, wb_ref[...], preferred_element_type=F32)
    yc = jnp.dot(c_ref[...], wc_ref[...], preferred_element_type=F32)
    merged = (ga_ref[...].astype(F32) * ya + gb_ref[...].astype(F32) * yb + gc_ref[...].astype(F32) * yc)
    y = jnp.dot(merged.astype(BF16), wo_ref[...], preferred_element_type=F32)
    o_ref[...] = x_ref[...] + gmod_ref[0] * y


def _merge(x2, seq, gmod, a_feat, c_feat, p, pg, conv_w, wa, wb, wc, wo, b_blk):
    t, d = x2.shape
    tm = min(256, seq)
    tiles_per_batch = seq // tm
    halo = 16
    hb = tm // halo
    kern = functools.partial(_merge_kernel, tiles_per_batch=tiles_per_batch)
    const = lambda shape: pl.BlockSpec(shape, lambda i: (0,) * len(shape))
    halo_spec = lambda blk: pl.BlockSpec((halo, B_WIDTH), lambda i: (jnp.maximum(i * hb - 1, 0), blk))
    est = (4 * tm * d * 4 + 4 * tm * A_DIM * 2 + 6 * tm * B_WIDTH * 2 + 6 * tm * d * 2
           + 2 * (A_DIM + B_WIDTH + C_DIM + d) * d * 2 + 6 * tm * d * 4)
    return pl.pallas_call(
        kern,
        grid=(t // tm,),
        in_specs=[
            pl.BlockSpec((tm, d), lambda i: (i, 0)),
            pl.BlockSpec((1, 1, d), lambda i: (i // tiles_per_batch, 0, 0)),
            pl.BlockSpec((tm, A_DIM), lambda i: (i, 0)),
            pl.BlockSpec((tm, C_DIM), lambda i: (i, 0)),
            pl.BlockSpec((tm, B_WIDTH), lambda i: (i, b_blk)),
            pl.BlockSpec((tm, B_WIDTH), lambda i: (i, b_blk + 1)),
            pl.BlockSpec((tm, B_WIDTH), lambda i: (i, b_blk + 2)),
            halo_spec(b_blk + 1),
            halo_spec(b_blk + 2),
            const((CONV_W, B_WIDTH)),
            pl.BlockSpec((tm, d), lambda i: (i, 0)),
            pl.BlockSpec((tm, d), lambda i: (i, 1)),
            pl.BlockSpec((tm, d), lambda i: (i, 2)),
            const((A_DIM, d)), const((B_WIDTH, d)), const((C_DIM, d)), const((d, d)),
        ],
        out_specs=pl.BlockSpec((tm, d), lambda i: (i, 0)),
        out_shape=jax.ShapeDtypeStruct((t, d), F32),
        compiler_params=_cparams(("arbitrary",), est),
        name="merge_outproj",
    )(x2, gmod, a_feat, c_feat, p, p, p, p, p, conv_w, pg, pg, pg, wa, wb, wc, wo)


def _first_max(vals, iota, n, axis):
    m = jnp.max(vals, axis=axis, keepdims=True)
    idx = jnp.min(jnp.where(vals == m, iota, n), axis=axis, keepdims=True)
    return m, idx


def _router_kernel(x_ref, g_ref, sc_ref, sh_ref, rwt_ref, rb_ref, h_ref, hp_ref, e_ref, w_ref, rank_ref, cnt_ref,
                   cnt_scr):
    step = pl.program_id(0)

    @pl.when(step == 0)
    def _():
        cnt_scr[...] = jnp.zeros_like(cnt_scr)

    tm = x_ref.shape[0]
    x = x_ref[...]
    y = x * lax.rsqrt(jnp.mean(x * x, axis=-1, keepdims=True) + EPS)
    h = y * g_ref[...] * (1.0 + sc_ref[0]) + sh_ref[0]
    h_ref[...] = h.astype(BF16)
    _store_token_tiles(hp_ref, 0, _pack_bf16_pairs(h))

    nt = (((1,), (1,)), ((), ()))
    h_hi, h_mid, _ = _split3(h)
    w_hi, w_mid, _ = _split3(rwt_ref[...])
    logits = (lax.dot_general(w_hi, h_hi, nt, preferred_element_type=F32)
              + lax.dot_general(w_hi, h_mid, nt, preferred_element_type=F32)
              + lax.dot_general(w_mid, h_hi, nt, preferred_element_type=F32))
    scores = _sigmoid(logits)
    sel = scores + rb_ref[:, 0:1]

    g3 = sel.reshape(N_GROUPS, GROUP_SIZE, tm)
    j3 = lax.broadcasted_iota(I32, g3.shape, 1)
    m1, i1 = _first_max(g3, j3, GROUP_SIZE, 1)
    m2 = jnp.max(jnp.where(j3 == i1, -jnp.inf, g3), axis=1, keepdims=True)
    gs = (m1 + m2).reshape(N_GROUPS, tm)
    gi = lax.broadcasted_iota(I32, gs.shape, 0)
    gmask = jnp.zeros(gs.shape, jnp.bool_)
    for _ in range(TOPK_GROUPS):
        _, idx = _first_max(gs, gi, N_GROUPS, 0)
        hit = gi == idx
        gmask = gmask | hit
        gs = jnp.where(hit, -jnp.inf, gs)
    emask = jnp.broadcast_to(gmask.reshape(N_GROUPS, 1, tm), (N_GROUPS, GROUP_SIZE, tm)).reshape(N_EXPERTS, tm)
    masked = jnp.where(emask, sel, -jnp.inf)

    ei = lax.broadcasted_iota(I32, (N_EXPERTS, tm), 0)
    chosen = jnp.zeros((N_EXPERTS, tm), F32)
    e_rows, w_rows = [], []
    for _ in range(TOP_K):
        _, idx = _first_max(masked, ei, N_EXPERTS, 0)
        hit = ei == idx
        e_rows.append(idx)
        w_rows.append(jnp.sum(jnp.where(hit, scores, 0.0), axis=0, keepdims=True))
        chosen = jnp.where(hit, 1.0, chosen)
        masked = jnp.where(hit, -jnp.inf, masked)
    eidx = jnp.concatenate(e_rows, axis=0)
    wts = jnp.concatenate(w_rows, axis=0)
    wts = wts / jnp.sum(wts, axis=0, keepdims=True) * ROUTED_SCALE
    e_ref[...] = eidx
    w_ref[...] = wts

    tr = lax.broadcasted_iota(I32, (tm, tm), 0)
    tc = lax.broadcasted_iota(I32, (tm, tm), 1)
    before = (tr < tc).astype(BF16)
    base = cnt_scr[:, 0:1]
    rank_all = jnp.dot(chosen.astype(BF16), before, preferred_element_type=F32) + base
    ranks = [jnp.sum(jnp.where(ei == e_rows[k], rank_all, 0.0), axis=0, keepdims=True) for k in range(TOP_K)]
    rank_ref[...] = jnp.concatenate(ranks, axis=0).astype(I32)
    total = base + jnp.sum(chosen, axis=1, keepdims=True)
    cnt_scr[...] = jnp.broadcast_to(total, cnt_scr.shape)
    cnt_ref[...] = jnp.broadcast_to(total, cnt_ref.shape).astype(I32)


def _router(x2, seq, g, sc, sh, router_wt, router_bias):
    t, d = x2.shape
    tm = min(512, seq)
    tiles_per_batch = seq // tm
    est = (2 * tm * d * 4 + 2 * tm * d * 2 + 2 * tm * d * 2 + 2 * N_EXPERTS * d * 4 + tm * tm * 2
           + 8 * tm * d * 4 + 16 * N_EXPERTS * tm * 4)
    return pl.pallas_call(
        _router_kernel,
        grid=(t // tm,),
        in_specs=[
            pl.BlockSpec((tm, d), lambda i: (i, 0)),
            pl.BlockSpec((1, d), lambda i: (0, 0)),
            pl.BlockSpec((1, 1, d), lambda i: (i // tiles_per_batch, 0, 0)),
            pl.BlockSpec((1, 1, d), lambda i: (i // tiles_per_batch, 0, 0)),
            pl.BlockSpec((N_EXPERTS, d), lambda i: (0, 0)),
            pl.BlockSpec((N_EXPERTS, LANES), lambda i: (0, 0)),
        ],
        out_specs=[
            pl.BlockSpec((tm, d), lambda i: (i, 0)),
            pl.BlockSpec((tm * d // (2 * LANES), LANES), lambda i: (i, 0)),
            pl.BlockSpec((TOP_K, tm), lambda i: (0, i)),
            pl.BlockSpec((TOP_K, tm), lambda i: (0, i)),
            pl.BlockSpec((TOP_K, tm), lambda i: (0, i)),
            pl.BlockSpec((N_EXPERTS, LANES), lambda i: (0, 0)),
        ],
        out_shape=[
            jax.ShapeDtypeStruct((t, d), BF16),
            jax.ShapeDtypeStruct((t * d // (2 * LANES), LANES), U32),
            jax.ShapeDtypeStruct((TOP_K, t), I32),
            jax.ShapeDtypeStruct((TOP_K, t), F32),
            jax.ShapeDtypeStruct((TOP_K, t), I32),
            jax.ShapeDtypeStruct((N_EXPERTS, LANES), I32),
        ],
        scratch_shapes=[pltpu.VMEM((N_EXPERTS, LANES), F32)],
        compiler_params=_cparams(("arbitrary",), est),
        name="norm_router",
    )(x2, g.reshape(1, d), sc, sh, router_wt, jnp.broadcast_to(router_bias.reshape(N_EXPERTS, 1), (N_EXPERTS, LANES)))


def _pos_kernel(start_ref, e_ref, rank_ref, pos_ref):
    e = e_ref[...]
    acc = rank_ref[...]
    for ex in range(N_EXPERTS):
        acc = acc + jnp.where(e == ex, start_ref[ex], 0)
    pos_ref[...] = acc


def _positions(pstart, eidx, rank):
    k, t = eidx.shape
    tl = min(4096, t)
    return pl.pallas_call(
        _pos_kernel,
        grid_spec=pltpu.PrefetchScalarGridSpec(
            num_scalar_prefetch=1,
            grid=(t // tl,),
            in_specs=[pl.BlockSpec((k, tl), lambda i, s: (0, i)), pl.BlockSpec((k, tl), lambda i, s: (0, i))],
            out_specs=pl.BlockSpec((k, tl), lambda i, s: (0, i)),
        ),
        out_shape=jax.ShapeDtypeStruct((k, t), I32),
        compiler_params=_cparams(("arbitrary",), 8 * k * tl * 4),
        name="moe_positions",
    )(pstart, eidx, rank)


def _dispatch_kernel(pos_ref, hp_ref, xs_ref, sem, *, tm, per):
    def copy(t, k):
        src = hp_ref.at[pl.ds(pl.multiple_of(t * per, per), per)]
        dst = xs_ref.at[pl.ds(pl.multiple_of(pos_ref[k, t] * per, per), per)]
        return pltpu.make_async_copy(src, dst, sem)

    def start(t, c):
        for k in range(TOP_K):
            copy(t, k).start(priority=k % 2)
        return c

    def wait(t, c):
        for k in range(TOP_K):
            copy(t, k).wait()
        return c

    lax.fori_loop(0, tm, start, 0, unroll=2)
    lax.fori_loop(0, tm, wait, 0, unroll=2)


def _dispatch(pos, hp, n_rows, per):
    t = hp.shape[0] // per
    tm = min(512, t)
    kern = functools.partial(_dispatch_kernel, tm=tm, per=per)
    return pl.pallas_call(
        kern,
        grid=(t // tm,),
        in_specs=[
            pl.BlockSpec((TOP_K, tm), lambda i: (0, i), memory_space=pltpu.SMEM),
            pl.BlockSpec((tm * per, LANES), lambda i: (i, 0)),
        ],
        out_specs=pl.BlockSpec(memory_space=pl.ANY),
        out_shape=jax.ShapeDtypeStruct((n_rows * per, LANES), U32),
        scratch_shapes=[pltpu.SemaphoreType.DMA(())],
        compiler_params=pltpu.CompilerParams(dimension_semantics=("arbitrary",), has_side_effects=True,
                                             vmem_limit_bytes=_vmem_limit(4 * tm * per * LANES * 4)),
        name="moe_dispatch",
    )(pos, hp)


def _expert_kernel(blk_exp_ref, n_used_ref, xs_ref, wg_ref, wu_ref, wd_ref, ys_ref, wg_scr, wu_scr, wd_scr, *, per):
    b = pl.program_id(0)
    new_expert = (b == 0) | (blk_exp_ref[b] != blk_exp_ref[jnp.maximum(b - 1, 0)])

    @pl.when(new_expert)
    def _():
        wg_scr[...] = wg_ref[0, 0].astype(BF16)
        wu_scr[...] = wu_ref[0, 0].astype(BF16)
        wd_scr[...] = wd_ref[0, 0].astype(BF16)

    @pl.when(b < n_used_ref[0])
    def _():
        lo, hi = _unpack_bf16_pairs(_load_token_tiles(xs_ref, 0, MOE_BLK, per))
        half = lo.shape[1]
        gate = (jnp.dot(lo, wg_scr[:half], preferred_element_type=F32)
                + jnp.dot(hi, wg_scr[half:], preferred_element_type=F32))
        up = (jnp.dot(lo, wu_scr[:half], preferred_element_type=F32)
              + jnp.dot(hi, wu_scr[half:], preferred_element_type=F32))
        hidden = (_silu(gate) * up).astype(BF16)
        _store_token_tiles(ys_ref, 0, _pack_bf16_pairs(jnp.dot(hidden, wd_scr[...], preferred_element_type=F32)))

    @pl.when(b >= n_used_ref[0])
    def _():
        ys_ref[...] = jnp.zeros_like(ys_ref)


def _experts(blk_exp, n_used, xs, wg, wu, wd, layer, per):
    half = per * LANES
    d = 2 * half
    de = wg.shape[3]
    nb = xs.shape[0] // (MOE_BLK * per)
    est = 4 * MOE_BLK * half * 4 + 2 * 3 * d * de * 4 + 3 * d * de * 2 + MOE_BLK * (2 * de + d) * 4
    return pl.pallas_call(
        functools.partial(_expert_kernel, per=per),
        grid_spec=pltpu.PrefetchScalarGridSpec(
            num_scalar_prefetch=2,
            grid=(nb,),
            in_specs=[
                pl.BlockSpec((MOE_BLK * per, LANES), lambda b, be, nu: (b, 0)),
                pl.BlockSpec((1, 1, d, de), lambda b, be, nu: (layer, be[b], 0, 0)),
                pl.BlockSpec((1, 1, d, de), lambda b, be, nu: (layer, be[b], 0, 0)),
                pl.BlockSpec((1, 1, de, d), lambda b, be, nu: (layer, be[b], 0, 0)),
            ],
            out_specs=pl.BlockSpec((MOE_BLK * per, LANES), lambda b, be, nu: (b, 0)),
            scratch_shapes=[pltpu.VMEM((d, de), BF16), pltpu.VMEM((d, de), BF16), pltpu.VMEM((de, d), BF16)],
        ),
        out_shape=jax.ShapeDtypeStruct(xs.shape, U32),
        compiler_params=_cparams(("arbitrary",), est),
        name="moe_experts",
    )(blk_exp, n_used, xs, wg, wu, wd)


def _combine_kernel(pos_ref, x_ref, gmod_ref, h_ref, wt_ref, sg_ref, su_ref, sd_ref, ng_ref, nsc_ref, nsh_ref, ys_ref,
                    *rest, tm, per, last):
    if last:
        hn_ref, yg_scr, sem = rest
    else:
        o_ref, hn_ref, yg_scr, sem = rest

    def copy(t, k):
        src = ys_ref.at[pl.ds(pl.multiple_of(pos_ref[k, t] * per, per), per)]
        dst = yg_scr.at[pl.ds(pl.multiple_of((k * tm + t) * per, per), per)]
        return pltpu.make_async_copy(src, dst, sem)

    def start(t, c):
        for k in range(TOP_K):
            copy(t, k).start(priority=k % 2)
        return c

    def wait(t, c):
        for k in range(TOP_K):
            copy(t, k).wait()
        return c

    lax.fori_loop(0, tm, start, 0, unroll=2)
    h = h_ref[...]
    hidden = (_silu(jnp.dot(h, sg_ref[...], preferred_element_type=F32))
              * jnp.dot(h, su_ref[...], preferred_element_type=F32)).astype(BF16)
    shared = jnp.dot(hidden, sd_ref[...], preferred_element_type=F32)
    lax.fori_loop(0, tm, wait, 0, unroll=2)
    half = per * LANES
    acc_lo = shared[:, :half]
    acc_hi = shared[:, half:]
    wt = wt_ref[...]
    for k in range(TOP_K):
        packed = _load_token_tiles(yg_scr, k * tm, tm, per)
        wk = wt[:, k:k + 1]
        acc_lo = acc_lo + wk * lax.bitcast_convert_type(packed << 16, F32)
        acc_hi = acc_hi + wk * lax.bitcast_convert_type(packed & jnp.uint32(0xFFFF0000), F32)
    g = gmod_ref[0]
    x_lo = x_ref[:, :half] + g[:, :half] * acc_lo
    x_hi = x_ref[:, half:] + g[:, half:] * acc_hi
    if not last:
        o_ref[:, :half] = x_lo
        o_ref[:, half:] = x_hi
    ms = (jnp.sum(x_lo * x_lo, axis=-1, keepdims=True) + jnp.sum(x_hi * x_hi, axis=-1, keepdims=True)) / (2 * half)
    r = lax.rsqrt(ms + EPS)
    ng = ng_ref[...]
    nsc = nsc_ref[0]
    nsh = nsh_ref[0]
    hn_ref[:, :half] = (x_lo * r * ng[:, :half] * (1.0 + nsc[:, :half]) + nsh[:, :half]).astype(hn_ref.dtype)
    hn_ref[:, half:] = (x_hi * r * ng[:, half:] * (1.0 + nsc[:, half:]) + nsh[:, half:]).astype(hn_ref.dtype)


def _combine(pos, x2, seq, gmod, h, wts_t, sg, su, sd, ys, per, next_g, next_sc, next_sh, last):
    t, d = x2.shape
    half = d // 2
    ds_ = sg.shape[1]
    tm = min(256, seq)
    tiles_per_batch = seq // tm
    kern = functools.partial(_combine_kernel, tm=tm, per=per, last=last)
    const = lambda shape: pl.BlockSpec(shape, lambda i: (0,) * len(shape))
    est = 6 * tm * d * 4 + 2 * tm * d * 2 + 2 * 3 * d * ds_ * 2 + TOP_K * tm * half * 4 + 8 * tm * d * 4
    tile = pl.BlockSpec((tm, d), lambda i: (i, 0))
    per_batch = pl.BlockSpec((1, 1, d), lambda i: (i // tiles_per_batch, 0, 0))
    if last:
        out_specs, out_shape = [tile], [jax.ShapeDtypeStruct((t, d), F32)]
    else:
        out_specs = [tile, tile]
        out_shape = [jax.ShapeDtypeStruct((t, d), F32), jax.ShapeDtypeStruct((t, d), BF16)]
    outs = pl.pallas_call(
        kern,
        grid=(t // tm,),
        in_specs=[
            pl.BlockSpec((TOP_K, tm), lambda i: (0, i), memory_space=pltpu.SMEM),
            tile,
            per_batch,
            tile,
            pl.BlockSpec((tm, TOP_K), lambda i: (i, 0)),
            const((d, ds_)), const((d, ds_)), const((ds_, d)),
            const((1, d)), per_batch, per_batch,
            pl.BlockSpec(memory_space=pl.ANY),
        ],
        out_specs=out_specs,
        out_shape=out_shape,
        scratch_shapes=[pltpu.VMEM((TOP_K * tm * per, LANES), U32), pltpu.SemaphoreType.DMA(())],
        compiler_params=_cparams(("arbitrary",), est),
        name="moe_combine",
    )(pos, x2, gmod, h, wts_t, sg, su, sd, next_g.reshape(1, d), next_sc, next_sh, ys)
    return (None, outs[0]) if last else (outs[0], outs[1])


def _moe_layout(counts, n_blocks):
    padded = (counts + MOE_BLK - 1) // MOE_BLK * MOE_BLK
    pends = jnp.cumsum(padded)
    pstart = (pends - padded).astype(I32)
    blk_row = jnp.arange(n_blocks, dtype=I32)[:, None] * MOE_BLK
    blk_exp = jnp.minimum(jnp.sum((pends[None, :] <= blk_row).astype(I32), axis=1), N_EXPERTS - 1)
    n_used = (pends[-1:] // MOE_BLK).astype(I32)
    return pstart, blk_exp, n_used


def _layer(x2, h1, bsz, seq, layer, mods, next_norm, last, w_in, lb, onorm_g, conv_w, rel_bias, w_a, w_b, w_c, w_out,
           norm2_g, router_w, router_bias, wg, wu, wd, sg, su, sd):
    t, d = x2.shape
    sh1, sc1, g1, sh2, sc2, g2 = mods
    n_col_blocks = w_in.shape[2] // A_DIM
    n_mix_blocks = n_col_blocks - (3 * d) // A_DIM
    pg = _project(h1, w_in, layer, tuple(range(n_mix_blocks, n_col_blocks)), "sigmoid", BF16, "proj_gates")
    af = _project(h1, w_in, layer, (1,), "none", F32, "proj_forget")
    p = _project(h1, w_in, layer, (4, 5, 0, 2, 3, 6, 7, 8), "none", BF16, "proj_mixers")
    b_blk = 0
    aq_blk = (3 * B_WIDTH) // A_DIM
    q_blk = aq_blk + 3
    a_feat = _hgrn(p, af, jnp.log(lb), jnp.log1p(-lb), onorm_g, bsz, seq, aq_blk)
    c_feat = _attention(p, rel_bias, bsz, seq, q_blk)
    x2 = _merge(x2, seq, g1, a_feat, c_feat, p, pg, conv_w, w_a.astype(BF16), w_b.astype(BF16), w_c.astype(BF16),
                w_out.astype(BF16), b_blk)

    h, hp, eidx, wts, rank, cnt = _router(x2, seq, norm2_g, sc2, sh2, router_w.T, router_bias)
    n_blocks = (t * TOP_K) // MOE_BLK + N_EXPERTS
    pstart, blk_exp, n_used = _moe_layout(cnt[:, 0], n_blocks)
    pos = _positions(pstart, eidx, rank)
    per = d // (2 * LANES)
    xs = _dispatch(pos, hp, n_blocks * MOE_BLK, per)
    ys = _experts(blk_exp, n_used, xs, wg, wu, wd, layer, per)
    return _combine(pos, x2, seq, g2, h, wts.T, sg.astype(BF16), su.astype(BF16), sd.astype(BF16), ys, per,
                    *next_norm, last)


def kernel(x, c, ada_w, ada_b, norm1_g, w_in, hgrn_lb_logits, hgrn_onorm_g, conv_w, rel_bias, w_branch_a, w_branch_b, w_branch_c, w_out, norm2_g, router_w, router_bias, moe_w_gate, moe_w_up, moe_w_down, shared_w_gate, shared_w_up, shared_w_down, final_g):
    bsz, seq, d = x.shape
    depth = ada_w.shape[0]
    mod = _modulation(c, ada_w, ada_b)
    lbs = jnp.cumsum(jax.nn.softmax(hgrn_lb_logits.astype(F32), axis=0), axis=0)
    lbs = lbs - lbs[0]
    mods = [[m.reshape(bsz, 1, d) for m in jnp.split(mod[l], N_MOD, axis=-1)] for l in range(depth)]
    x2 = x.reshape(bsz * seq, d)
    h1 = _norm_mod(x2, seq, norm1_g[0], mods[0][1], mods[0][0])
    no_mod = jnp.zeros((bsz, 1, d), F32)
    for l in range(depth):
        last = l == depth - 1
        next_norm = (final_g, no_mod, no_mod) if last else (norm1_g[l + 1], mods[l + 1][1], mods[l + 1][0])
        x2, h1 = _layer(x2, h1, bsz, seq, l, mods[l], next_norm, last, w_in, lbs[l], hgrn_onorm_g[l], conv_w[l],
                        rel_bias[l], w_branch_a[l], w_branch_b[l], w_branch_c[l], w_out[l], norm2_g[l], router_w[l],
                        router_bias[l], moe_w_gate, moe_w_up, moe_w_down, shared_w_gate[l], shared_w_up[l],
                        shared_w_down[l])
    return h1.reshape(bsz, seq, d)
```
